```python
import math
import jax, jax.numpy as jnp
from jax import lax
import numpy as np

D_MODEL = 2048
BATCH = 8
SEQ = 2048
DEPTH = 4

MOBA_HEADS = 16
MOBA_HEAD_DIM = 64
MOBA_DIM = MOBA_HEADS * MOBA_HEAD_DIM
MOBA_BLOCK = 256
MOBA_TOPK = 3
MOBA_Q_CHUNK = 16
RWKV_HEADS = 16
RWKV_HEAD_DIM = 64
RWKV_DIM = RWKV_HEADS * RWKV_HEAD_DIM
RWKV_DECAY_LORA = 64
RWKV_A_LORA = 64
RWKV_V_LORA = 32
RWKV_G_LORA = 160
RWKV_GN_EPS = 64e-5
RWKV_IN = 3 * RWKV_DIM + RWKV_DECAY_LORA + RWKV_A_LORA + RWKV_G_LORA
EVEN_IN = 3 * MOBA_DIM + RWKV_IN
EVEN_MIX = MOBA_DIM + RWKV_DIM
MLA_HEADS = 16
MLA_Q_LORA = 512
MLA_KV_LORA = 512
MLA_NOPE_DIM = 128
MLA_ROPE_DIM = 64
MLA_V_DIM = 128
MLA_Q_BLOCK = 128
ROPE_THETA = 10000.0
ODD_IN = MLA_Q_LORA + MLA_KV_LORA + MLA_ROPE_DIM
ODD_MIX = MLA_HEADS * MLA_V_DIM
N_EXPERTS = 32
TOP_K = 4
D_EXPERT = 1024
SWIGLU_ALPHA = 1.702
SWIGLU_LIMIT = 7.0
MOE_ROW_BLOCK = 128
N_EVEN = (DEPTH + 1) // 2
N_ODD = DEPTH // 2
DEEPNORM_ALPHA = (2 * DEPTH) ** 0.25
DEEPNORM_BETA = (8 * DEPTH) ** -0.25
LN_EPS = 1e-5
RMS_EPS = 1e-6
NEG_INF = -1e30

kernel_name = "hybrid_moba_rwkv7_mla_moe_deepnorm"


def layer_norm(x, g, b):
    xf = x.astype(jnp.float32)
    mu = jnp.mean(xf, axis=-1, keepdims=True)
    var = jnp.mean(jnp.square(xf - mu), axis=-1, keepdims=True)
    return ((xf - mu) * lax.rsqrt(var + LN_EPS) * g.astype(jnp.float32) + b.astype(jnp.float32)).astype(x.dtype)


def rms_norm(x, g):
    xf = x.astype(jnp.float32)
    ms = jnp.mean(jnp.square(xf), axis=-1, keepdims=True)
    return (xf * lax.rsqrt(ms + RMS_EPS) * g.astype(jnp.float32)).astype(x.dtype)


def token_shift(z):
    return jnp.pad(z, ((0, 0), (1, 0), (0, 0)))[:, :-1]


def rope_tables(S, dim, dtype):
    inv = ROPE_THETA ** (-jnp.arange(0, dim, 2, dtype=jnp.float32) / dim)
    ang = jnp.arange(S, dtype=jnp.float32)[:, None] * inv[None, :]
    return jnp.cos(ang).astype(dtype), jnp.sin(ang).astype(dtype)


def apply_rope(x, cos, sin):
    x1, x2 = jnp.split(x, 2, axis=-1)
    return jnp.concatenate([x1 * cos - x2 * sin, x1 * sin + x2 * cos], axis=-1)


def moba_attention(q, k, v):
    B, H, S, Dh = q.shape
    nb = -(-S // MOBA_BLOCK)
    pad = nb * MOBA_BLOCK - S
    kb = jnp.pad(k, ((0, 0), (0, 0), (0, pad), (0, 0))).reshape(B, H, nb, MOBA_BLOCK, Dh)
    vb = jnp.pad(v, ((0, 0), (0, 0), (0, pad), (0, 0))).reshape(B, H, nb, MOBA_BLOCK, Dh)
    k_mean = jnp.mean(kb.astype(jnp.float32), axis=3)
    n_sel = min(MOBA_TOPK, nb)
    slopes = 2.0 ** (-8.0 * jnp.arange(1, H + 1, dtype=jnp.float32) / H)
    scale = Dh ** -0.5
    b_ix = jnp.arange(B)[:, None, None, None]
    h_ix = jnp.arange(H)[None, :, None, None]
    blk_pos = jnp.arange(MOBA_BLOCK)
    qc_len = MOBA_Q_CHUNK

    def chunk(c):
        t0 = c * qc_len
        qc = lax.dynamic_slice_in_dim(q, t0, qc_len, axis=2)
        t = t0 + jnp.arange(qc_len)
        cur = t0 // MOBA_BLOCK
        gate = jnp.einsum('bhqd,bhnd->bhqn', qc.astype(jnp.float32), k_mean)
        gate = jnp.where(jnp.arange(nb) < cur, gate, NEG_INF)
        _, idx = lax.top_k(gate, n_sel)
        valid = idx < cur
        k_sel = kb[b_ix, h_ix, idx]
        v_sel = vb[b_ix, h_ix, idx]
        dist_sel = (t[:, None, None] - (idx[..., None] * MOBA_BLOCK + blk_pos)).astype(jnp.float32)
        s_sel = (jnp.einsum('bhqd,bhqrkd->bhqrk', qc, k_sel).astype(jnp.float32) * scale
                 - slopes[:, None, None, None] * dist_sel)
        s_sel = jnp.where(valid[..., None], s_sel, NEG_INF)
        k_cur = lax.dynamic_index_in_dim(kb, cur, axis=2, keepdims=False)
        v_cur = lax.dynamic_index_in_dim(vb, cur, axis=2, keepdims=False)
        dist_cur = t[:, None] - (cur * MOBA_BLOCK + blk_pos)[None, :]
        s_cur = (jnp.einsum('bhqd,bhkd->bhqk', qc, k_cur).astype(jnp.float32) * scale
                 - slopes[:, None, None] * dist_cur.astype(jnp.float32))
        s_cur = jnp.where(dist_cur >= 0, s_cur, NEG_INF)
        s = jnp.concatenate([s_sel.reshape(B, H, qc_len, n_sel * MOBA_BLOCK), s_cur], axis=-1)
        p = jax.nn.softmax(s, axis=-1).astype(v.dtype)
        p_sel = p[..., :n_sel * MOBA_BLOCK].reshape(B, H, qc_len, n_sel, MOBA_BLOCK)
        p_cur = p[..., n_sel * MOBA_BLOCK:]
        return (jnp.einsum('bhqrk,bhqrkd->bhqd', p_sel, v_sel)
                + jnp.einsum('bhqk,bhkd->bhqd', p_cur, v_cur))

    out = lax.map(chunk, jnp.arange(S // qc_len))
    return out.transpose(1, 0, 3, 2, 4).reshape(B, S, H * Dh)


def rwkv7_scan(r, decay, k, v, a, b):
    B, S, H, N = r.shape

    def step(state, inp):
        r_t, w_t, k_t, v_t, a_t, b_t = inp
        sa = jnp.einsum('bhij,bhj->bhi', state, a_t)
        state = (state * w_t[:, :, None, :] + sa[..., None] * b_t[:, :, None, :]
                 + v_t[..., None] * k_t[:, :, None, :])
        return state, jnp.einsum('bhij,bhj->bhi', state, r_t)

    xs = tuple(jnp.moveaxis(z.astype(jnp.float32), 1, 0) for z in (r, decay, k, v, a, b))
    state0 = jnp.zeros((B, H, N, N), jnp.float32)
    _, y = lax.scan(step, state0, xs)
    return jnp.moveaxis(y, 0, 1)


def rwkv7_mix(p, mu, w0, w2, a0, a2, g2, k_k, k_a, r_k, ln_w, ln_b, v_first, v_lora):
    B, S, _ = p.shape
    C, H, N = RWKV_DIM, RWKV_HEADS, RWKV_HEAD_DIM
    p = p + (token_shift(p) - p) * mu
    r, k, v = p[..., :C], p[..., C:2 * C], p[..., 2 * C:3 * C]
    o = 3 * C
    wd = p[..., o:o + RWKV_DECAY_LORA]
    ad = p[..., o + RWKV_DECAY_LORA:o + RWKV_DECAY_LORA + RWKV_A_LORA]
    gd = p[..., o + RWKV_DECAY_LORA + RWKV_A_LORA:]
    w = -jax.nn.softplus(-(w0 + jnp.tanh(wd) @ w2)) - 0.5
    decay = jnp.exp(-jnp.exp(w.astype(jnp.float32)))
    a = jax.nn.sigmoid(a0 + ad @ a2)
    g = jax.nn.sigmoid(gd) @ g2
    if v_lora is not None:
        v0, v1, v2 = v_lora
        v = v + (v_first - v) * jax.nn.sigmoid(v0 + (v @ v1) @ v2)
    heads = lambda z: z.reshape(B, S, H, N)
    kk = heads(k * k_k).astype(jnp.float32)
    kk = kk * lax.rsqrt(jnp.maximum(jnp.sum(jnp.square(kk), axis=-1, keepdims=True), 1e-24))
    k = k * (1.0 + (a - 1.0) * k_a)
    a_h = heads(a).astype(jnp.float32)
    y = rwkv7_scan(heads(r), heads(decay), heads(k), heads(v), -kk, kk * a_h)
    y_mu = jnp.mean(y, axis=-1, keepdims=True)
    y_var = jnp.mean(jnp.square(y - y_mu), axis=-1, keepdims=True)
    y = ((y - y_mu) * lax.rsqrt(y_var + RWKV_GN_EPS) * ln_w.reshape(H, N).astype(jnp.float32)
         + ln_b.reshape(H, N).astype(jnp.float32))
    bonus = jnp.sum((heads(r) * heads(k) * r_k).astype(jnp.float32), axis=-1, keepdims=True)
    y = y + bonus * heads(v).astype(jnp.float32)
    out = (y.reshape(B, S, C) * g.astype(jnp.float32)).astype(p.dtype)
    return out, v


def mla_attention(h, w_in, q_norm, kv_norm, w_uq, w_ukv):
    B, S, _ = h.shape
    H = MLA_HEADS
    proj = h @ w_in
    c_q = proj[..., :MLA_Q_LORA]
    c_kv = proj[..., MLA_Q_LORA:MLA_Q_LORA + MLA_KV_LORA]
    k_rope = proj[..., MLA_Q_LORA + MLA_KV_LORA:]
    q = (rms_norm(c_q, q_norm) @ w_uq).reshape(B, S, H, MLA_NOPE_DIM + MLA_ROPE_DIM)
    kv = (rms_norm(c_kv, kv_norm) @ w_ukv).reshape(B, S, H, MLA_NOPE_DIM + MLA_V_DIM)
    q_nope, q_rope = q[..., :MLA_NOPE_DIM], q[..., MLA_NOPE_DIM:]
    k_nope, v = kv[..., :MLA_NOPE_DIM], kv[..., MLA_NOPE_DIM:]
    cos, sin = rope_tables(S, MLA_ROPE_DIM, h.dtype)
    q_rope = apply_rope(q_rope, cos[:, None, :], sin[:, None, :])
    k_rope = apply_rope(k_rope, cos, sin)
    scale = (MLA_NOPE_DIM + MLA_ROPE_DIM) ** -0.5
    k_pos = jnp.arange(S)

    def block(i):
        t0 = i * MLA_Q_BLOCK
        qn = lax.dynamic_slice_in_dim(q_nope, t0, MLA_Q_BLOCK, axis=1)
        qr = lax.dynamic_slice_in_dim(q_rope, t0, MLA_Q_BLOCK, axis=1)
        s = (jnp.einsum('bqhd,bkhd->bhqk', qn, k_nope)
             + jnp.einsum('bqhd,bkd->bhqk', qr, k_rope)).astype(jnp.float32) * scale
        q_pos = t0 + jnp.arange(MLA_Q_BLOCK)
        s = jnp.where(k_pos[None, :] <= q_pos[:, None], s, NEG_INF)
        p = jax.nn.softmax(s, axis=-1).astype(v.dtype)
        return jnp.einsum('bhqk,bkhd->bqhd', p, v)

    o = lax.map(block, jnp.arange(S // MLA_Q_BLOCK))
    return o.transpose(1, 0, 2, 3, 4).reshape(B, S, H * MLA_V_DIM)


def moe_ffn(h, w_r, b_r, w1, b1, w2, b2):
    B, S, D = h.shape
    x = h.reshape(-1, D)
    T = x.shape[0]
    R = MOE_ROW_BLOCK
    logits = x.astype(jnp.float32) @ w_r.astype(jnp.float32) + b_r.astype(jnp.float32)
    top_val, top_idx = lax.top_k(logits, TOP_K)
    gates = jax.nn.softmax(top_val, axis=-1)
    M = T * TOP_K
    flat_e = top_idx.reshape(M)
    flat_tok = jnp.arange(M) // TOP_K
    flat_g = gates.reshape(M)
    order = jnp.argsort(flat_e)
    e_sorted, tok_sorted, g_sorted = flat_e[order], flat_tok[order], flat_g[order]
    counts = jnp.bincount(flat_e, length=N_EXPERTS)
    padded = (counts + R - 1) // R * R
    pad_end = jnp.cumsum(padded)
    pad_start = pad_end - padded
    start = jnp.cumsum(counts) - counts
    dest = pad_start[e_sorted] + (jnp.arange(M) - start[e_sorted])
    n_blocks = (M + N_EXPERTS * (R - 1) + R - 1) // R
    rows = n_blocks * R
    xbuf = jnp.zeros((rows, D), x.dtype).at[dest].set(x[tok_sorted])
    block_e = jnp.minimum(jnp.searchsorted(pad_end, jnp.arange(n_blocks) * R, side='right'), N_EXPERTS - 1)

    def expert_block(args):
        xb, e = args
        hgu = xb @ w1[e] + b1[e]
        gate = jnp.minimum(hgu[:, :D_EXPERT], SWIGLU_LIMIT)
        up = jnp.clip(hgu[:, D_EXPERT:], -SWIGLU_LIMIT, SWIGLU_LIMIT)
        act = gate * jax.nn.sigmoid(SWIGLU_ALPHA * gate) * (up + 1.0)
        return act @ w2[e] + b2[e]

    ybuf = lax.map(expert_block, (xbuf.reshape(n_blocks, R, D), block_e)).reshape(rows, D)
    y = ybuf[dest] * g_sorted[:, None].astype(x.dtype)
    out = jnp.zeros((T, D), x.dtype).at[tok_sorted].add(y)
    return out.reshape(B, S, D)


def setup_inputs(seed: int = 0) -> dict:
    key = jax.random.key(seed)
    ks = iter(jax.random.split(key, 40))

    def nrm(shape, scale):
        return jax.random.normal(next(ks), shape, jnp.float32) * scale

    def unif(shape, lo, hi):
        return jax.random.uniform(next(ks), shape, jnp.float32, lo, hi)

    D, C = D_MODEL, RWKV_DIM
    NV = max(N_EVEN - 1, 0)
    return {
        'x': nrm((BATCH, SEQ, D), 1.0),
        'ev_w_in': nrm((N_EVEN, D, EVEN_IN), D ** -0.5),
        'ev_w_out': nrm((N_EVEN, EVEN_MIX, D), EVEN_MIX ** -0.5 * DEEPNORM_BETA),
        'rw_mu': unif((N_EVEN, RWKV_IN), 0.0, 1.0),
        'rw_w0': unif((N_EVEN, C), -5.0, -0.5),
        'rw_w2': nrm((N_EVEN, RWKV_DECAY_LORA, C), 0.1 * RWKV_DECAY_LORA ** -0.5),
        'rw_a0': nrm((N_EVEN, C), 0.1),
        'rw_a2': nrm((N_EVEN, RWKV_A_LORA, C), 0.1 * RWKV_A_LORA ** -0.5),
        'rw_g2': nrm((N_EVEN, RWKV_G_LORA, C), RWKV_G_LORA ** -0.5),
        'rw_k_k': 0.85 + nrm((N_EVEN, C), 0.05),
        'rw_k_a': 1.0 + nrm((N_EVEN, C), 0.05),
        'rw_r_k': nrm((N_EVEN, RWKV_HEADS, RWKV_HEAD_DIM), 0.1),
        'rw_ln_w': 1.0 + nrm((N_EVEN, C), 0.05),
        'rw_ln_b': nrm((N_EVEN, C), 0.01),
        'rw_v0': 1.0 + nrm((NV, C), 0.1),
        'rw_v1': nrm((NV, C, RWKV_V_LORA), C ** -0.5),
        'rw_v2': nrm((NV, RWKV_V_LORA, C), 0.1 * RWKV_V_LORA ** -0.5),
        'od_w_in': nrm((N_ODD, D, ODD_IN), D ** -0.5),
        'od_q_norm': 1.0 + nrm((N_ODD, MLA_Q_LORA), 0.05),
        'od_kv_norm': 1.0 + nrm((N_ODD, MLA_KV_LORA), 0.05),
        'od_w_uq': nrm((N_ODD, MLA_Q_LORA, MLA_HEADS * (MLA_NOPE_DIM + MLA_ROPE_DIM)), MLA_Q_LORA ** -0.5),
        'od_w_ukv': nrm((N_ODD, MLA_KV_LORA, MLA_HEADS * (MLA_NOPE_DIM + MLA_V_DIM)), MLA_KV_LORA ** -0.5),
        'od_w_out': nrm((N_ODD, ODD_MIX, D), ODD_MIX ** -0.5 * DEEPNORM_BETA),
        'ln_mix_g': 1.0 + nrm((DEPTH, D), 0.05),
        'ln_mix_b': nrm((DEPTH, D), 0.01),
        'ln_ffn_g': 1.0 + nrm((DEPTH, D), 0.05),
        'ln_ffn_b': nrm((DEPTH, D), 0.01),
        'moe_w_r': nrm((DEPTH, D, N_EXPERTS), D ** -0.5),
        'moe_b_r': nrm((DEPTH, N_EXPERTS), 0.01),
        'moe_w1': nrm((DEPTH, N_EXPERTS, D, 2 * D_EXPERT), D ** -0.5),
        'moe_b1': nrm((DEPTH, N_EXPERTS, 2 * D_EXPERT), 0.01),
        'moe_w2': nrm((DEPTH, N_EXPERTS, D_EXPERT, D), D_EXPERT ** -0.5 * DEEPNORM_BETA),
        'moe_b2': nrm((DEPTH, N_EXPERTS, D), 0.01),
    }


def reference(x, ev_w_in, ev_w_out, rw_mu, rw_w0, rw_w2, rw_a0, rw_a2, rw_g2, rw_k_k, rw_k_a,
              rw_r_k, rw_ln_w, rw_ln_b, rw_v0, rw_v1, rw_v2, od_w_in, od_q_norm, od_kv_norm,
              od_w_uq, od_w_ukv, od_w_out, ln_mix_g, ln_mix_b, ln_ffn_g, ln_ffn_b,
              moe_w_r, moe_b_r, moe_w1, moe_b1, moe_w2, moe_b2):
    B, S, _ = x.shape
    h = x
    v_first = None
    for layer in range(DEPTH):
        j = layer // 2
        if layer % 2 == 0:
            proj = h @ ev_w_in[j]
            to_heads = lambda z: z.reshape(B, S, MOBA_HEADS, MOBA_HEAD_DIM).transpose(0, 2, 1, 3)
            q = to_heads(proj[..., :MOBA_DIM])
            k = to_heads(proj[..., MOBA_DIM:2 * MOBA_DIM])
            v = to_heads(proj[..., 2 * MOBA_DIM:3 * MOBA_DIM])
            a_out = moba_attention(q, k, v)
            v_lora = None if j == 0 else (rw_v0[j - 1], rw_v1[j - 1], rw_v2[j - 1])
            b_out, rw_val = rwkv7_mix(proj[..., 3 * MOBA_DIM:], rw_mu[j], rw_w0[j], rw_w2[j],
                                      rw_a0[j], rw_a2[j], rw_g2[j], rw_k_k[j], rw_k_a[j],
                                      rw_r_k[j], rw_ln_w[j], rw_ln_b[j], v_first, v_lora)
            if j == 0:
                v_first = rw_val
            mix = jnp.concatenate([a_out, b_out], axis=-1) @ ev_w_out[j]
        else:
            mix = mla_attention(h, od_w_in[j], od_q_norm[j], od_kv_norm[j],
                                od_w_uq[j], od_w_ukv[j]) @ od_w_out[j]
        h = layer_norm(DEEPNORM_ALPHA * h + mix, ln_mix_g[layer], ln_mix_b[layer])
        ffn = moe_ffn(h, moe_w_r[layer], moe_b_r[layer], moe_w1[layer], moe_b1[layer],
                      moe_w2[layer], moe_b2[layer])
        h = layer_norm(DEEPNORM_ALPHA * h + ffn, ln_ffn_g[layer], ln_ffn_b[layer])
    return h
```

```python
import functools
import math

import jax
import jax.numpy as jnp
import numpy as np
from jax import lax
from jax.experimental import pallas as pl
from jax.experimental.pallas import tpu as pltpu

F32 = jnp.float32
BF16 = jnp.bfloat16
HIGHEST = lax.Precision.HIGHEST

DEPTH = 4
MOBA_HEADS = 16
MOBA_HEAD_DIM = 64
MOBA_DIM = MOBA_HEADS * MOBA_HEAD_DIM
MOBA_BLOCK = 256
MOBA_TOPK = 3
RWKV_HEADS = 16
RWKV_HEAD_DIM = 64
RWKV_DIM = RWKV_HEADS * RWKV_HEAD_DIM
RWKV_DECAY_LORA = 64
RWKV_A_LORA = 64
RWKV_G_LORA = 160
RWKV_LORA = RWKV_DECAY_LORA + RWKV_A_LORA + RWKV_G_LORA
RWKV_GN_EPS = 64e-5
RWKV_CHUNK = 64
MLA_HEADS = 16
MLA_Q_LORA = 512
MLA_KV_LORA = 512
MLA_NOPE_DIM = 128
MLA_ROPE_DIM = 64
MLA_V_DIM = 128
ROPE_THETA = 10000.0
N_EXPERTS = 32
TOP_K = 4
D_EXPERT = 1024
SWIGLU_ALPHA = 1.702
SWIGLU_LIMIT = 7.0
MOE_ROWS = 256
DEEPNORM_ALPHA = (2 * DEPTH) ** 0.25
LN_EPS = 1e-5
RMS_EPS = 1e-6
NEG_INF = -1e30

LANES = 128
VMEM_LIMIT = 56 * 1024 * 1024


def _cparams(*sem):
    return pltpu.CompilerParams(dimension_semantics=sem, vmem_limit_bytes=VMEM_LIMIT)


def _dot(a, b, precision=None):
    return jnp.dot(a, b, preferred_element_type=F32, precision=precision)


def _dot_nt(a, b, precision=None):
    return lax.dot_general(a, b, (((1,), (1,)), ((), ())), preferred_element_type=F32,
                           precision=precision)


def _dot_tn(a, b, precision=None):
    return lax.dot_general(a, b, (((0,), (0,)), ((), ())), preferred_element_type=F32,
                           precision=precision)


def _mm_kernel(a_ref, w_ref, o_ref):
    o_ref[...] = _dot(a_ref[...], w_ref[...]).astype(o_ref.dtype)


def matmul(a, w, out_dtype, tm=1024, tn=512):
    M, K = a.shape
    N = w.shape[1]
    tm, tn = min(tm, M), min(tn, N)
    assert M % tm == 0 and N % tn == 0
    return pl.pallas_call(
        _mm_kernel,
        grid=(M // tm, N // tn),
        in_specs=[pl.BlockSpec((tm, K), lambda i, j: (i, 0)),
                  pl.BlockSpec((K, tn), lambda i, j: (0, j))],
        out_specs=pl.BlockSpec((tm, tn), lambda i, j: (i, j)),
        out_shape=jax.ShapeDtypeStruct((M, N), out_dtype),
        compiler_params=_cparams("parallel", "parallel"),
        name="matmul",
    )(a, w)


def _layer_norm_rows(z, g, b):
    mu = jnp.mean(z, axis=-1, keepdims=True)
    zc = z - mu
    var = jnp.mean(zc * zc, axis=-1, keepdims=True)
    return zc * lax.rsqrt(var + LN_EPS) * g + b


def _proj_ln_kernel(n_in, *refs):
    a_refs = refs[:n_in]
    w_refs = refs[n_in:2 * n_in]
    h_ref, g_ref, b_ref, o32_ref, o16_ref = refs[2 * n_in:]
    acc = DEEPNORM_ALPHA * h_ref[...]
    for a_ref, w_ref in zip(a_refs, w_refs):
        acc = acc + _dot(a_ref[...], w_ref[...])
    y = _layer_norm_rows(acc, g_ref[...], b_ref[...])
    o32_ref[...] = y
    o16_ref[...] = y.astype(BF16)


def proj_residual_ln(a_list, w_list, h, g, b, tm=256):
    M, D = h.shape
    n_in = len(a_list)
    in_specs = ([pl.BlockSpec((tm, a.shape[1]), lambda i: (i, 0)) for a in a_list]
                + [pl.BlockSpec(w.shape, lambda i: (0, 0)) for w in w_list]
                + [pl.BlockSpec((tm, D), lambda i: (i, 0)),
                   pl.BlockSpec((1, D), lambda i: (0, 0)),
                   pl.BlockSpec((1, D), lambda i: (0, 0))])
    return pl.pallas_call(
        functools.partial(_proj_ln_kernel, n_in),
        grid=(M // tm,),
        in_specs=in_specs,
        out_specs=[pl.BlockSpec((tm, D), lambda i: (i, 0)),
                   pl.BlockSpec((tm, D), lambda i: (i, 0))],
        out_shape=[jax.ShapeDtypeStruct((M, D), F32), jax.ShapeDtypeStruct((M, D), BF16)],
        compiler_params=_cparams("parallel"),
        name="proj_residual_ln",
    )(*a_list, *w_list, h, g.reshape(1, D), b.reshape(1, D))


def _combine_ln_kernel(y_ref, gate_ref, h_ref, g_ref, b_ref, o32_ref, o16_ref):
    acc = DEEPNORM_ALPHA * h_ref[...]
    gates = gate_ref[...]
    for k in range(TOP_K):
        acc = acc + y_ref[k] * gates[:, k:k + 1]
    y = _layer_norm_rows(acc, g_ref[...], b_ref[...])
    o32_ref[...] = y
    o16_ref[...] = y.astype(BF16)


def combine_residual_ln(y4, gates, h, g, b, tm=256):
    M, D = h.shape
    return pl.pallas_call(
        _combine_ln_kernel,
        grid=(M // tm,),
        in_specs=[pl.BlockSpec((TOP_K, tm, D), lambda i: (0, i, 0)),
                  pl.BlockSpec((tm, TOP_K), lambda i: (i, 0)),
                  pl.BlockSpec((tm, D), lambda i: (i, 0)),
                  pl.BlockSpec((1, D), lambda i: (0, 0)),
                  pl.BlockSpec((1, D), lambda i: (0, 0))],
        out_specs=[pl.BlockSpec((tm, D), lambda i: (i, 0)),
                   pl.BlockSpec((tm, D), lambda i: (i, 0))],
        out_shape=[jax.ShapeDtypeStruct((M, D), F32), jax.ShapeDtypeStruct((M, D), BF16)],
        compiler_params=_cparams("parallel"),
        name="combine_residual_ln",
    )(y4, gates, h, g.reshape(1, D), b.reshape(1, D))


def _router_kernel(h_ref, w_ref, b_ref, o_ref):
    o_ref[...] = _dot(h_ref[...], w_ref[...], precision=HIGHEST) + b_ref[...]


def router_logits(h, w_r, b_r, tm=512):
    M, D = h.shape
    w = jnp.zeros((D, LANES), F32).at[:, :N_EXPERTS].set(w_r)
    b = jnp.zeros((1, LANES), F32).at[0, :N_EXPERTS].set(b_r)
    out = pl.pallas_call(
        _router_kernel,
        grid=(M // tm,),
        in_specs=[pl.BlockSpec((tm, D), lambda i: (i, 0)),
                  pl.BlockSpec((D, LANES), lambda i: (0, 0)),
                  pl.BlockSpec((1, LANES), lambda i: (0, 0))],
        out_specs=pl.BlockSpec((tm, LANES), lambda i: (i, 0)),
        out_shape=jax.ShapeDtypeStruct((M, LANES), F32),
        compiler_params=_cparams("parallel"),
        name="router_logits",
    )(h, w, b)
    return out[:, :N_EXPERTS]


def _expert_kernel(be_ref, nv_ref, x_ref, w1_ref, b1_ref, w2_ref, b2_ref, o_ref):
    i = pl.program_id(0)

    @pl.when(i < nv_ref[0])
    def _():
        hgu = _dot(x_ref[...], w1_ref[0]) + b1_ref[0]
        gate = jnp.minimum(hgu[:, :D_EXPERT], SWIGLU_LIMIT)
        up = jnp.clip(hgu[:, D_EXPERT:], -SWIGLU_LIMIT, SWIGLU_LIMIT)
        act = gate * jax.nn.sigmoid(SWIGLU_ALPHA * gate) * (up + 1.0)
        o_ref[...] = _dot(act.astype(BF16), w2_ref[0]) + b2_ref[0]

    @pl.when(i >= nv_ref[0])
    def _():
        o_ref[...] = jnp.zeros_like(o_ref)


def expert_ffn(xbuf, block_e, n_valid, w1, b1, w2, b2):
    rows, D = xbuf.shape
    R = MOE_ROWS
    n_blocks = rows // R
    grid_spec = pltpu.PrefetchScalarGridSpec(
        num_scalar_prefetch=2,
        grid=(n_blocks,),
        in_specs=[pl.BlockSpec((R, D), lambda i, be, nv: (i, 0)),
                  pl.BlockSpec((1, D, 2 * D_EXPERT), lambda i, be, nv: (be[i], 0, 0)),
                  pl.BlockSpec((1, 1, 2 * D_EXPERT), lambda i, be, nv: (be[i], 0, 0)),
                  pl.BlockSpec((1, D_EXPERT, D), lambda i, be, nv: (be[i], 0, 0)),
                  pl.BlockSpec((1, 1, D), lambda i, be, nv: (be[i], 0, 0))],
        out_specs=pl.BlockSpec((R, D), lambda i, be, nv: (i, 0)),
    )
    return pl.pallas_call(
        _expert_kernel,
        grid_spec=grid_spec,
        out_shape=jax.ShapeDtypeStruct((rows, D), F32),
        compiler_params=_cparams("arbitrary"),
        name="expert_ffn",
    )(block_e, n_valid, xbuf, w1, b1.reshape(N_EXPERTS, 1, -1), w2, b2.reshape(N_EXPERTS, 1, -1))


def moe_layer(h32, h16, w_r, b_r, w1, b1, w2, b2, ln_g, ln_b):
    T, D = h32.shape
    R = MOE_ROWS
    M = T * TOP_K
    logits = router_logits(h32, w_r, b_r)
    top_val, top_idx = lax.top_k(logits, TOP_K)
    gates = jax.nn.softmax(top_val, axis=-1)
    flat_e = top_idx.reshape(M)
    onehot = (flat_e[:, None] == jnp.arange(N_EXPERTS)[None, :]).astype(jnp.int32)
    csum = jnp.cumsum(onehot, axis=0)
    counts = csum[-1]
    rank = jnp.take_along_axis(csum, flat_e[:, None], axis=1)[:, 0] - 1
    padded = (counts + R - 1) // R * R
    pad_end = jnp.cumsum(padded)
    pad_start = pad_end - padded
    pos = pad_start[flat_e] + rank
    n_blocks = (M + N_EXPERTS * (R - 1) + R - 1) // R
    rows = n_blocks * R
    src_tok = jnp.full((rows,), T, jnp.int32).at[pos].set(jnp.arange(M, dtype=jnp.int32) // TOP_K)
    block_e = jnp.minimum(jnp.searchsorted(pad_end, jnp.arange(n_blocks) * R, side='right'),
                          N_EXPERTS - 1).astype(jnp.int32)
    n_valid = (pad_end[-1] // R).astype(jnp.int32).reshape(1)
    x_pad = jnp.concatenate([h16, jnp.zeros((1, D), BF16)], axis=0)
    xbuf = x_pad[src_tok]
    ybuf = expert_ffn(xbuf, block_e, n_valid, w1, b1, w2, b2)
    y4 = ybuf[pos.reshape(T, TOP_K).T]
    return combine_residual_ln(y4, gates, h32, ln_g, ln_b)


def _moba_kernel(q_ref, k_ref, v_ref, o_ref, kmean_ref):
    p = pl.program_id(1)
    cur = pl.program_id(2)
    BLK = MOBA_BLOCK
    nb = k_ref.shape[0] // BLK
    lane = lax.broadcasted_iota(jnp.int32, (1, LANES), 1)
    head0 = lane < MOBA_HEAD_DIM

    @pl.when(cur == 0)
    def _():
        kmean_ref[...] = jnp.zeros_like(kmean_ref)
        for n in range(nb):
            kmean_ref[n:n + 1, :] = jnp.mean(k_ref[n * BLK:(n + 1) * BLK, :], axis=0, keepdims=True)

    q = q_ref[...]
    qs = jnp.concatenate([jnp.where(head0, q, 0.0), jnp.where(head0, 0.0, q)], axis=0)
    row = lax.broadcasted_iota(jnp.int32, (2 * BLK, 1), 0)
    hh = (row >= BLK).astype(jnp.int32)
    head_idx = 2 * p + hh + 1
    slope = jnp.exp(head_idx.astype(F32) * (-8.0 * math.log(2.0) / MOBA_HEADS))

    gate = _dot_nt(qs, kmean_ref[...], precision=HIGHEST)
    lane_b = lax.broadcasted_iota(jnp.int32, (2 * BLK, LANES), 1)
    rank = jnp.zeros((2 * BLK, LANES), jnp.int32)
    for m in range(nb):
        gm = gate[:, m:m + 1]
        ahead = (gm > gate) | ((gm == gate) & (m < lane_b))
        rank = rank + jnp.where(ahead, 1, 0) * (m < cur).astype(jnp.int32)
    sel = (rank < MOBA_TOPK) & (lane_b < cur)

    scale = MOBA_HEAD_DIM ** -0.5
    qb = qs.astype(BF16)
    r_in = jnp.where(row >= BLK, row - BLK, row)
    col = lax.broadcasted_iota(jnp.int32, (1, BLK), 1)
    rel = (r_in - col).astype(F32)

    k_cur = k_ref[pl.ds(pl.multiple_of(cur * BLK, BLK), BLK), :].astype(BF16)
    v_cur = v_ref[pl.ds(pl.multiple_of(cur * BLK, BLK), BLK), :].astype(BF16)
    s = _dot_nt(qb, k_cur) * scale - slope * rel
    s = jnp.where(rel >= 0, s, NEG_INF)
    m_run = jnp.max(s, axis=-1, keepdims=True)
    pr = jnp.exp(s - m_run)
    l_run = jnp.sum(pr, axis=-1, keepdims=True)
    acc = _dot(pr.astype(BF16), v_cur)

    def body(n, carry):
        m_run, l_run, acc = carry
        off = pl.multiple_of(n * BLK, BLK)
        k_n = k_ref[pl.ds(off, BLK), :].astype(BF16)
        v_n = v_ref[pl.ds(off, BLK), :].astype(BF16)
        sel_n = jnp.sum(jnp.where(sel & (lane_b == n), 1.0, 0.0), axis=-1, keepdims=True) > 0.5
        dist = rel + ((cur - n) * BLK).astype(F32)
        s = _dot_nt(qb, k_n) * scale - slope * dist
        s = jnp.where(sel_n, s, NEG_INF)
        m_new = jnp.maximum(m_run, jnp.max(s, axis=-1, keepdims=True))
        alpha = jnp.exp(m_run - m_new)
        pr = jnp.exp(s - m_new)
        l_new = alpha * l_run + jnp.sum(pr, axis=-1, keepdims=True)
        acc_new = alpha * acc + _dot(pr.astype(BF16), v_n)
        return m_new, l_new, acc_new

    m_run, l_run, acc = lax.fori_loop(0, cur, body, (m_run, l_run, acc))
    out = acc / l_run
    o_ref[...] = jnp.where(head0, out[:BLK], out[BLK:]).astype(o_ref.dtype)


def moba_attention(proj, B, S):
    T = B * S
    BLK = MOBA_BLOCK
    n_pairs = MOBA_DIM // LANES
    nq = S // BLK
    return pl.pallas_call(
        _moba_kernel,
        grid=(B, n_pairs, nq),
        in_specs=[pl.BlockSpec((BLK, LANES), lambda b, p, c: (b * nq + c, p)),
                  pl.BlockSpec((S, LANES), lambda b, p, c: (b, n_pairs + p)),
                  pl.BlockSpec((S, LANES), lambda b, p, c: (b, 2 * n_pairs + p))],
        out_specs=pl.BlockSpec((BLK, LANES), lambda b, p, c: (b * nq + c, p)),
        out_shape=jax.ShapeDtypeStruct((T, MOBA_DIM), BF16),
        scratch_shapes=[pltpu.VMEM((LANES, LANES), F32)],
        compiler_params=_cparams("parallel", "parallel", "arbitrary"),
        name="moba_attention",
    )(proj, proj, proj)


def _rwkv_prep_kernel(has_vres, S, *refs):
    if has_vres:
        (pm_ref, pl_ref, pm_prev_ref, pl_prev_ref, mu_m_ref, mu_l_ref, w0_ref, w2_ref, a0_ref, a2_ref,
         g2_ref, kk_ref, ka_ref, vfirst_ref, v0_ref, v1_ref, v2_ref,
         r_o, lw_o, k_o, v_o, kkn_o, a_o, g_o) = refs
    else:
        (pm_ref, pl_ref, pm_prev_ref, pl_prev_ref, mu_m_ref, mu_l_ref, w0_ref, w2_ref, a0_ref, a2_ref,
         g2_ref, kk_ref, ka_ref,
         r_o, lw_o, k_o, v_o, kkn_o, a_o, g_o) = refs
    i = pl.program_id(0)
    tm = pm_ref.shape[0]
    C = RWKV_DIM
    row = lax.broadcasted_iota(jnp.int32, (tm, 1), 0)
    seq_start = (i * tm) % S == 0

    def shifted(cur_ref, prev_ref, mu_ref):
        x = cur_ref[...]
        prev_row = jnp.where(seq_start, 0.0, prev_ref[7:8, :])
        xs = jnp.where(row == 0, prev_row, pltpu.roll(x, 1, 0))
        return x + (xs - x) * mu_ref[...]

    pm = shifted(pm_ref, pm_prev_ref, mu_m_ref)
    plo = shifted(pl_ref, pl_prev_ref, mu_l_ref)
    r = pm[:, :C]
    k = pm[:, C:2 * C]
    v = pm[:, 2 * C:]
    wd = plo[:, :RWKV_DECAY_LORA]
    ad = plo[:, RWKV_DECAY_LORA:RWKV_DECAY_LORA + RWKV_A_LORA]
    gd = plo[:, RWKV_DECAY_LORA + RWKV_A_LORA:RWKV_LORA]
    w = -jax.nn.softplus(-(w0_ref[...] + _dot(jnp.tanh(wd).astype(BF16), w2_ref[...]))) - 0.5
    a = jax.nn.sigmoid(a0_ref[...] + _dot(ad.astype(BF16), a2_ref[...]))
    g = _dot(jax.nn.sigmoid(gd).astype(BF16), g2_ref[...])
    if has_vres:
        lo = _dot(_dot(v.astype(BF16), v1_ref[...]).astype(BF16), v2_ref[...])
        v = v + (vfirst_ref[...] - v) * jax.nn.sigmoid(v0_ref[...] + lo)
    kk = k * kk_ref[...]
    hid_r = lax.broadcasted_iota(jnp.int32, (LANES, LANES), 0) // RWKV_HEAD_DIM
    hid_c = lax.broadcasted_iota(jnp.int32, (LANES, LANES), 1) // RWKV_HEAD_DIM
    ones_bd = (hid_r == hid_c).astype(F32)
    for pp in range(C // LANES):
        sl = slice(pp * LANES, (pp + 1) * LANES)
        kkp = kk[:, sl]
        ss = _dot(kkp * kkp, ones_bd, precision=HIGHEST)
        kkn_o[:, sl] = kkp * lax.rsqrt(jnp.maximum(ss, 1e-24))
    r_o[...] = r
    lw_o[...] = -jnp.exp(w)
    k_o[...] = k * (1.0 + (a - 1.0) * ka_ref[...])
    v_o[...] = v
    a_o[...] = a
    g_o[...] = g


def rwkv_prep(proj, plora, S, mu, w0, w2, a0, a2, g2, k_k, k_a, v_first, v_lora, tm=256):
    T = proj.shape[0]
    C = RWKV_DIM
    LP = plora.shape[1]
    has_vres = v_lora is not None
    mu_m = mu[:3 * C].reshape(1, 3 * C)
    mu_l = jnp.zeros((1, LP), F32).at[0, :RWKV_LORA].set(mu[3 * C:])
    row = lambda z: z.reshape(1, -1)
    full = lambda z: pl.BlockSpec(z.shape, lambda i: (0,) * z.ndim)
    args = [proj, plora, proj, plora, mu_m, mu_l, row(w0), w2.astype(BF16), row(a0), a2.astype(BF16),
            g2.astype(BF16), row(k_k), row(k_a)]
    in_specs = [pl.BlockSpec((tm, 3 * C), lambda i: (i, 1)),
                pl.BlockSpec((tm, LP), lambda i: (i, 0)),
                pl.BlockSpec((8, 3 * C), lambda i: (jnp.maximum(i * (tm // 8) - 1, 0), 1)),
                pl.BlockSpec((8, LP), lambda i: (jnp.maximum(i * (tm // 8) - 1, 0), 0))]
    in_specs += [full(z) for z in args[4:]]
    if has_vres:
        v0, v1, v2 = v_lora
        extra = [v_first, row(v0), v1.astype(BF16), v2.astype(BF16)]
        args += extra
        in_specs += [pl.BlockSpec((tm, C), lambda i: (i, 0))] + [full(z) for z in extra[1:]]
    out_spec = pl.BlockSpec((tm, C), lambda i: (i, 0))
    return pl.pallas_call(
        functools.partial(_rwkv_prep_kernel, has_vres, S),
        grid=(T // tm,),
        in_specs=in_specs,
        out_specs=[out_spec] * 7,
        out_shape=[jax.ShapeDtypeStruct((T, C), F32)] * 7,
        compiler_params=_cparams("parallel"),
        name="rwkv_prep",
    )(*args)


def _rwkv_scan_kernel(r_ref, lw_ref, k_ref, v_ref, kk_ref, a_ref, g_ref, rk_ref, lnw_ref, lnb_ref,
                      o_ref, state_ref):
    c = pl.program_id(1)
    CH = RWKV_CHUNK
    N = RWKV_HEAD_DIM
    P2 = 2 * CH
    prec = HIGHEST

    @pl.when(c == 0)
    def _():
        state_ref[...] = jnp.zeros_like(state_ref)

    lane = lax.broadcasted_iota(jnp.int32, (1, LANES), 1)
    head0 = lane < N
    ri = lax.broadcasted_iota(jnp.int32, (P2, P2), 0)
    ci = lax.broadcasted_iota(jnp.int32, (P2, P2), 1)
    same_head = (ri // CH) == (ci // CH)
    strict = (ri % CH) > (ci % CH)
    incl = (ri % CH) >= (ci % CH)
    eye = (ri == ci).astype(F32)
    ones_bd = same_head.astype(F32)
    ti = lax.broadcasted_iota(jnp.int32, (CH, CH), 0)
    si = lax.broadcasted_iota(jnp.int32, (CH, CH), 1)
    tril_incl = (ti >= si).astype(F32)

    def stack(x):
        return jnp.concatenate([jnp.where(head0, x, 0.0), jnp.where(head0, 0.0, x)], axis=0)

    for p in range(RWKV_DIM // LANES):
        sl = slice(p * LANES, (p + 1) * LANES)
        r, lw, k, v = r_ref[:, sl], lw_ref[:, sl], k_ref[:, sl], v_ref[:, sl]
        kk, a = kk_ref[:, sl], a_ref[:, sl]
        cum = _dot(tril_incl, lw, precision=HIGHEST)
        cum_end = cum[CH - 1:CH, :]
        e_pos = jnp.exp(cum)
        e_neg = jnp.exp(-cum)
        e_end = jnp.exp(cum_end - cum)
        b = kk * a
        A_st = stack(-kk * jnp.exp(cum - lw))
        R_st = stack(r * e_pos)
        B_st = stack(b * e_neg)
        K_st = stack(k * e_neg)
        Bend_st = stack(b * e_end)
        Kend_st = stack(k * e_end)
        V_st = stack(v)

        AR = jnp.concatenate([A_st, R_st], axis=0)
        BK = jnp.concatenate([B_st, K_st], axis=0)
        G = _dot_nt(AR, BK, precision=prec)
        Lab = jnp.where(strict, G[:P2, :P2], 0.0)
        Lak = jnp.where(strict, G[:P2, P2:], 0.0)
        Mrb = jnp.where(incl, G[P2:, :P2], 0.0)
        Mrk = jnp.where(incl, G[P2:, P2:], 0.0)

        Tinv = eye + Lab
        Lp = Lab
        for _ in range(int(math.log2(CH)) - 1):
            Lp = _dot(Lp, Lp, precision=prec)
            Tinv = Tinv + _dot(Tinv, Lp, precision=prec)

        LakV = _dot(Lak, V_st, precision=prec)
        AU = _dot(Tinv, jnp.concatenate([A_st, LakV], axis=1), precision=prec)
        Ahat, U0 = AU[:, :LANES], AU[:, LANES:]
        MM = _dot(Mrb, AU, precision=prec)
        Rhat = R_st + MM[:, :LANES]
        Y0 = MM[:, LANES:] + _dot(Mrk, V_st, precision=prec)
        BT = _dot_tn(Bend_st, AU, precision=prec)
        Mmat = eye * jnp.exp(cum_end) + BT[:, :LANES]
        Z = BT[:, LANES:] + _dot_tn(Kend_st, V_st, precision=prec)

        S0 = state_ref[p]
        RS = _dot(jnp.concatenate([Rhat, Mmat], axis=0), S0, precision=prec)
        Y_st = RS[:P2] + Y0
        state_ref[p] = jnp.where(same_head, RS[P2:] + Z, 0.0)
        y = Y_st[:CH] + Y_st[CH:]

        mean = _dot(y, ones_bd, precision=HIGHEST) * (1.0 / N)
        yc = y - mean
        var = _dot(yc * yc, ones_bd, precision=HIGHEST) * (1.0 / N)
        yn = yc * lax.rsqrt(var + RWKV_GN_EPS) * lnw_ref[:, sl] + lnb_ref[:, sl]
        bonus = _dot(r * k * rk_ref[:, sl], ones_bd, precision=HIGHEST)
        o_ref[:, sl] = ((yn + bonus * v) * g_ref[:, sl]).astype(o_ref.dtype)


def rwkv_scan(r, lw, k, v, kk, a, g, r_k, ln_w, ln_b, B, S):
    T, C = r.shape
    CH = RWKV_CHUNK
    nc = S // CH
    blk = pl.BlockSpec((CH, C), lambda b, c: (b * nc + c, 0))
    par = pl.BlockSpec((1, C), lambda b, c: (0, 0))
    return pl.pallas_call(
        _rwkv_scan_kernel,
        grid=(B, nc),
        in_specs=[blk] * 7 + [par] * 3,
        out_specs=blk,
        out_shape=jax.ShapeDtypeStruct((T, C), BF16),
        scratch_shapes=[pltpu.VMEM((C // LANES, LANES, LANES), F32)],
        compiler_params=_cparams("parallel", "arbitrary"),
        name="rwkv_scan",
    )(r, lw, k, v, kk, a, g, r_k.reshape(1, C), ln_w.reshape(1, C), ln_b.reshape(1, C))


def _rope_pairs(x, cos2, sin2):
    lane = lax.broadcasted_iota(jnp.int32, (1, LANES), 1)
    first_half = (lane % MLA_ROPE_DIM) < (MLA_ROPE_DIM // 2)
    partner = jnp.where(first_half, pltpu.roll(x, LANES - MLA_ROPE_DIM // 2, 1),
                        pltpu.roll(x, MLA_ROPE_DIM // 2, 1))
    return x * cos2 + partner * sin2


def _mla_cproj_kernel(h_ref, w_ref, cos_ref, sin_ref, c_ref, kr_ref):
    acc = _dot(h_ref[...], w_ref[...])
    NC = MLA_Q_LORA + MLA_KV_LORA
    c_ref[...] = acc[:, :NC]
    kr_ref[...] = _rope_pairs(acc[:, NC:], cos_ref[...], sin_ref[...]).astype(kr_ref.dtype)


def mla_cproj(h16, w_in, cos2, sin2, S, tm=512):
    T, D = h16.shape
    NC = MLA_Q_LORA + MLA_KV_LORA
    w = jnp.concatenate([w_in, w_in[:, NC:]], axis=1).astype(BF16)
    ns = S // tm
    return pl.pallas_call(
        _mla_cproj_kernel,
        grid=(T // tm,),
        in_specs=[pl.BlockSpec((tm, D), lambda i: (i, 0)),
                  pl.BlockSpec((D, NC + LANES), lambda i: (0, 0)),
                  pl.BlockSpec((tm, LANES), lambda i: (i % ns, 0)),
                  pl.BlockSpec((tm, LANES), lambda i: (i % ns, 0))],
        out_specs=[pl.BlockSpec((tm, NC), lambda i: (i, 0)),
                   pl.BlockSpec((tm, LANES), lambda i: (i, 0))],
        out_shape=[jax.ShapeDtypeStruct((T, NC), F32), jax.ShapeDtypeStruct((T, LANES), BF16)],
        compiler_params=_cparams("parallel"),
        name="mla_cproj",
    )(h16, w, cos2, sin2)


def _rms_up_kernel(rope_tile, c_ref, g_ref, w_ref, cos_ref, sin_ref, o_ref, cn_ref):
    j = pl.program_id(1)

    @pl.when(j == 0)
    def _():
        x = c_ref[...]
        ms = jnp.mean(x * x, axis=-1, keepdims=True)
        cn_ref[...] = (x * lax.rsqrt(ms + RMS_EPS) * g_ref[...]).astype(BF16)

    acc = _dot(cn_ref[...], w_ref[...])

    if rope_tile is None:
        o_ref[...] = acc.astype(o_ref.dtype)
    else:
        @pl.when(j != rope_tile)
        def _():
            o_ref[...] = acc.astype(o_ref.dtype)

        @pl.when(j == rope_tile)
        def _():
            cos2, sin2 = cos_ref[...], sin_ref[...]
            for gidx in range(acc.shape[1] // LANES):
                sl = slice(gidx * LANES, (gidx + 1) * LANES)
                o_ref[:, sl] = _rope_pairs(acc[:, sl], cos2, sin2).astype(o_ref.dtype)


def rms_up_proj(c, col_block, gain, w, cos2, sin2, S, rope_tile, tm=512, tn=1024):
    T = c.shape[0]
    K, N = w.shape
    ns = S // tm
    return pl.pallas_call(
        functools.partial(_rms_up_kernel, rope_tile),
        grid=(T // tm, N // tn),
        in_specs=[pl.BlockSpec((tm, K), lambda i, j: (i, col_block)),
                  pl.BlockSpec((1, K), lambda i, j: (0, 0)),
                  pl.BlockSpec((K, tn), lambda i, j: (0, j)),
                  pl.BlockSpec((tm, LANES), lambda i, j: (i % ns, 0)),
                  pl.BlockSpec((tm, LANES), lambda i, j: (i % ns, 0))],
        out_specs=pl.BlockSpec((tm, tn), lambda i, j: (i, j)),
        out_shape=jax.ShapeDtypeStruct((T, N), BF16),
        scratch_shapes=[pltpu.VMEM((tm, K), BF16)],
        compiler_params=_cparams("parallel", "arbitrary"),
        name="rms_up_proj",
    )(c, gain.reshape(1, K), w, cos2, sin2)


def _mla_attn_kernel(qn_ref, qr_ref, kn_ref, v_ref, kr_ref, o_ref):
    hd = pl.program_id(1)
    qi = pl.program_id(2)
    TQ = qn_ref.shape[0]
    lane = lax.broadcasted_iota(jnp.int32, (1, LANES), 1)
    own = (lane // MLA_ROPE_DIM) == (hd % 2)
    qr = jnp.where(own, qr_ref[...], jnp.zeros_like(qr_ref[...]))
    q = jnp.concatenate([qn_ref[...], qr], axis=1)
    scale = (MLA_NOPE_DIM + MLA_ROPE_DIM) ** -0.5
    row = lax.broadcasted_iota(jnp.int32, (TQ, TQ), 0)
    col = lax.broadcasted_iota(jnp.int32, (TQ, TQ), 1)

    def scores(off):
        kc = jnp.concatenate([kn_ref[pl.ds(off, TQ), :], kr_ref[pl.ds(off, TQ), :]], axis=1)
        return _dot_nt(q, kc) * scale

    off_d = pl.multiple_of(qi * TQ, TQ)
    s = jnp.where(col <= row, scores(off_d), NEG_INF)
    m_run = jnp.max(s, axis=-1, keepdims=True)
    pr = jnp.exp(s - m_run)
    l_run = jnp.sum(pr, axis=-1, keepdims=True)
    acc = _dot(pr.astype(BF16), v_ref[pl.ds(off_d, TQ), :])

    def body(n, carry):
        m_run, l_run, acc = carry
        off = pl.multiple_of(n * TQ, TQ)
        s = scores(off)
        m_new = jnp.maximum(m_run, jnp.max(s, axis=-1, keepdims=True))
        alpha = jnp.exp(m_run - m_new)
        pr = jnp.exp(s - m_new)
        l_new = alpha * l_run + jnp.sum(pr, axis=-1, keepdims=True)
        acc_new = alpha * acc + _dot(pr.astype(BF16), v_ref[pl.ds(off, TQ), :])
        return m_new, l_new, acc_new

    m_run, l_run, acc = lax.fori_loop(0, qi, body, (m_run, l_run, acc))
    o_ref[...] = (acc / l_run).astype(o_ref.dtype)


def mla_attention(q, kv, kr2, B, S, tq=256):
    T = B * S
    H = MLA_HEADS
    nq = S // tq
    return pl.pallas_call(
        _mla_attn_kernel,
        grid=(B, H, nq),
        in_specs=[pl.BlockSpec((tq, LANES), lambda b, h, i: (b * nq + i, h)),
                  pl.BlockSpec((tq, LANES), lambda b, h, i: (b * nq + i, H + h // 2)),
                  pl.BlockSpec((S, LANES), lambda b, h, i: (b, 2 * h)),
                  pl.BlockSpec((S, LANES), lambda b, h, i: (b, 2 * h + 1)),
                  pl.BlockSpec((S, LANES), lambda b, h, i: (b, 0))],
        out_specs=pl.BlockSpec((tq, LANES), lambda b, h, i: (b * nq + i, h)),
        out_shape=jax.ShapeDtypeStruct((T, H * MLA_V_DIM), BF16),
        compiler_params=_cparams("parallel", "parallel", "arbitrary"),
        name="mla_attention",
    )(q, q, kv, kv, kr2)


def _rope_tables(S):
    half = MLA_ROPE_DIM // 2
    inv = ROPE_THETA ** (-jnp.arange(0, MLA_ROPE_DIM, 2, dtype=F32) / MLA_ROPE_DIM)
    ang = jnp.arange(S, dtype=F32)[:, None] * inv[None, :]
    cos, sin = jnp.cos(ang), jnp.sin(ang)
    reps = LANES // MLA_ROPE_DIM
    cos2 = jnp.tile(jnp.concatenate([cos, cos], axis=1), (1, reps))
    sin2 = jnp.tile(jnp.concatenate([-sin, sin], axis=1), (1, reps))
    assert cos2.shape == (S, LANES) and half * 2 * reps == LANES
    return cos2, sin2


def kernel(x, ev_w_in, ev_w_out, rw_mu, rw_w0, rw_w2, rw_a0, rw_a2, rw_g2, rw_k_k, rw_k_a, rw_r_k,
           rw_ln_w, rw_ln_b, rw_v0, rw_v1, rw_v2, od_w_in, od_q_norm, od_kv_norm, od_w_uq, od_w_ukv,
           od_w_out, ln_mix_g, ln_mix_b, ln_ffn_g, ln_ffn_b, moe_w_r, moe_b_r, moe_w1, moe_b1,
           moe_w2, moe_b2):
    B, S, D = x.shape
    T = B * S
    h32 = x.reshape(T, D)
    h16 = h32.astype(BF16)
    cos2, sin2 = _rope_tables(S)
    n_main = 3 * MOBA_DIM + 3 * RWKV_DIM
    lora_pad = -(-RWKV_LORA // LANES) * LANES
    qd = MLA_NOPE_DIM + MLA_ROPE_DIM
    perm = np.concatenate([
        (np.arange(MLA_HEADS)[:, None] * qd + np.arange(MLA_NOPE_DIM)[None, :]).reshape(-1),
        (np.arange(MLA_HEADS)[:, None] * qd + MLA_NOPE_DIM + np.arange(MLA_ROPE_DIM)[None, :]).reshape(-1)])
    v_first = None
    for layer in range(DEPTH):
        j = layer // 2
        if layer % 2 == 0:
            w_in = ev_w_in[j]
            w_main = w_in[:, :n_main].astype(BF16)
            w_lora = jnp.zeros((D, lora_pad), BF16).at[:, :RWKV_LORA].set(w_in[:, n_main:].astype(BF16))
            proj = matmul(h16, w_main, F32)
            plora = matmul(h16, w_lora, F32)
            a_out = moba_attention(proj, B, S)
            v_lora = None if j == 0 else (rw_v0[j - 1], rw_v1[j - 1], rw_v2[j - 1])
            r, lw, k, v, kk, a, g = rwkv_prep(proj, plora, S, rw_mu[j], rw_w0[j], rw_w2[j], rw_a0[j],
                                             rw_a2[j], rw_g2[j], rw_k_k[j], rw_k_a[j], v_first, v_lora)
            if j == 0:
                v_first = v
            b_out = rwkv_scan(r, lw, k, v, kk, a, g, rw_r_k[j], rw_ln_w[j], rw_ln_b[j], B, S)
            w_out = ev_w_out[j].astype(BF16)
            h32, h16 = proj_residual_ln([a_out, b_out], [w_out[:MOBA_DIM], w_out[MOBA_DIM:]], h32,
                                        ln_mix_g[layer], ln_mix_b[layer])
        else:
            c, kr2 = mla_cproj(h16, od_w_in[j], cos2, sin2, S)
            q = rms_up_proj(c, 0, od_q_norm[j], od_w_uq[j][:, perm].astype(BF16), cos2, sin2, S,
                            rope_tile=2)
            kv = rms_up_proj(c, 1, od_kv_norm[j], od_w_ukv[j].astype(BF16), cos2, sin2, S,
                             rope_tile=None)
            o = mla_attention(q, kv, kr2, B, S)
            h32, h16 = proj_residual_ln([o], [od_w_out[j].astype(BF16)], h32,
                                        ln_mix_g[layer], ln_mix_b[layer])
        h32, h16 = moe_layer(h32, h16, moe_w_r[layer], moe_b_r[layer], moe_w1[layer].astype(BF16),
                             moe_b1[layer], moe_w2[layer].astype(BF16), moe_b2[layer],
                             ln_ffn_g[layer], ln_ffn_b[layer])
    return h32.reshape(B, S, D)
```

```python
import functools
import math

import jax
import jax.numpy as jnp
import numpy as np
from jax import lax
from jax.experimental import pallas as pl
from jax.experimental.pallas import tpu as pltpu

F32 = jnp.float32
BF16 = jnp.bfloat16
HIGHEST = lax.Precision.HIGHEST

DEPTH = 4
MOBA_HEADS = 16
MOBA_HEAD_DIM = 64
MOBA_DIM = MOBA_HEADS * MOBA_HEAD_DIM
MOBA_BLOCK = 256
MOBA_TOPK = 3
RWKV_HEADS = 16
RWKV_HEAD_DIM = 64
RWKV_DIM = RWKV_HEADS * RWKV_HEAD_DIM
RWKV_DECAY_LORA = 64
RWKV_A_LORA = 64
RWKV_G_LORA = 160
RWKV_LORA = RWKV_DECAY_LORA + RWKV_A_LORA + RWKV_G_LORA
RWKV_GN_EPS = 64e-5
RWKV_CHUNK = 64
MLA_HEADS = 16
MLA_Q_LORA = 512
MLA_KV_LORA = 512
MLA_NOPE_DIM = 128
MLA_ROPE_DIM = 64
MLA_V_DIM = 128
ROPE_THETA = 10000.0
N_EXPERTS = 32
TOP_K = 4
D_EXPERT = 1024
SWIGLU_ALPHA = 1.702
SWIGLU_LIMIT = 7.0
MOE_ROWS = 256
DEEPNORM_ALPHA = (2 * DEPTH) ** 0.25
LN_EPS = 1e-5
RMS_EPS = 1e-6
NEG_INF = -1e30

LANES = 128
VMEM_LIMIT = 56 * 1024 * 1024


def _cparams(*sem):
    return pltpu.CompilerParams(dimension_semantics=sem, vmem_limit_bytes=VMEM_LIMIT)


def _dot(a, b, precision=None):
    return jnp.dot(a, b, preferred_element_type=F32, precision=precision)


def _dot_nt(a, b, precision=None):
    return lax.dot_general(a, b, (((1,), (1,)), ((), ())), preferred_element_type=F32,
                           precision=precision)


def _dot_tn(a, b, precision=None):
    return lax.dot_general(a, b, (((0,), (0,)), ((), ())), preferred_element_type=F32,
                           precision=precision)


def _mm_kernel(a_ref, w_ref, o_ref):
    o_ref[...] = _dot(a_ref[...], w_ref[...]).astype(o_ref.dtype)


def matmul(a, w, out_dtype, tm=1024, tn=512):
    M, K = a.shape
    N = w.shape[1]
    tm, tn = min(tm, M), min(tn, N)
    assert M % tm == 0 and N % tn == 0
    return pl.pallas_call(
        _mm_kernel,
        grid=(M // tm, N // tn),
        in_specs=[pl.BlockSpec((tm, K), lambda i, j: (i, 0)),
                  pl.BlockSpec((K, tn), lambda i, j: (0, j))],
        out_specs=pl.BlockSpec((tm, tn), lambda i, j: (i, j)),
        out_shape=jax.ShapeDtypeStruct((M, N), out_dtype),
        compiler_params=_cparams("parallel", "parallel"),
        name="matmul",
    )(a, w)


def _layer_norm_rows(z, g, b):
    mu = jnp.mean(z, axis=-1, keepdims=True)
    zc = z - mu
    var = jnp.mean(zc * zc, axis=-1, keepdims=True)
    return zc * lax.rsqrt(var + LN_EPS) * g + b


def _proj_ln_kernel(n_in, *refs):
    a_refs = refs[:n_in]
    w_refs = refs[n_in:2 * n_in]
    h_ref, g_ref, b_ref, o32_ref, o16_ref = refs[2 * n_in:]
    acc = DEEPNORM_ALPHA * h_ref[...]
    for a_ref, w_ref in zip(a_refs, w_refs):
        acc = acc + _dot(a_ref[...], w_ref[...])
    y = _layer_norm_rows(acc, g_ref[...], b_ref[...])
    o32_ref[...] = y
    o16_ref[...] = y.astype(BF16)


def proj_residual_ln(a_list, w_list, h, g, b, tm=256):
    M, D = h.shape
    n_in = len(a_list)
    in_specs = ([pl.BlockSpec((tm, a.shape[1]), lambda i: (i, 0)) for a in a_list]
                + [pl.BlockSpec(w.shape, lambda i: (0, 0)) for w in w_list]
                + [pl.BlockSpec((tm, D), lambda i: (i, 0)),
                   pl.BlockSpec((1, D), lambda i: (0, 0)),
                   pl.BlockSpec((1, D), lambda i: (0, 0))])
    return pl.pallas_call(
        functools.partial(_proj_ln_kernel, n_in),
        grid=(M // tm,),
        in_specs=in_specs,
        out_specs=[pl.BlockSpec((tm, D), lambda i: (i, 0)),
                   pl.BlockSpec((tm, D), lambda i: (i, 0))],
        out_shape=[jax.ShapeDtypeStruct((M, D), F32), jax.ShapeDtypeStruct((M, D), BF16)],
        compiler_params=_cparams("parallel"),
        name="proj_residual_ln",
    )(*a_list, *w_list, h, g.reshape(1, D), b.reshape(1, D))


def _combine_ln_kernel(y_ref, gate_ref, h_ref, g_ref, b_ref, o32_ref, o16_ref):
    acc = DEEPNORM_ALPHA * h_ref[...]
    gates = gate_ref[...]
    for k in range(TOP_K):
        acc = acc + y_ref[k].astype(F32) * gates[:, k:k + 1]
    y = _layer_norm_rows(acc, g_ref[...], b_ref[...])
    o32_ref[...] = y
    o16_ref[...] = y.astype(BF16)


def combine_residual_ln(y4, gates, h, g, b, tm=256):
    M, D = h.shape
    return pl.pallas_call(
        _combine_ln_kernel,
        grid=(M // tm,),
        in_specs=[pl.BlockSpec((TOP_K, tm, D), lambda i: (0, i, 0)),
                  pl.BlockSpec((tm, TOP_K), lambda i: (i, 0)),
                  pl.BlockSpec((tm, D), lambda i: (i, 0)),
                  pl.BlockSpec((1, D), lambda i: (0, 0)),
                  pl.BlockSpec((1, D), lambda i: (0, 0))],
        out_specs=[pl.BlockSpec((tm, D), lambda i: (i, 0)),
                   pl.BlockSpec((tm, D), lambda i: (i, 0))],
        out_shape=[jax.ShapeDtypeStruct((M, D), F32), jax.ShapeDtypeStruct((M, D), BF16)],
        compiler_params=_cparams("parallel"),
        name="combine_residual_ln",
    )(y4, gates, h, g.reshape(1, D), b.reshape(1, D))


def _router_kernel(h_ref, w_ref, b_ref, o_ref):
    o_ref[...] = _dot(h_ref[...], w_ref[...], precision=HIGHEST) + b_ref[...]


def router_logits(h, w_r, b_r, tm=512):
    M, D = h.shape
    w = jnp.zeros((D, LANES), F32).at[:, :N_EXPERTS].set(w_r)
    b = jnp.zeros((1, LANES), F32).at[0, :N_EXPERTS].set(b_r)
    out = pl.pallas_call(
        _router_kernel,
        grid=(M // tm,),
        in_specs=[pl.BlockSpec((tm, D), lambda i: (i, 0)),
                  pl.BlockSpec((D, LANES), lambda i: (0, 0)),
                  pl.BlockSpec((1, LANES), lambda i: (0, 0))],
        out_specs=pl.BlockSpec((tm, LANES), lambda i: (i, 0)),
        out_shape=jax.ShapeDtypeStruct((M, LANES), F32),
        compiler_params=_cparams("parallel"),
        name="router_logits",
    )(h, w, b)
    return out[:, :N_EXPERTS]


def _expert_kernel(be_ref, nv_ref, x_ref, w1_ref, b1_ref, w2_ref, b2_ref, o_ref):
    i = pl.program_id(0)

    @pl.when(i < nv_ref[0])
    def _():
        hgu = _dot(x_ref[...], w1_ref[0]) + b1_ref[0]
        gate = jnp.minimum(hgu[:, :D_EXPERT], SWIGLU_LIMIT)
        up = jnp.clip(hgu[:, D_EXPERT:], -SWIGLU_LIMIT, SWIGLU_LIMIT)
        act = gate * jax.nn.sigmoid(SWIGLU_ALPHA * gate) * (up + 1.0)
        o_ref[...] = (_dot(act.astype(BF16), w2_ref[0]) + b2_ref[0]).astype(o_ref.dtype)

    @pl.when(i >= nv_ref[0])
    def _():
        o_ref[...] = jnp.zeros_like(o_ref)


def expert_ffn(xbuf, block_e, n_valid, w1, b1, w2, b2):
    rows, D = xbuf.shape
    R = MOE_ROWS
    n_blocks = rows // R
    grid_spec = pltpu.PrefetchScalarGridSpec(
        num_scalar_prefetch=2,
        grid=(n_blocks,),
        in_specs=[pl.BlockSpec((R, D), lambda i, be, nv: (i, 0)),
                  pl.BlockSpec((1, D, 2 * D_EXPERT), lambda i, be, nv: (be[i], 0, 0)),
                  pl.BlockSpec((1, 1, 2 * D_EXPERT), lambda i, be, nv: (be[i], 0, 0)),
                  pl.BlockSpec((1, D_EXPERT, D), lambda i, be, nv: (be[i], 0, 0)),
                  pl.BlockSpec((1, 1, D), lambda i, be, nv: (be[i], 0, 0))],
        out_specs=pl.BlockSpec((R, D), lambda i, be, nv: (i, 0)),
    )
    return pl.pallas_call(
        _expert_kernel,
        grid_spec=grid_spec,
        out_shape=jax.ShapeDtypeStruct((rows, D), BF16),
        compiler_params=_cparams("arbitrary"),
        name="expert_ffn",
    )(block_e, n_valid, xbuf, w1, b1.reshape(N_EXPERTS, 1, -1), w2, b2.reshape(N_EXPERTS, 1, -1))


def moe_layer(h32, h16, w_r, b_r, w1, b1, w2, b2, ln_g, ln_b):
    T, D = h32.shape
    R = MOE_ROWS
    M = T * TOP_K
    logits = router_logits(h32, w_r, b_r)
    top_val, top_idx = lax.top_k(logits, TOP_K)
    gates = jax.nn.softmax(top_val, axis=-1)
    flat_e = top_idx.reshape(M)
    onehot = (flat_e[:, None] == jnp.arange(N_EXPERTS)[None, :]).astype(jnp.int32)
    csum = jnp.cumsum(onehot, axis=0)
    counts = csum[-1]
    rank = jnp.take_along_axis(csum, flat_e[:, None], axis=1)[:, 0] - 1
    padded = (counts + R - 1) // R * R
    pad_end = jnp.cumsum(padded)
    pad_start = pad_end - padded
    pos = pad_start[flat_e] + rank
    n_blocks = (M + N_EXPERTS * (R - 1) + R - 1) // R
    rows = n_blocks * R
    src_tok = jnp.full((rows,), T, jnp.int32).at[pos].set(jnp.arange(M, dtype=jnp.int32) // TOP_K)
    block_e = jnp.minimum(jnp.searchsorted(pad_end, jnp.arange(n_blocks) * R, side='right'),
                          N_EXPERTS - 1).astype(jnp.int32)
    n_valid = (pad_end[-1] // R).astype(jnp.int32).reshape(1)
    x_pad = jnp.concatenate([h16, jnp.zeros((1, D), BF16)], axis=0)
    xbuf = x_pad[src_tok]
    ybuf = expert_ffn(xbuf, block_e, n_valid, w1, b1, w2, b2)
    y4 = ybuf[pos.reshape(T, TOP_K).T]
    return combine_residual_ln(y4, gates, h32, ln_g, ln_b)


MOBA_GROUP = 2


def _moba_kernel(q_ref, k_ref, v_ref, o_ref, kmean_ref, kt_ref):
    pg = pl.program_id(1)
    cur = pl.program_id(2)
    BLK = MOBA_BLOCK
    GP = MOBA_GROUP
    nb = k_ref.shape[0] // BLK
    lane = lax.broadcasted_iota(jnp.int32, (1, LANES), 1)
    head0 = lane < MOBA_HEAD_DIM
    LOG2E = math.log2(math.e)

    @pl.when(cur == 0)
    def _():
        kmean_ref[...] = jnp.zeros_like(kmean_ref)
        for g in range(GP):
            for n in range(nb):
                kb = k_ref[n * BLK:(n + 1) * BLK, g * LANES:(g + 1) * LANES]
                kmean_ref[g, n:n + 1, :] = jnp.mean(kb, axis=0, keepdims=True)
                kt_ref[g, :, n * BLK:(n + 1) * BLK] = kb.T.astype(BF16)

    row = lax.broadcasted_iota(jnp.int32, (2 * BLK, 1), 0)
    lane_b = lax.broadcasted_iota(jnp.int32, (2 * BLK, LANES), 1)
    r_in = jnp.where(row >= BLK, row - BLK, row)
    col = lax.broadcasted_iota(jnp.int32, (1, BLK), 1)
    rel = r_in - col
    causal = rel >= 0
    c2 = MOBA_HEAD_DIM ** -0.5 * LOG2E

    qb, sel, slope2, srel = [], [], [], []
    for g in range(GP):
        q = q_ref[:, g * LANES:(g + 1) * LANES]
        qs = jnp.concatenate([jnp.where(head0, q, 0.0), jnp.where(head0, 0.0, q)], axis=0)
        head_idx = 2 * (pg * GP + g) + (row >= BLK).astype(jnp.int32) + 1
        sl2 = jnp.exp(head_idx.astype(F32) * (-8.0 * math.log(2.0) / MOBA_HEADS)) * LOG2E
        gate = _dot_nt(qs, kmean_ref[g], precision=HIGHEST)
        rank = jnp.zeros((2 * BLK, LANES), jnp.int32)
        for m in range(nb):
            gm = gate[:, m:m + 1]
            ahead = (gm > gate) | ((gm == gate) & (m < lane_b))
            rank = rank + jnp.where(ahead, 1, 0) * (m < cur).astype(jnp.int32)
        sel.append((rank < MOBA_TOPK) & (lane_b < cur))
        qb.append(qs.astype(BF16))
        slope2.append(sl2)
        srel.append(sl2 * rel.astype(F32))

    def tile(g, n, carry, diag):
        m_run, l_run, acc = carry
        off = pl.multiple_of(n * BLK, BLK)
        t = _dot(qb[g], kt_ref[g, :, pl.ds(off, BLK)]) * c2 - srel[g]
        if diag:
            t = jnp.where(causal, t, NEG_INF)
            m_new = jnp.maximum(m_run, jnp.max(t, axis=-1, keepdims=True))
            shift = m_new
        else:
            sel_n = jnp.sum(jnp.where(sel[g] & (lane_b == n), 1.0, 0.0), axis=-1, keepdims=True) > 0.5
            bias = jnp.where(sel_n, -slope2[g] * ((cur - n) * BLK).astype(F32), NEG_INF)
            m_new = jnp.maximum(m_run, jnp.max(t, axis=-1, keepdims=True) + bias)
            shift = m_new - bias
        alpha = jnp.exp2(m_run - m_new)
        pr = jnp.exp2(t - shift)
        l_new = alpha * l_run + (pr[:, :LANES] + pr[:, LANES:])
        v_t = v_ref[pl.ds(off, BLK), g * LANES:(g + 1) * LANES].astype(BF16)
        acc_new = alpha * acc + _dot(pr.astype(BF16), v_t)
        return m_new, l_new, acc_new

    init = (jnp.full((2 * BLK, 1), NEG_INF, F32), jnp.zeros((2 * BLK, LANES), F32),
            jnp.zeros((2 * BLK, LANES), F32))
    carry = tuple(tile(g, cur, init, True) for g in range(GP))

    def body(n, carry):
        return tuple(tile(g, n, carry[g], False) for g in range(GP))

    carry = lax.fori_loop(0, cur, body, carry)
    for g in range(GP):
        m_run, l_run, acc = carry[g]
        out = acc / jnp.sum(l_run, axis=-1, keepdims=True)
        o_ref[:, g * LANES:(g + 1) * LANES] = jnp.where(head0, out[:BLK], out[BLK:]).astype(o_ref.dtype)


def moba_attention(proj, B, S):
    T = B * S
    BLK = MOBA_BLOCK
    GP = MOBA_GROUP
    W = GP * LANES
    n_grp = MOBA_DIM // W
    nq = S // BLK
    return pl.pallas_call(
        _moba_kernel,
        grid=(B, n_grp, nq),
        in_specs=[pl.BlockSpec((BLK, W), lambda b, p, c: (b * nq + c, p)),
                  pl.BlockSpec((S, W), lambda b, p, c: (b, n_grp + p)),
                  pl.BlockSpec((S, W), lambda b, p, c: (b, 2 * n_grp + p))],
        out_specs=pl.BlockSpec((BLK, W), lambda b, p, c: (b * nq + c, p)),
        out_shape=jax.ShapeDtypeStruct((T, MOBA_DIM), BF16),
        scratch_shapes=[pltpu.VMEM((GP, LANES, LANES), F32), pltpu.VMEM((GP, LANES, S), BF16)],
        compiler_params=_cparams("parallel", "parallel", "arbitrary"),
        name="moba_attention",
    )(proj, proj, proj)


def _rwkv_prep_kernel(has_vres, S, *refs):
    if has_vres:
        (pm_ref, pl_ref, pm_prev_ref, pl_prev_ref, mu_m_ref, mu_l_ref, w0_ref, w2_ref, a0_ref, a2_ref,
         g2_ref, kk_ref, ka_ref, vfirst_ref, v0_ref, v1_ref, v2_ref,
         r_o, lw_o, k_o, v_o, kkn_o, a_o, g_o) = refs
    else:
        (pm_ref, pl_ref, pm_prev_ref, pl_prev_ref, mu_m_ref, mu_l_ref, w0_ref, w2_ref, a0_ref, a2_ref,
         g2_ref, kk_ref, ka_ref,
         r_o, lw_o, k_o, v_o, kkn_o, a_o, g_o) = refs
    i = pl.program_id(0)
    tm = pm_ref.shape[0]
    C = RWKV_DIM
    row = lax.broadcasted_iota(jnp.int32, (tm, 1), 0)
    seq_start = (i * tm) % S == 0

    def shifted(cur_ref, prev_ref, mu_ref):
        x = cur_ref[...]
        prev_row = jnp.where(seq_start, 0.0, prev_ref[7:8, :])
        xs = jnp.where(row == 0, prev_row, pltpu.roll(x, 1, 0))
        return x + (xs - x) * mu_ref[...]

    pm = shifted(pm_ref, pm_prev_ref, mu_m_ref)
    plo = shifted(pl_ref, pl_prev_ref, mu_l_ref)
    r = pm[:, :C]
    k = pm[:, C:2 * C]
    v = pm[:, 2 * C:]
    wd = plo[:, :RWKV_DECAY_LORA]
    ad = plo[:, RWKV_DECAY_LORA:RWKV_DECAY_LORA + RWKV_A_LORA]
    gd = plo[:, RWKV_DECAY_LORA + RWKV_A_LORA:RWKV_LORA]
    w = -jax.nn.softplus(-(w0_ref[...] + _dot(jnp.tanh(wd).astype(BF16), w2_ref[...]))) - 0.5
    a = jax.nn.sigmoid(a0_ref[...] + _dot(ad.astype(BF16), a2_ref[...]))
    g = _dot(jax.nn.sigmoid(gd).astype(BF16), g2_ref[...])
    if has_vres:
        lo = _dot(_dot(v.astype(BF16), v1_ref[...]).astype(BF16), v2_ref[...])
        v = v + (vfirst_ref[...] - v) * jax.nn.sigmoid(v0_ref[...] + lo)
    kk = k * kk_ref[...]
    hid_r = lax.broadcasted_iota(jnp.int32, (LANES, LANES), 0) // RWKV_HEAD_DIM
    hid_c = lax.broadcasted_iota(jnp.int32, (LANES, LANES), 1) // RWKV_HEAD_DIM
    ones_bd = (hid_r == hid_c).astype(F32)
    for pp in range(C // LANES):
        sl = slice(pp * LANES, (pp + 1) * LANES)
        kkp = kk[:, sl]
        ss = _dot(kkp * kkp, ones_bd, precision=HIGHEST)
        kkn_o[:, sl] = kkp * lax.rsqrt(jnp.maximum(ss, 1e-24))
    r_o[...] = r
    lw_o[...] = -jnp.exp(w)
    k_o[...] = k * (1.0 + (a - 1.0) * ka_ref[...])
    v_o[...] = v
    a_o[...] = a
    g_o[...] = g


def rwkv_prep(proj, plora, S, mu, w0, w2, a0, a2, g2, k_k, k_a, v_first, v_lora, tm=256):
    T = proj.shape[0]
    C = RWKV_DIM
    LP = plora.shape[1]
    has_vres = v_lora is not None
    mu_m = mu[:3 * C].reshape(1, 3 * C)
    mu_l = jnp.zeros((1, LP), F32).at[0, :RWKV_LORA].set(mu[3 * C:])
    row = lambda z: z.reshape(1, -1)
    full = lambda z: pl.BlockSpec(z.shape, lambda i: (0,) * z.ndim)
    args = [proj, plora, proj, plora, mu_m, mu_l, row(w0), w2.astype(BF16), row(a0), a2.astype(BF16),
            g2.astype(BF16), row(k_k), row(k_a)]
    in_specs = [pl.BlockSpec((tm, 3 * C), lambda i: (i, 1)),
                pl.BlockSpec((tm, LP), lambda i: (i, 0)),
                pl.BlockSpec((8, 3 * C), lambda i: (jnp.maximum(i * (tm // 8) - 1, 0), 1)),
                pl.BlockSpec((8, LP), lambda i: (jnp.maximum(i * (tm // 8) - 1, 0), 0))]
    in_specs += [full(z) for z in args[4:]]
    if has_vres:
        v0, v1, v2 = v_lora
        extra = [v_first, row(v0), v1.astype(BF16), v2.astype(BF16)]
        args += extra
        in_specs += [pl.BlockSpec((tm, C), lambda i: (i, 0))] + [full(z) for z in extra[1:]]
    out_spec = pl.BlockSpec((tm, C), lambda i: (i, 0))
    return pl.pallas_call(
        functools.partial(_rwkv_prep_kernel, has_vres, S),
        grid=(T // tm,),
        in_specs=in_specs,
        out_specs=[out_spec] * 7,
        out_shape=[jax.ShapeDtypeStruct((T, C), F32)] * 7,
        compiler_params=_cparams("parallel"),
        name="rwkv_prep",
    )(*args)


def _split3_bf16(x):
    x0 = x.astype(BF16)
    r1 = x - x0.astype(F32)
    x1 = r1.astype(BF16)
    x2 = (r1 - x1.astype(F32)).astype(BF16)
    return x0, x1, x2


def _sum01_left(m01, x):
    x0, x1, x2 = _split3_bf16(x)
    return _dot(m01, x0) + (_dot(m01, x1) + _dot(m01, x2))


def _sum01_right(x, m01):
    x0, x1, x2 = _split3_bf16(x)
    return _dot(x0, m01) + (_dot(x1, m01) + _dot(x2, m01))


_NN = (((1,), (0,)), ((), ()))
_NT = (((1,), (1,)), ((), ()))
_TN = (((0,), (0,)), ((), ()))


def _mm(a, b, passes, dims=_NN):
    dg = lambda x, y: lax.dot_general(x, y, dims, preferred_element_type=F32)
    if passes == 1:
        return dg(a.astype(BF16), b.astype(BF16))
    a_hi = a.astype(BF16)
    a_lo = (a - a_hi.astype(F32)).astype(BF16)
    b_hi = b.astype(BF16)
    b_lo = (b - b_hi.astype(F32)).astype(BF16)
    return dg(a_hi, b_hi) + (dg(a_hi, b_lo) + dg(a_lo, b_hi))


RWKV_PASSES = dict(gram=1, inv=1, apply=1, state=3)


def _rwkv_scan_kernel(r_ref, lw_ref, k_ref, v_ref, kk_ref, a_ref, g_ref, rk_ref, lnw_ref, lnb_ref,
                      o_ref, state_ref):
    c = pl.program_id(1)
    CH = RWKV_CHUNK
    N = RWKV_HEAD_DIM
    P2 = 2 * CH
    NP = RWKV_DIM // LANES
    pg, pi, pa, ps = (RWKV_PASSES[n] for n in ("gram", "inv", "apply", "state"))

    @pl.when(c == 0)
    def _():
        state_ref[...] = jnp.zeros_like(state_ref)

    lane = lax.broadcasted_iota(jnp.int32, (1, LANES), 1)
    head0 = lane < N
    ri = lax.broadcasted_iota(jnp.int32, (P2, P2), 0)
    ci = lax.broadcasted_iota(jnp.int32, (P2, P2), 1)
    same_head = (ri // CH) == (ci // CH)
    strict = (ri % CH) > (ci % CH)
    incl = (ri % CH) >= (ci % CH)
    eye = (ri == ci).astype(F32)
    ones_bd = same_head.astype(BF16)
    ti = lax.broadcasted_iota(jnp.int32, (CH, CH), 0)
    si = lax.broadcasted_iota(jnp.int32, (CH, CH), 1)
    tril_incl = (ti >= si).astype(BF16)

    def stack(x):
        return jnp.concatenate([jnp.where(head0, x, 0.0), jnp.where(head0, 0.0, x)], axis=0)

    pairs = range(NP)
    sls = [slice(p * LANES, (p + 1) * LANES) for p in pairs]
    r = [r_ref[:, sl] for sl in sls]
    k = [k_ref[:, sl] for sl in sls]
    v = [v_ref[:, sl] for sl in sls]
    kk = [kk_ref[:, sl] for sl in sls]
    lw = [lw_ref[:, sl] for sl in sls]
    cum = [_sum01_left(tril_incl, lw[p]) for p in pairs]
    cum_end = [cum[p][CH - 1:CH, :] for p in pairs]
    b = [kk[p] * a_ref[:, sls[p]] for p in pairs]
    e_neg = [jnp.exp(-cum[p]) for p in pairs]
    e_end = [jnp.exp(cum_end[p] - cum[p]) for p in pairs]
    A_st = [stack(-kk[p] * jnp.exp(cum[p] - lw[p])) for p in pairs]
    R_st = [stack(r[p] * jnp.exp(cum[p])) for p in pairs]
    BK = [jnp.concatenate([stack(b[p] * e_neg[p]), stack(k[p] * e_neg[p])], axis=0) for p in pairs]
    Bend_st = [stack(b[p] * e_end[p]) for p in pairs]
    Kend_st = [stack(k[p] * e_end[p]) for p in pairs]
    V_st = [stack(v[p]) for p in pairs]

    G = [_mm(jnp.concatenate([A_st[p], R_st[p]], axis=0), BK[p], pg, _NT) for p in pairs]
    Lab = [jnp.where(strict, G[p][:P2, :P2], 0.0) for p in pairs]
    Lak = [jnp.where(strict, G[p][:P2, P2:], 0.0) for p in pairs]
    Mrb = [jnp.where(incl, G[p][P2:, :P2], 0.0) for p in pairs]
    Mrk = [jnp.where(incl, G[p][P2:, P2:], 0.0) for p in pairs]

    Tinv = [eye + Lab[p] for p in pairs]
    Lp = Lab
    for _ in range(int(math.log2(CH)) - 1):
        Lp = [_mm(Lp[p], Lp[p], pi) for p in pairs]
        Tinv = [Tinv[p] + _mm(Tinv[p], Lp[p], pi) for p in pairs]

    LakV = [_mm(Lak[p], V_st[p], pa) for p in pairs]
    AU = [_mm(Tinv[p], jnp.concatenate([A_st[p], LakV[p]], axis=1), pa) for p in pairs]
    MM = [_mm(Mrb[p], AU[p], pa) for p in pairs]
    MV = [_mm(Mrk[p], V_st[p], pa) for p in pairs]
    BT = [_mm(Bend_st[p], AU[p], pa, _TN) for p in pairs]
    KV = [_mm(Kend_st[p], V_st[p], pa, _TN) for p in pairs]
    Rhat = [R_st[p] + MM[p][:, :LANES] for p in pairs]
    Mmat = [eye * jnp.exp(cum_end[p]) + BT[p][:, :LANES] for p in pairs]

    RS = [_mm(jnp.concatenate([Rhat[p], Mmat[p]], axis=0), state_ref[p], ps) for p in pairs]
    for p in pairs:
        state_ref[p] = jnp.where(same_head, RS[p][P2:] + BT[p][:, LANES:] + KV[p], 0.0)
    Y_st = [RS[p][:P2] + MM[p][:, LANES:] + MV[p] for p in pairs]
    y = [Y_st[p][:CH] + Y_st[p][CH:] for p in pairs]

    mean = [_sum01_right(y[p], ones_bd) * (1.0 / N) for p in pairs]
    yc = [y[p] - mean[p] for p in pairs]
    var = [_sum01_right(yc[p] * yc[p], ones_bd) * (1.0 / N) for p in pairs]
    bonus = [_sum01_right(r[p] * k[p] * rk_ref[:, sls[p]], ones_bd) for p in pairs]
    for p in pairs:
        yn = yc[p] * lax.rsqrt(var[p] + RWKV_GN_EPS) * lnw_ref[:, sls[p]] + lnb_ref[:, sls[p]]
        o_ref[:, sls[p]] = ((yn + bonus[p] * v[p]) * g_ref[:, sls[p]]).astype(o_ref.dtype)


def rwkv_scan(r, lw, k, v, kk, a, g, r_k, ln_w, ln_b, B, S):
    T, C = r.shape
    CH = RWKV_CHUNK
    nc = S // CH
    blk = pl.BlockSpec((CH, C), lambda b, c: (b * nc + c, 0))
    par = pl.BlockSpec((1, C), lambda b, c: (0, 0))
    return pl.pallas_call(
        _rwkv_scan_kernel,
        grid=(B, nc),
        in_specs=[blk] * 7 + [par] * 3,
        out_specs=blk,
        out_shape=jax.ShapeDtypeStruct((T, C), BF16),
        scratch_shapes=[pltpu.VMEM((C // LANES, LANES, LANES), F32)],
        compiler_params=_cparams("parallel", "arbitrary"),
        name="rwkv_scan",
    )(r, lw, k, v, kk, a, g, r_k.reshape(1, C), ln_w.reshape(1, C), ln_b.reshape(1, C))


def _rope_pairs(x, cos2, sin2):
    lane = lax.broadcasted_iota(jnp.int32, (1, LANES), 1)
    first_half = (lane % MLA_ROPE_DIM) < (MLA_ROPE_DIM // 2)
    partner = jnp.where(first_half, pltpu.roll(x, LANES - MLA_ROPE_DIM // 2, 1),
                        pltpu.roll(x, MLA_ROPE_DIM // 2, 1))
    return x * cos2 + partner * sin2


def _mla_cproj_kernel(h_ref, w_ref, cos_ref, sin_ref, c_ref, kr_ref):
    acc = _dot(h_ref[...], w_ref[...])
    NC = MLA_Q_LORA + MLA_KV_LORA
    c_ref[...] = acc[:, :NC]
    kr_ref[...] = _rope_pairs(acc[:, NC:], cos_ref[...], sin_ref[...]).astype(kr_ref.dtype)


def mla_cproj(h16, w_in, cos2, sin2, S, tm=512):
    T, D = h16.shape
    NC = MLA_Q_LORA + MLA_KV_LORA
    w = jnp.concatenate([w_in, w_in[:, NC:]], axis=1).astype(BF16)
    ns = S // tm
    return pl.pallas_call(
        _mla_cproj_kernel,
        grid=(T // tm,),
        in_specs=[pl.BlockSpec((tm, D), lambda i: (i, 0)),
                  pl.BlockSpec((D, NC + LANES), lambda i: (0, 0)),
                  pl.BlockSpec((tm, LANES), lambda i: (i % ns, 0)),
                  pl.BlockSpec((tm, LANES), lambda i: (i % ns, 0))],
        out_specs=[pl.BlockSpec((tm, NC), lambda i: (i, 0)),
                   pl.BlockSpec((tm, LANES), lambda i: (i, 0))],
        out_shape=[jax.ShapeDtypeStruct((T, NC), F32), jax.ShapeDtypeStruct((T, LANES), BF16)],
        compiler_params=_cparams("parallel"),
        name="mla_cproj",
    )(h16, w, cos2, sin2)


def _rms_up_kernel(rope_tile, c_ref, g_ref, w_ref, cos_ref, sin_ref, o_ref, cn_ref):
    j = pl.program_id(1)

    @pl.when(j == 0)
    def _():
        x = c_ref[...]
        ms = jnp.mean(x * x, axis=-1, keepdims=True)
        cn_ref[...] = (x * lax.rsqrt(ms + RMS_EPS) * g_ref[...]).astype(BF16)

    acc = _dot(cn_ref[...], w_ref[...])

    if rope_tile is None:
        o_ref[...] = acc.astype(o_ref.dtype)
    else:
        @pl.when(j != rope_tile)
        def _():
            o_ref[...] = acc.astype(o_ref.dtype)

        @pl.when(j == rope_tile)
        def _():
            cos2, sin2 = cos_ref[...], sin_ref[...]
            for gidx in range(acc.shape[1] // LANES):
                sl = slice(gidx * LANES, (gidx + 1) * LANES)
                o_ref[:, sl] = _rope_pairs(acc[:, sl], cos2, sin2).astype(o_ref.dtype)


def rms_up_proj(c, col_block, gain, w, cos2, sin2, S, rope_tile, tm=512, tn=1024):
    T = c.shape[0]
    K, N = w.shape
    ns = S // tm
    return pl.pallas_call(
        functools.partial(_rms_up_kernel, rope_tile),
        grid=(T // tm, N // tn),
        in_specs=[pl.BlockSpec((tm, K), lambda i, j: (i, col_block)),
                  pl.BlockSpec((1, K), lambda i, j: (0, 0)),
                  pl.BlockSpec((K, tn), lambda i, j: (0, j)),
                  pl.BlockSpec((tm, LANES), lambda i, j: (i % ns, 0)),
                  pl.BlockSpec((tm, LANES), lambda i, j: (i % ns, 0))],
        out_specs=pl.BlockSpec((tm, tn), lambda i, j: (i, j)),
        out_shape=jax.ShapeDtypeStruct((T, N), BF16),
        scratch_shapes=[pltpu.VMEM((tm, K), BF16)],
        compiler_params=_cparams("parallel", "arbitrary"),
        name="rms_up_proj",
    )(c, gain.reshape(1, K), w, cos2, sin2)


MLA_GROUP = 4


def _mla_attn_kernel(qn_ref, qr_ref, kv_ref, kr_ref, o_ref, kt_ref):
    qi = pl.program_id(2)
    TQ = qn_ref.shape[0]
    S = kv_ref.shape[0]
    G = MLA_GROUP
    lane = lax.broadcasted_iota(jnp.int32, (1, LANES), 1)

    @pl.when(qi == 0)
    def _():
        for j in range(G):
            for n in range(S // TQ):
                rows = slice(n * TQ, (n + 1) * TQ)
                kc = jnp.concatenate([kv_ref[rows, 2 * j * LANES:(2 * j + 1) * LANES], kr_ref[rows, :]],
                                     axis=1)
                kt_ref[j, :, rows] = kc.astype(F32).T.astype(BF16)

    c2 = (MLA_NOPE_DIM + MLA_ROPE_DIM) ** -0.5 * math.log2(math.e)
    row = lax.broadcasted_iota(jnp.int32, (TQ, TQ), 0)
    col = lax.broadcasted_iota(jnp.int32, (TQ, TQ), 1)
    qs = []
    for j in range(G):
        own = (lane // MLA_ROPE_DIM) == (j % 2)
        qr = qr_ref[:, (j // 2) * LANES:(j // 2 + 1) * LANES]
        qs.append(jnp.concatenate([qn_ref[:, j * LANES:(j + 1) * LANES],
                                   jnp.where(own, qr, jnp.zeros_like(qr))], axis=1))

    def tile(j, off, carry, diag):
        m_run, l_run, acc = carry
        t = _dot(qs[j], kt_ref[j, :, pl.ds(off, TQ)]) * c2
        if diag:
            t = jnp.where(col <= row, t, NEG_INF)
        m_new = jnp.maximum(m_run, jnp.max(t, axis=-1, keepdims=True))
        alpha = jnp.exp2(m_run - m_new)
        pr = jnp.exp2(t - m_new)
        l_new = alpha * l_run + (pr[:, :LANES] + pr[:, LANES:])
        v_t = kv_ref[pl.ds(off, TQ), (2 * j + 1) * LANES:(2 * j + 2) * LANES]
        acc_new = alpha * acc + _dot(pr.astype(BF16), v_t)
        return m_new, l_new, acc_new

    init = (jnp.full((TQ, 1), NEG_INF, F32), jnp.zeros((TQ, LANES), F32), jnp.zeros((TQ, LANES), F32))
    off_d = pl.multiple_of(qi * TQ, TQ)
    carry = tuple(tile(j, off_d, init, True) for j in range(G))

    def body(n, carry):
        off = pl.multiple_of(n * TQ, TQ)
        return tuple(tile(j, off, carry[j], False) for j in range(G))

    carry = lax.fori_loop(0, qi, body, carry)
    for j in range(G):
        m_run, l_run, acc = carry[j]
        o_ref[:, j * LANES:(j + 1) * LANES] = (
            acc / jnp.sum(l_run, axis=-1, keepdims=True)).astype(o_ref.dtype)


def mla_attention(q, kv, kr2, B, S, tq=256):
    T = B * S
    H, G = MLA_HEADS, MLA_GROUP
    nq = S // tq
    n_nope = H // G
    return pl.pallas_call(
        _mla_attn_kernel,
        grid=(B, H // G, nq),
        in_specs=[pl.BlockSpec((tq, G * LANES), lambda b, g, i: (b * nq + i, g)),
                  pl.BlockSpec((tq, G // 2 * LANES), lambda b, g, i: (b * nq + i, 2 * n_nope + g)),
                  pl.BlockSpec((S, 2 * G * LANES), lambda b, g, i: (b, g)),
                  pl.BlockSpec((S, LANES), lambda b, g, i: (b, 0))],
        out_specs=pl.BlockSpec((tq, G * LANES), lambda b, g, i: (b * nq + i, g)),
        out_shape=jax.ShapeDtypeStruct((T, H * MLA_V_DIM), BF16),
        scratch_shapes=[pltpu.VMEM((G, 2 * LANES, S), BF16)],
        compiler_params=_cparams("parallel", "parallel", "arbitrary"),
        name="mla_attention",
    )(q, q, kv, kr2)


def _rope_tables(S):
    half = MLA_ROPE_DIM // 2
    inv = ROPE_THETA ** (-jnp.arange(0, MLA_ROPE_DIM, 2, dtype=F32) / MLA_ROPE_DIM)
    ang = jnp.arange(S, dtype=F32)[:, None] * inv[None, :]
    cos, sin = jnp.cos(ang), jnp.sin(ang)
    reps = LANES // MLA_ROPE_DIM
    cos2 = jnp.tile(jnp.concatenate([cos, cos], axis=1), (1, reps))
    sin2 = jnp.tile(jnp.concatenate([-sin, sin], axis=1), (1, reps))
    assert cos2.shape == (S, LANES) and half * 2 * reps == LANES
    return cos2, sin2


def kernel(x, ev_w_in, ev_w_out, rw_mu, rw_w0, rw_w2, rw_a0, rw_a2, rw_g2, rw_k_k, rw_k_a, rw_r_k,
           rw_ln_w, rw_ln_b, rw_v0, rw_v1, rw_v2, od_w_in, od_q_norm, od_kv_norm, od_w_uq, od_w_ukv,
           od_w_out, ln_mix_g, ln_mix_b, ln_ffn_g, ln_ffn_b, moe_w_r, moe_b_r, moe_w1, moe_b1,
           moe_w2, moe_b2):
    B, S, D = x.shape
    T = B * S
    h32 = x.reshape(T, D)
    h16 = h32.astype(BF16)
    cos2, sin2 = _rope_tables(S)
    n_main = 3 * MOBA_DIM + 3 * RWKV_DIM
    lora_pad = -(-RWKV_LORA // LANES) * LANES
    qd = MLA_NOPE_DIM + MLA_ROPE_DIM
    perm = np.concatenate([
        (np.arange(MLA_HEADS)[:, None] * qd + np.arange(MLA_NOPE_DIM)[None, :]).reshape(-1),
        (np.arange(MLA_HEADS)[:, None] * qd + MLA_NOPE_DIM + np.arange(MLA_ROPE_DIM)[None, :]).reshape(-1)])
    v_first = None
    for layer in range(DEPTH):
        j = layer // 2
        if layer % 2 == 0:
            w_in = ev_w_in[j]
            w_main = w_in[:, :n_main].astype(BF16)
            w_lora = jnp.zeros((D, lora_pad), BF16).at[:, :RWKV_LORA].set(w_in[:, n_main:].astype(BF16))
            proj = matmul(h16, w_main, F32)
            plora = matmul(h16, w_lora, F32)
            a_out = moba_attention(proj, B, S)
            v_lora = None if j == 0 else (rw_v0[j - 1], rw_v1[j - 1], rw_v2[j - 1])
            r, lw, k, v, kk, a, g = rwkv_prep(proj, plora, S, rw_mu[j], rw_w0[j], rw_w2[j], rw_a0[j],
                                             rw_a2[j], rw_g2[j], rw_k_k[j], rw_k_a[j], v_first, v_lora)
            if j == 0:
                v_first = v
            b_out = rwkv_scan(r, lw, k, v, kk, a, g, rw_r_k[j], rw_ln_w[j], rw_ln_b[j], B, S)
            w_out = ev_w_out[j].astype(BF16)
            h32, h16 = proj_residual_ln([a_out, b_out], [w_out[:MOBA_DIM], w_out[MOBA_DIM:]], h32,
                                        ln_mix_g[layer], ln_mix_b[layer])
        else:
            c, kr2 = mla_cproj(h16, od_w_in[j], cos2, sin2, S)
            q = rms_up_proj(c, 0, od_q_norm[j], od_w_uq[j][:, perm].astype(BF16), cos2, sin2, S,
                            rope_tile=2)
            kv = rms_up_proj(c, 1, od_kv_norm[j], od_w_ukv[j].astype(BF16), cos2, sin2, S,
                             rope_tile=None)
            o = mla_attention(q, kv, kr2, B, S)
            h32, h16 = proj_residual_ln([o], [od_w_out[j].astype(BF16)], h32,
                                        ln_mix_g[layer], ln_mix_b[layer])
        h32, h16 = moe_layer(h32, h16, moe_w_r[layer], moe_b_r[layer], moe_w1[layer].astype(BF16),
                             moe_b1[layer], moe_w2[layer].astype(BF16), moe_b2[layer],
                             ln_ffn_g[layer], ln_ffn_b[layer])
    return h32.reshape(B, S, D)
```

```python
import functools
import math

import jax
import jax.numpy as jnp
import numpy as np
from jax import lax
from jax.experimental import pallas as pl
from jax.experimental.pallas import tpu as pltpu

F32 = jnp.float32
BF16 = jnp.bfloat16
HIGHEST = lax.Precision.HIGHEST

DEPTH = 4
MOBA_HEADS = 16
MOBA_HEAD_DIM = 64
MOBA_DIM = MOBA_HEADS * MOBA_HEAD_DIM
MOBA_BLOCK = 256
MOBA_TOPK = 3
RWKV_HEADS = 16
RWKV_HEAD_DIM = 64
RWKV_DIM = RWKV_HEADS * RWKV_HEAD_DIM
RWKV_DECAY_LORA = 64
RWKV_A_LORA = 64
RWKV_G_LORA = 160
RWKV_LORA = RWKV_DECAY_LORA + RWKV_A_LORA + RWKV_G_LORA
RWKV_GN_EPS = 64e-5
RWKV_CHUNK = 64
MLA_HEADS = 16
MLA_Q_LORA = 512
MLA_KV_LORA = 512
MLA_NOPE_DIM = 128
MLA_ROPE_DIM = 64
MLA_V_DIM = 128
ROPE_THETA = 10000.0
N_EXPERTS = 32
TOP_K = 4
D_EXPERT = 1024
SWIGLU_ALPHA = 1.702
SWIGLU_LIMIT = 7.0
MOE_ROWS = 256
DEEPNORM_ALPHA = (2 * DEPTH) ** 0.25
LN_EPS = 1e-5
RMS_EPS = 1e-6
NEG_INF = -1e30

LANES = 128
VMEM_LIMIT = 56 * 1024 * 1024


def _cparams(*sem):
    return pltpu.CompilerParams(dimension_semantics=sem, vmem_limit_bytes=VMEM_LIMIT)


def _dot(a, b, precision=None):
    return jnp.dot(a, b, preferred_element_type=F32, precision=precision)


def _dot_nt(a, b, precision=None):
    return lax.dot_general(a, b, (((1,), (1,)), ((), ())), preferred_element_type=F32,
                           precision=precision)


def _dot_tn(a, b, precision=None):
    return lax.dot_general(a, b, (((0,), (0,)), ((), ())), preferred_element_type=F32,
                           precision=precision)


def _mm_kernel(a_ref, w_ref, o_ref):
    o_ref[...] = _dot(a_ref[...], w_ref[...]).astype(o_ref.dtype)


def matmul(a, w, out_dtype, tm=1024, tn=512):
    M, K = a.shape
    N = w.shape[1]
    tm, tn = min(tm, M), min(tn, N)
    assert M % tm == 0 and N % tn == 0
    return pl.pallas_call(
        _mm_kernel,
        grid=(M // tm, N // tn),
        in_specs=[pl.BlockSpec((tm, K), lambda i, j: (i, 0)),
                  pl.BlockSpec((K, tn), lambda i, j: (0, j))],
        out_specs=pl.BlockSpec((tm, tn), lambda i, j: (i, j)),
        out_shape=jax.ShapeDtypeStruct((M, N), out_dtype),
        compiler_params=_cparams("parallel", "parallel"),
        name="matmul",
    )(a, w)


def _layer_norm_rows(z, g, b):
    mu = jnp.mean(z, axis=-1, keepdims=True)
    zc = z - mu
    var = jnp.mean(zc * zc, axis=-1, keepdims=True)
    return zc * lax.rsqrt(var + LN_EPS) * g + b


def _proj_ln_kernel(n_in, *refs):
    a_refs = refs[:n_in]
    w_refs = refs[n_in:2 * n_in]
    h_ref, g_ref, b_ref, o32_ref, o16_ref = refs[2 * n_in:]
    acc = DEEPNORM_ALPHA * h_ref[...]
    for a_ref, w_ref in zip(a_refs, w_refs):
        acc = acc + _dot(a_ref[...], w_ref[...])
    y = _layer_norm_rows(acc, g_ref[...], b_ref[...])
    o32_ref[...] = y
    o16_ref[...] = y.astype(BF16)


def proj_residual_ln(a_list, w_list, h, g, b, tm=256):
    M, D = h.shape
    n_in = len(a_list)
    in_specs = ([pl.BlockSpec((tm, a.shape[1]), lambda i: (i, 0)) for a in a_list]
                + [pl.BlockSpec(w.shape, lambda i: (0, 0)) for w in w_list]
                + [pl.BlockSpec((tm, D), lambda i: (i, 0)),
                   pl.BlockSpec((1, D), lambda i: (0, 0)),
                   pl.BlockSpec((1, D), lambda i: (0, 0))])
    return pl.pallas_call(
        functools.partial(_proj_ln_kernel, n_in),
        grid=(M // tm,),
        in_specs=in_specs,
        out_specs=[pl.BlockSpec((tm, D), lambda i: (i, 0)),
                   pl.BlockSpec((tm, D), lambda i: (i, 0))],
        out_shape=[jax.ShapeDtypeStruct((M, D), F32), jax.ShapeDtypeStruct((M, D), BF16)],
        compiler_params=_cparams("parallel"),
        name="proj_residual_ln",
    )(*a_list, *w_list, h, g.reshape(1, D), b.reshape(1, D))


def _combine_ln_kernel(y_ref, gate_ref, h_ref, g_ref, b_ref, o32_ref, o16_ref):
    acc = DEEPNORM_ALPHA * h_ref[...]
    gates = gate_ref[...]
    for k in range(TOP_K):
        acc = acc + y_ref[k].astype(F32) * gates[:, k:k + 1]
    y = _layer_norm_rows(acc, g_ref[...], b_ref[...])
    o32_ref[...] = y
    o16_ref[...] = y.astype(BF16)


def combine_residual_ln(y4, gates, h, g, b, tm=256):
    M, D = h.shape
    return pl.pallas_call(
        _combine_ln_kernel,
        grid=(M // tm,),
        in_specs=[pl.BlockSpec((TOP_K, tm, D), lambda i: (0, i, 0)),
                  pl.BlockSpec((tm, TOP_K), lambda i: (i, 0)),
                  pl.BlockSpec((tm, D), lambda i: (i, 0)),
                  pl.BlockSpec((1, D), lambda i: (0, 0)),
                  pl.BlockSpec((1, D), lambda i: (0, 0))],
        out_specs=[pl.BlockSpec((tm, D), lambda i: (i, 0)),
                   pl.BlockSpec((tm, D), lambda i: (i, 0))],
        out_shape=[jax.ShapeDtypeStruct((M, D), F32), jax.ShapeDtypeStruct((M, D), BF16)],
        compiler_params=_cparams("parallel"),
        name="combine_residual_ln",
    )(y4, gates, h, g.reshape(1, D), b.reshape(1, D))


def _router_kernel(h_ref, w_ref, b_ref, o_ref):
    o_ref[...] = _dot(h_ref[...], w_ref[...], precision=HIGHEST) + b_ref[...]


def router_logits(h, w_r, b_r, tm=512):
    M, D = h.shape
    w = jnp.zeros((D, LANES), F32).at[:, :N_EXPERTS].set(w_r)
    b = jnp.zeros((1, LANES), F32).at[0, :N_EXPERTS].set(b_r)
    out = pl.pallas_call(
        _router_kernel,
        grid=(M // tm,),
        in_specs=[pl.BlockSpec((tm, D), lambda i: (i, 0)),
                  pl.BlockSpec((D, LANES), lambda i: (0, 0)),
                  pl.BlockSpec((1, LANES), lambda i: (0, 0))],
        out_specs=pl.BlockSpec((tm, LANES), lambda i: (i, 0)),
        out_shape=jax.ShapeDtypeStruct((M, LANES), F32),
        compiler_params=_cparams("parallel"),
        name="router_logits",
    )(h, w, b)
    return out[:, :N_EXPERTS]


def _cast_kernel(x_ref, o_ref):
    o_ref[...] = x_ref[...].astype(o_ref.dtype)


def cast_layer_bf16(w, layer, rows=512):
    _, E, K, N = w.shape
    rows = min(rows, K)
    return pl.pallas_call(
        _cast_kernel,
        grid=(E, K // rows),
        in_specs=[pl.BlockSpec((None, 1, rows, N), lambda e, i: (layer, e, i, 0))],
        out_specs=pl.BlockSpec((1, rows, N), lambda e, i: (e, i, 0)),
        out_shape=jax.ShapeDtypeStruct((E, K, N), BF16),
        compiler_params=_cparams("parallel", "parallel"),
        name="cast_layer_bf16",
    )(w)


def _expert_kernel(be_ref, nv_ref, x_ref, w1_ref, b1_ref, w2_ref, b2_ref, o_ref):
    i = pl.program_id(0)

    @pl.when(i < nv_ref[0])
    def _():
        hgu = _dot(x_ref[...], w1_ref[0]) + b1_ref[0]
        gate = jnp.minimum(hgu[:, :D_EXPERT], SWIGLU_LIMIT)
        up = jnp.clip(hgu[:, D_EXPERT:], -SWIGLU_LIMIT, SWIGLU_LIMIT)
        act = gate * jax.nn.sigmoid(SWIGLU_ALPHA * gate) * (up + 1.0)
        o_ref[...] = (_dot(act.astype(BF16), w2_ref[0]) + b2_ref[0]).astype(o_ref.dtype)

    @pl.when(i >= nv_ref[0])
    def _():
        o_ref[...] = jnp.zeros_like(o_ref)


def expert_ffn(xbuf, block_e, n_valid, w1, b1, w2, b2):
    rows, D = xbuf.shape
    R = MOE_ROWS
    n_blocks = rows // R
    grid_spec = pltpu.PrefetchScalarGridSpec(
        num_scalar_prefetch=2,
        grid=(n_blocks,),
        in_specs=[pl.BlockSpec((R, D), lambda i, be, nv: (i, 0)),
                  pl.BlockSpec((1, D, 2 * D_EXPERT), lambda i, be, nv: (be[i], 0, 0)),
                  pl.BlockSpec((1, 1, 2 * D_EXPERT), lambda i, be, nv: (be[i], 0, 0)),
                  pl.BlockSpec((1, D_EXPERT, D), lambda i, be, nv: (be[i], 0, 0)),
                  pl.BlockSpec((1, 1, D), lambda i, be, nv: (be[i], 0, 0))],
        out_specs=pl.BlockSpec((R, D), lambda i, be, nv: (i, 0)),
    )
    return pl.pallas_call(
        _expert_kernel,
        grid_spec=grid_spec,
        out_shape=jax.ShapeDtypeStruct((rows, D), BF16),
        compiler_params=_cparams("arbitrary"),
        name="expert_ffn",
    )(block_e, n_valid, xbuf, w1, b1.reshape(N_EXPERTS, 1, -1), w2, b2.reshape(N_EXPERTS, 1, -1))


def moe_layer(h32, h16, w_r, b_r, w1, b1, w2, b2, ln_g, ln_b):
    T, D = h32.shape
    R = MOE_ROWS
    M = T * TOP_K
    logits = router_logits(h32, w_r, b_r)
    top_val, top_idx = lax.top_k(logits, TOP_K)
    gates = jax.nn.softmax(top_val, axis=-1)
    onehot = top_idx[:, :, None] == jnp.arange(N_EXPERTS, dtype=jnp.int32)[None, None, :]
    csum = jnp.cumsum(jnp.sum(onehot, axis=1, dtype=jnp.int32), axis=0)
    counts = csum[-1]
    padded = (counts + R - 1) // R * R
    pad_end = jnp.cumsum(padded)
    pad_start = pad_end - padded
    pos = jnp.sum(jnp.where(onehot, (csum - 1 + pad_start[None, :])[:, None, :], 0), axis=-1)
    n_blocks = (M + N_EXPERTS * (R - 1) + R - 1) // R
    rows = n_blocks * R
    src_tok = (jnp.arange(rows, dtype=jnp.int32) % T).at[pos.reshape(M)].set(
        jnp.arange(M, dtype=jnp.int32) // TOP_K)
    block_start = jnp.arange(n_blocks, dtype=jnp.int32) * R
    block_e = jnp.minimum(jnp.sum(pad_end[None, :] <= block_start[:, None], axis=1),
                          N_EXPERTS - 1).astype(jnp.int32)
    n_valid = (pad_end[-1] // R).astype(jnp.int32).reshape(1)
    xbuf = h16[src_tok]
    ybuf = expert_ffn(xbuf, block_e, n_valid, w1, b1, w2, b2)
    y4 = ybuf[pos.T]
    return combine_residual_ln(y4, gates, h32, ln_g, ln_b)


MOBA_GROUP = 2


def _moba_kernel(q_ref, k_ref, v_ref, o_ref, kmean_ref, kt_ref):
    pg = pl.program_id(1)
    cur = pl.program_id(2)
    BLK = MOBA_BLOCK
    GP = MOBA_GROUP
    nb = k_ref.shape[0] // BLK
    lane = lax.broadcasted_iota(jnp.int32, (1, LANES), 1)
    head0 = lane < MOBA_HEAD_DIM
    LOG2E = math.log2(math.e)

    @pl.when(cur == 0)
    def _():
        kmean_ref[...] = jnp.zeros_like(kmean_ref)
        for g in range(GP):
            for n in range(nb):
                kb = k_ref[n * BLK:(n + 1) * BLK, g * LANES:(g + 1) * LANES]
                kmean_ref[g, n:n + 1, :] = jnp.mean(kb, axis=0, keepdims=True)
                kt_ref[g, :, n * BLK:(n + 1) * BLK] = kb.T.astype(BF16)

    row = lax.broadcasted_iota(jnp.int32, (2 * BLK, 1), 0)
    lane_b = lax.broadcasted_iota(jnp.int32, (2 * BLK, LANES), 1)
    r_in = jnp.where(row >= BLK, row - BLK, row)
    col = lax.broadcasted_iota(jnp.int32, (1, BLK), 1)
    rel = r_in - col
    causal = rel >= 0
    c2 = MOBA_HEAD_DIM ** -0.5 * LOG2E

    qb, sel, slope2, srel = [], [], [], []
    for g in range(GP):
        q = q_ref[:, g * LANES:(g + 1) * LANES]
        qs = jnp.concatenate([jnp.where(head0, q, 0.0), jnp.where(head0, 0.0, q)], axis=0)
        head_idx = 2 * (pg * GP + g) + (row >= BLK).astype(jnp.int32) + 1
        sl2 = jnp.exp(head_idx.astype(F32) * (-8.0 * math.log(2.0) / MOBA_HEADS)) * LOG2E
        gate = _dot_nt(qs, kmean_ref[g], precision=HIGHEST)
        rank = jnp.zeros((2 * BLK, LANES), jnp.int32)
        for m in range(nb):
            gm = gate[:, m:m + 1]
            ahead = (gm > gate) | ((gm == gate) & (m < lane_b))
            rank = rank + jnp.where(ahead, 1, 0) * (m < cur).astype(jnp.int32)
        sel.append((rank < MOBA_TOPK) & (lane_b < cur))
        qb.append(qs.astype(BF16))
        slope2.append(sl2)
        srel.append(sl2 * rel.astype(F32))

    def tile(g, n, carry, diag):
        m_run, l_run, acc = carry
        off = pl.multiple_of(n * BLK, BLK)
        t = _dot(qb[g], kt_ref[g, :, pl.ds(off, BLK)]) * c2 - srel[g]
        if diag:
            t = jnp.where(causal, t, NEG_INF)
            m_new = jnp.maximum(m_run, jnp.max(t, axis=-1, keepdims=True))
            shift = m_new
        else:
            sel_n = jnp.sum(jnp.where(sel[g] & (lane_b == n), 1.0, 0.0), axis=-1, keepdims=True) > 0.5
            bias = jnp.where(sel_n, -slope2[g] * ((cur - n) * BLK).astype(F32), NEG_INF)
            m_new = jnp.maximum(m_run, jnp.max(t, axis=-1, keepdims=True) + bias)
            shift = m_new - bias
        alpha = jnp.exp2(m_run - m_new)
        pr = jnp.exp2(t - shift)
        l_new = alpha * l_run + (pr[:, :LANES] + pr[:, LANES:])
        v_t = v_ref[pl.ds(off, BLK), g * LANES:(g + 1) * LANES].astype(BF16)
        acc_new = alpha * acc + _dot(pr.astype(BF16), v_t)
        return m_new, l_new, acc_new

    init = (jnp.full((2 * BLK, 1), NEG_INF, F32), jnp.zeros((2 * BLK, LANES), F32),
            jnp.zeros((2 * BLK, LANES), F32))
    carry = tuple(tile(g, cur, init, True) for g in range(GP))

    def body(n, carry):
        return tuple(tile(g, n, carry[g], False) for g in range(GP))

    carry = lax.fori_loop(0, cur, body, carry)
    for g in range(GP):
        m_run, l_run, acc = carry[g]
        out = acc / jnp.sum(l_run, axis=-1, keepdims=True)
        o_ref[:, g * LANES:(g + 1) * LANES] = jnp.where(head0, out[:BLK], out[BLK:]).astype(o_ref.dtype)


def moba_attention(proj, B, S):
    T = B * S
    BLK = MOBA_BLOCK
    GP = MOBA_GROUP
    W = GP * LANES
    n_grp = MOBA_DIM // W
    nq = S // BLK
    return pl.pallas_call(
        _moba_kernel,
        grid=(B, n_grp, nq),
        in_specs=[pl.BlockSpec((BLK, W), lambda b, p, c: (b * nq + c, p)),
                  pl.BlockSpec((S, W), lambda b, p, c: (b, n_grp + p)),
                  pl.BlockSpec((S, W), lambda b, p, c: (b, 2 * n_grp + p))],
        out_specs=pl.BlockSpec((BLK, W), lambda b, p, c: (b * nq + c, p)),
        out_shape=jax.ShapeDtypeStruct((T, MOBA_DIM), BF16),
        scratch_shapes=[pltpu.VMEM((GP, LANES, LANES), F32), pltpu.VMEM((GP, LANES, S), BF16)],
        compiler_params=_cparams("parallel", "parallel", "arbitrary"),
        name="moba_attention",
    )(proj, proj, proj)


def _rwkv_prep_kernel(has_vres, S, *refs):
    if has_vres:
        (pm_ref, pl_ref, pm_prev_ref, pl_prev_ref, mu_m_ref, mu_l_ref, w0_ref, w2_ref, a0_ref, a2_ref,
         g2_ref, kk_ref, ka_ref, vfirst_ref, v0_ref, v1_ref, v2_ref,
         r_o, lw_o, k_o, v_o, kkn_o, a_o, g_o) = refs
    else:
        (pm_ref, pl_ref, pm_prev_ref, pl_prev_ref, mu_m_ref, mu_l_ref, w0_ref, w2_ref, a0_ref, a2_ref,
         g2_ref, kk_ref, ka_ref,
         r_o, lw_o, k_o, v_o, kkn_o, a_o, g_o) = refs
    i = pl.program_id(0)
    tm = pm_ref.shape[0]
    C = RWKV_DIM
    row = lax.broadcasted_iota(jnp.int32, (tm, 1), 0)
    seq_start = (i * tm) % S == 0

    def shifted(cur_ref, prev_ref, mu_ref):
        x = cur_ref[...]
        prev_row = jnp.where(seq_start, 0.0, prev_ref[7:8, :])
        xs = jnp.where(row == 0, prev_row, pltpu.roll(x, 1, 0))
        return x + (xs - x) * mu_ref[...]

    pm = shifted(pm_ref, pm_prev_ref, mu_m_ref)
    plo = shifted(pl_ref, pl_prev_ref, mu_l_ref)
    r = pm[:, :C]
    k = pm[:, C:2 * C]
    v = pm[:, 2 * C:]
    wd = plo[:, :RWKV_DECAY_LORA]
    ad = plo[:, RWKV_DECAY_LORA:RWKV_DECAY_LORA + RWKV_A_LORA]
    gd = plo[:, RWKV_DECAY_LORA + RWKV_A_LORA:RWKV_LORA]
    w = -jax.nn.softplus(-(w0_ref[...] + _dot(jnp.tanh(wd).astype(BF16), w2_ref[...]))) - 0.5
    a = jax.nn.sigmoid(a0_ref[...] + _dot(ad.astype(BF16), a2_ref[...]))
    g = _dot(jax.nn.sigmoid(gd).astype(BF16), g2_ref[...])
    if has_vres:
        lo = _dot(_dot(v.astype(BF16), v1_ref[...]).astype(BF16), v2_ref[...])
        v = v + (vfirst_ref[...] - v) * jax.nn.sigmoid(v0_ref[...] + lo)
    kk = k * kk_ref[...]
    hid_r = lax.broadcasted_iota(jnp.int32, (LANES, LANES), 0) // RWKV_HEAD_DIM
    hid_c = lax.broadcasted_iota(jnp.int32, (LANES, LANES), 1) // RWKV_HEAD_DIM
    ones_bd = (hid_r == hid_c).astype(F32)
    for pp in range(C // LANES):
        sl = slice(pp * LANES, (pp + 1) * LANES)
        kkp = kk[:, sl]
        ss = _dot(kkp * kkp, ones_bd, precision=HIGHEST)
        kkn_o[:, sl] = kkp * lax.rsqrt(jnp.maximum(ss, 1e-24))
    r_o[...] = r
    lw_o[...] = -jnp.exp(w)
    k_o[...] = k * (1.0 + (a - 1.0) * ka_ref[...])
    v_o[...] = v
    a_o[...] = a
    g_o[...] = g


def rwkv_prep(proj, plora, S, mu, w0, w2, a0, a2, g2, k_k, k_a, v_first, v_lora, tm=256):
    T = proj.shape[0]
    C = RWKV_DIM
    LP = plora.shape[1]
    has_vres = v_lora is not None
    mu_m = mu[:3 * C].reshape(1, 3 * C)
    mu_l = jnp.zeros((1, LP), F32).at[0, :RWKV_LORA].set(mu[3 * C:])
    row = lambda z: z.reshape(1, -1)
    full = lambda z: pl.BlockSpec(z.shape, lambda i: (0,) * z.ndim)
    args = [proj, plora, proj, plora, mu_m, mu_l, row(w0), w2.astype(BF16), row(a0), a2.astype(BF16),
            g2.astype(BF16), row(k_k), row(k_a)]
    in_specs = [pl.BlockSpec((tm, 3 * C), lambda i: (i, 1)),
                pl.BlockSpec((tm, LP), lambda i: (i, 0)),
                pl.BlockSpec((8, 3 * C), lambda i: (jnp.maximum(i * (tm // 8) - 1, 0), 1)),
                pl.BlockSpec((8, LP), lambda i: (jnp.maximum(i * (tm // 8) - 1, 0), 0))]
    in_specs += [full(z) for z in args[4:]]
    if has_vres:
        v0, v1, v2 = v_lora
        extra = [v_first, row(v0), v1.astype(BF16), v2.astype(BF16)]
        args += extra
        in_specs += [pl.BlockSpec((tm, C), lambda i: (i, 0))] + [full(z) for z in extra[1:]]
    out_spec = pl.BlockSpec((tm, C), lambda i: (i, 0))
    return pl.pallas_call(
        functools.partial(_rwkv_prep_kernel, has_vres, S),
        grid=(T // tm,),
        in_specs=in_specs,
        out_specs=[out_spec] * 7,
        out_shape=[jax.ShapeDtypeStruct((T, C), F32)] * 7,
        compiler_params=_cparams("parallel"),
        name="rwkv_prep",
    )(*args)


def _split3_bf16(x):
    x0 = x.astype(BF16)
    r1 = x - x0.astype(F32)
    x1 = r1.astype(BF16)
    x2 = (r1 - x1.astype(F32)).astype(BF16)
    return x0, x1, x2


def _sum01_left(m01, x):
    x0, x1, x2 = _split3_bf16(x)
    return _dot(m01, x0) + (_dot(m01, x1) + _dot(m01, x2))


def _sum01_right(x, m01):
    x0, x1, x2 = _split3_bf16(x)
    return _dot(x0, m01) + (_dot(x1, m01) + _dot(x2, m01))


_NN = (((1,), (0,)), ((), ()))
_NT = (((1,), (1,)), ((), ()))
_TN = (((0,), (0,)), ((), ()))


def _mm(a, b, passes, dims=_NN):
    dg = lambda x, y: lax.dot_general(x, y, dims, preferred_element_type=F32)
    if passes == 1:
        return dg(a.astype(BF16), b.astype(BF16))
    a_hi = a.astype(BF16)
    a_lo = (a - a_hi.astype(F32)).astype(BF16)
    b_hi = b.astype(BF16)
    b_lo = (b - b_hi.astype(F32)).astype(BF16)
    return dg(a_hi, b_hi) + (dg(a_hi, b_lo) + dg(a_lo, b_hi))


RWKV_PASSES = dict(gram=1, inv=1, apply=1, state=3)


def _rwkv_scan_kernel(r_ref, lw_ref, k_ref, v_ref, kk_ref, a_ref, g_ref, rk_ref, lnw_ref, lnb_ref,
                      o_ref, state_ref):
    c = pl.program_id(1)
    CH = RWKV_CHUNK
    N = RWKV_HEAD_DIM
    P2 = 2 * CH
    NP = RWKV_DIM // LANES
    pg, pi, pa, ps = (RWKV_PASSES[n] for n in ("gram", "inv", "apply", "state"))

    @pl.when(c == 0)
    def _():
        state_ref[...] = jnp.zeros_like(state_ref)

    lane = lax.broadcasted_iota(jnp.int32, (1, LANES), 1)
    head0 = lane < N
    ri = lax.broadcasted_iota(jnp.int32, (P2, P2), 0)
    ci = lax.broadcasted_iota(jnp.int32, (P2, P2), 1)
    same_head = (ri // CH) == (ci // CH)
    strict = (ri % CH) > (ci % CH)
    incl = (ri % CH) >= (ci % CH)
    eye = (ri == ci).astype(F32)
    ones_bd = same_head.astype(BF16)
    ti = lax.broadcasted_iota(jnp.int32, (CH, CH), 0)
    si = lax.broadcasted_iota(jnp.int32, (CH, CH), 1)
    tril_incl = (ti >= si).astype(BF16)

    def stack(x):
        return jnp.concatenate([jnp.where(head0, x, 0.0), jnp.where(head0, 0.0, x)], axis=0)

    pairs = range(NP)
    sls = [slice(p * LANES, (p + 1) * LANES) for p in pairs]
    r = [r_ref[:, sl] for sl in sls]
    k = [k_ref[:, sl] for sl in sls]
    v = [v_ref[:, sl] for sl in sls]
    kk = [kk_ref[:, sl] for sl in sls]
    lw = [lw_ref[:, sl] for sl in sls]
    cum = [_sum01_left(tril_incl, lw[p]) for p in pairs]
    cum_end = [cum[p][CH - 1:CH, :] for p in pairs]
    b = [kk[p] * a_ref[:, sls[p]] for p in pairs]
    e_neg = [jnp.exp(-cum[p]) for p in pairs]
    e_end = [jnp.exp(cum_end[p] - cum[p]) for p in pairs]
    A_st = [stack(-kk[p] * jnp.exp(cum[p] - lw[p])) for p in pairs]
    R_st = [stack(r[p] * jnp.exp(cum[p])) for p in pairs]
    BK = [jnp.concatenate([stack(b[p] * e_neg[p]), stack(k[p] * e_neg[p])], axis=0) for p in pairs]
    Bend_st = [stack(b[p] * e_end[p]) for p in pairs]
    Kend_st = [stack(k[p] * e_end[p]) for p in pairs]
    V_st = [stack(v[p]) for p in pairs]

    G = [_mm(jnp.concatenate([A_st[p], R_st[p]], axis=0), BK[p], pg, _NT) for p in pairs]
    Lab = [jnp.where(strict, G[p][:P2, :P2], 0.0) for p in pairs]
    Lak = [jnp.where(strict, G[p][:P2, P2:], 0.0) for p in pairs]
    Mrb = [jnp.where(incl, G[p][P2:, :P2], 0.0) for p in pairs]
    Mrk = [jnp.where(incl, G[p][P2:, P2:], 0.0) for p in pairs]

    Tinv = [eye + Lab[p] for p in pairs]
    Lp = Lab
    for _ in range(int(math.log2(CH)) - 1):
        Lp = [_mm(Lp[p], Lp[p], pi) for p in pairs]
        Tinv = [Tinv[p] + _mm(Tinv[p], Lp[p], pi) for p in pairs]

    LakV = [_mm(Lak[p], V_st[p], pa) for p in pairs]
    AU = [_mm(Tinv[p], jnp.concatenate([A_st[p], LakV[p]], axis=1), pa) for p in pairs]
    MM = [_mm(Mrb[p], AU[p], pa) for p in pairs]
    MV = [_mm(Mrk[p], V_st[p], pa) for p in pairs]
    BT = [_mm(Bend_st[p], AU[p], pa, _TN) for p in pairs]
    KV = [_mm(Kend_st[p], V_st[p], pa, _TN) for p in pairs]
    Rhat = [R_st[p] + MM[p][:, :LANES] for p in pairs]
    Mmat = [eye * jnp.exp(cum_end[p]) + BT[p][:, :LANES] for p in pairs]

    RS = [_mm(jnp.concatenate([Rhat[p], Mmat[p]], axis=0), state_ref[p], ps) for p in pairs]
    for p in pairs:
        state_ref[p] = jnp.where(same_head, RS[p][P2:] + BT[p][:, LANES:] + KV[p], 0.0)
    Y_st = [RS[p][:P2] + MM[p][:, LANES:] + MV[p] for p in pairs]
    y = [Y_st[p][:CH] + Y_st[p][CH:] for p in pairs]

    mean = [_sum01_right(y[p], ones_bd) * (1.0 / N) for p in pairs]
    yc = [y[p] - mean[p] for p in pairs]
    var = [_sum01_right(yc[p] * yc[p], ones_bd) * (1.0 / N) for p in pairs]
    bonus = [_sum01_right(r[p] * k[p] * rk_ref[:, sls[p]], ones_bd) for p in pairs]
    for p in pairs:
        yn = yc[p] * lax.rsqrt(var[p] + RWKV_GN_EPS) * lnw_ref[:, sls[p]] + lnb_ref[:, sls[p]]
        o_ref[:, sls[p]] = ((yn + bonus[p] * v[p]) * g_ref[:, sls[p]]).astype(o_ref.dtype)


def rwkv_scan(r, lw, k, v, kk, a, g, r_k, ln_w, ln_b, B, S):
    T, C = r.shape
    CH = RWKV_CHUNK
    nc = S // CH
    blk = pl.BlockSpec((CH, C), lambda b, c: (b * nc + c, 0))
    par = pl.BlockSpec((1, C), lambda b, c: (0, 0))
    return pl.pallas_call(
        _rwkv_scan_kernel,
        grid=(B, nc),
        in_specs=[blk] * 7 + [par] * 3,
        out_specs=blk,
        out_shape=jax.ShapeDtypeStruct((T, C), BF16),
        scratch_shapes=[pltpu.VMEM((C // LANES, LANES, LANES), F32)],
        compiler_params=_cparams("parallel", "arbitrary"),
        name="rwkv_scan",
    )(r, lw, k, v, kk, a, g, r_k.reshape(1, C), ln_w.reshape(1, C), ln_b.reshape(1, C))


def _rope_pairs(x, cos2, sin2):
    lane = lax.broadcasted_iota(jnp.int32, (1, LANES), 1)
    first_half = (lane % MLA_ROPE_DIM) < (MLA_ROPE_DIM // 2)
    partner = jnp.where(first_half, pltpu.roll(x, LANES - MLA_ROPE_DIM // 2, 1),
                        pltpu.roll(x, MLA_ROPE_DIM // 2, 1))
    return x * cos2 + partner * sin2


def _mla_cproj_kernel(h_ref, w_ref, cos_ref, sin_ref, c_ref, kr_ref):
    acc = _dot(h_ref[...], w_ref[...])
    NC = MLA_Q_LORA + MLA_KV_LORA
    c_ref[...] = acc[:, :NC]
    kr_ref[...] = _rope_pairs(acc[:, NC:], cos_ref[...], sin_ref[...]).astype(kr_ref.dtype)


def mla_cproj(h16, w_in, cos2, sin2, S, tm=512):
    T, D = h16.shape
    NC = MLA_Q_LORA + MLA_KV_LORA
    w = jnp.concatenate([w_in, w_in[:, NC:]], axis=1).astype(BF16)
    ns = S // tm
    return pl.pallas_call(
        _mla_cproj_kernel,
        grid=(T // tm,),
        in_specs=[pl.BlockSpec((tm, D), lambda i: (i, 0)),
                  pl.BlockSpec((D, NC + LANES), lambda i: (0, 0)),
                  pl.BlockSpec((tm, LANES), lambda i: (i % ns, 0)),
                  pl.BlockSpec((tm, LANES), lambda i: (i % ns, 0))],
        out_specs=[pl.BlockSpec((tm, NC), lambda i: (i, 0)),
                   pl.BlockSpec((tm, LANES), lambda i: (i, 0))],
        out_shape=[jax.ShapeDtypeStruct((T, NC), F32), jax.ShapeDtypeStruct((T, LANES), BF16)],
        compiler_params=_cparams("parallel"),
        name="mla_cproj",
    )(h16, w, cos2, sin2)


def _rms_up_kernel(rope_tile, c_ref, g_ref, w_ref, cos_ref, sin_ref, o_ref, cn_ref):
    j = pl.program_id(1)

    @pl.when(j == 0)
    def _():
        x = c_ref[...]
        ms = jnp.mean(x * x, axis=-1, keepdims=True)
        cn_ref[...] = (x * lax.rsqrt(ms + RMS_EPS) * g_ref[...]).astype(BF16)

    acc = _dot(cn_ref[...], w_ref[...])

    if rope_tile is None:
        o_ref[...] = acc.astype(o_ref.dtype)
    else:
        @pl.when(j != rope_tile)
        def _():
            o_ref[...] = acc.astype(o_ref.dtype)

        @pl.when(j == rope_tile)
        def _():
            cos2, sin2 = cos_ref[...], sin_ref[...]
            for gidx in range(acc.shape[1] // LANES):
                sl = slice(gidx * LANES, (gidx + 1) * LANES)
                o_ref[:, sl] = _rope_pairs(acc[:, sl], cos2, sin2).astype(o_ref.dtype)


def rms_up_proj(c, col_block, gain, w, cos2, sin2, S, rope_tile, tm=512, tn=1024):
    T = c.shape[0]
    K, N = w.shape
    ns = S // tm
    return pl.pallas_call(
        functools.partial(_rms_up_kernel, rope_tile),
        grid=(T // tm, N // tn),
        in_specs=[pl.BlockSpec((tm, K), lambda i, j: (i, col_block)),
                  pl.BlockSpec((1, K), lambda i, j: (0, 0)),
                  pl.BlockSpec((K, tn), lambda i, j: (0, j)),
                  pl.BlockSpec((tm, LANES), lambda i, j: (i % ns, 0)),
                  pl.BlockSpec((tm, LANES), lambda i, j: (i % ns, 0))],
        out_specs=pl.BlockSpec((tm, tn), lambda i, j: (i, j)),
        out_shape=jax.ShapeDtypeStruct((T, N), BF16),
        scratch_shapes=[pltpu.VMEM((tm, K), BF16)],
        compiler_params=_cparams("parallel", "arbitrary"),
        name="rms_up_proj",
    )(c, gain.reshape(1, K), w, cos2, sin2)


MLA_GROUP = 4


def _mla_attn_kernel(qn_ref, qr_ref, kv_ref, kr_ref, o_ref, kt_ref):
    qi = pl.program_id(2)
    TQ = qn_ref.shape[0]
    S = kv_ref.shape[0]
    G = MLA_GROUP
    lane = lax.broadcasted_iota(jnp.int32, (1, LANES), 1)

    @pl.when(qi == 0)
    def _():
        for j in range(G):
            for n in range(S // TQ):
                rows = slice(n * TQ, (n + 1) * TQ)
                kc = jnp.concatenate([kv_ref[rows, 2 * j * LANES:(2 * j + 1) * LANES], kr_ref[rows, :]],
                                     axis=1)
                kt_ref[j, :, rows] = kc.astype(F32).T.astype(BF16)

    c2 = (MLA_NOPE_DIM + MLA_ROPE_DIM) ** -0.5 * math.log2(math.e)
    row = lax.broadcasted_iota(jnp.int32, (TQ, TQ), 0)
    col = lax.broadcasted_iota(jnp.int32, (TQ, TQ), 1)
    qs = []
    for j in range(G):
        own = (lane // MLA_ROPE_DIM) == (j % 2)
        qr = qr_ref[:, (j // 2) * LANES:(j // 2 + 1) * LANES]
        qs.append(jnp.concatenate([qn_ref[:, j * LANES:(j + 1) * LANES],
                                   jnp.where(own, qr, jnp.zeros_like(qr))], axis=1))

    def tile(j, off, carry, diag):
        m_run, l_run, acc = carry
        t = _dot(qs[j], kt_ref[j, :, pl.ds(off, TQ)]) * c2
        if diag:
            t = jnp.where(col <= row, t, NEG_INF)
        m_new = jnp.maximum(m_run, jnp.max(t, axis=-1, keepdims=True))
        alpha = jnp.exp2(m_run - m_new)
        pr = jnp.exp2(t - m_new)
        l_new = alpha * l_run + (pr[:, :LANES] + pr[:, LANES:])
        v_t = kv_ref[pl.ds(off, TQ), (2 * j + 1) * LANES:(2 * j + 2) * LANES]
        acc_new = alpha * acc + _dot(pr.astype(BF16), v_t)
        return m_new, l_new, acc_new

    init = (jnp.full((TQ, 1), NEG_INF, F32), jnp.zeros((TQ, LANES), F32), jnp.zeros((TQ, LANES), F32))
    off_d = pl.multiple_of(qi * TQ, TQ)
    carry = tuple(tile(j, off_d, init, True) for j in range(G))

    def body(n, carry):
        off = pl.multiple_of(n * TQ, TQ)
        return tuple(tile(j, off, carry[j], False) for j in range(G))

    carry = lax.fori_loop(0, qi, body, carry)
    for j in range(G):
        m_run, l_run, acc = carry[j]
        o_ref[:, j * LANES:(j + 1) * LANES] = (
            acc / jnp.sum(l_run, axis=-1, keepdims=True)).astype(o_ref.dtype)


def mla_attention(q, kv, kr2, B, S, tq=256):
    T = B * S
    H, G = MLA_HEADS, MLA_GROUP
    nq = S // tq
    n_nope = H // G
    return pl.pallas_call(
        _mla_attn_kernel,
        grid=(B, H // G, nq),
        in_specs=[pl.BlockSpec((tq, G * LANES), lambda b, g, i: (b * nq + i, g)),
                  pl.BlockSpec((tq, G // 2 * LANES), lambda b, g, i: (b * nq + i, 2 * n_nope + g)),
                  pl.BlockSpec((S, 2 * G * LANES), lambda b, g, i: (b, g)),
                  pl.BlockSpec((S, LANES), lambda b, g, i: (b, 0))],
        out_specs=pl.BlockSpec((tq, G * LANES), lambda b, g, i: (b * nq + i, g)),
        out_shape=jax.ShapeDtypeStruct((T, H * MLA_V_DIM), BF16),
        scratch_shapes=[pltpu.VMEM((G, 2 * LANES, S), BF16)],
        compiler_params=_cparams("parallel", "parallel", "arbitrary"),
        name="mla_attention",
    )(q, q, kv, kr2)


def _rope_tables(S):
    half = MLA_ROPE_DIM // 2
    inv = ROPE_THETA ** (-jnp.arange(0, MLA_ROPE_DIM, 2, dtype=F32) / MLA_ROPE_DIM)
    ang = jnp.arange(S, dtype=F32)[:, None] * inv[None, :]
    cos, sin = jnp.cos(ang), jnp.sin(ang)
    reps = LANES // MLA_ROPE_DIM
    cos2 = jnp.tile(jnp.concatenate([cos, cos], axis=1), (1, reps))
    sin2 = jnp.tile(jnp.concatenate([-sin, sin], axis=1), (1, reps))
    assert cos2.shape == (S, LANES) and half * 2 * reps == LANES
    return cos2, sin2


def kernel(x, ev_w_in, ev_w_out, rw_mu, rw_w0, rw_w2, rw_a0, rw_a2, rw_g2, rw_k_k, rw_k_a, rw_r_k,
           rw_ln_w, rw_ln_b, rw_v0, rw_v1, rw_v2, od_w_in, od_q_norm, od_kv_norm, od_w_uq, od_w_ukv,
           od_w_out, ln_mix_g, ln_mix_b, ln_ffn_g, ln_ffn_b, moe_w_r, moe_b_r, moe_w1, moe_b1,
           moe_w2, moe_b2):
    B, S, D = x.shape
    T = B * S
    h32 = x.reshape(T, D)
    h16 = h32.astype(BF16)
    cos2, sin2 = _rope_tables(S)
    n_main = 3 * MOBA_DIM + 3 * RWKV_DIM
    lora_pad = -(-RWKV_LORA // LANES) * LANES
    qd = MLA_NOPE_DIM + MLA_ROPE_DIM
    perm = np.concatenate([
        (np.arange(MLA_HEADS)[:, None] * qd + np.arange(MLA_NOPE_DIM)[None, :]).reshape(-1),
        (np.arange(MLA_HEADS)[:, None] * qd + MLA_NOPE_DIM + np.arange(MLA_ROPE_DIM)[None, :]).reshape(-1)])
    v_first = None
    for layer in range(DEPTH):
        j = layer // 2
        if layer % 2 == 0:
            w_in = ev_w_in[j]
            w_main = w_in[:, :n_main].astype(BF16)
            w_lora = jnp.zeros((D, lora_pad), BF16).at[:, :RWKV_LORA].set(w_in[:, n_main:].astype(BF16))
            proj = matmul(h16, w_main, F32)
            plora = matmul(h16, w_lora, F32)
            a_out = moba_attention(proj, B, S)
            v_lora = None if j == 0 else (rw_v0[j - 1], rw_v1[j - 1], rw_v2[j - 1])
            r, lw, k, v, kk, a, g = rwkv_prep(proj, plora, S, rw_mu[j], rw_w0[j], rw_w2[j], rw_a0[j],
                                             rw_a2[j], rw_g2[j], rw_k_k[j], rw_k_a[j], v_first, v_lora)
            if j == 0:
                v_first = v
            b_out = rwkv_scan(r, lw, k, v, kk, a, g, rw_r_k[j], rw_ln_w[j], rw_ln_b[j], B, S)
            w_out = ev_w_out[j].astype(BF16)
            h32, h16 = proj_residual_ln([a_out, b_out], [w_out[:MOBA_DIM], w_out[MOBA_DIM:]], h32,
                                        ln_mix_g[layer], ln_mix_b[layer])
        else:
            c, kr2 = mla_cproj(h16, od_w_in[j], cos2, sin2, S)
            q = rms_up_proj(c, 0, od_q_norm[j], od_w_uq[j][:, perm].astype(BF16), cos2, sin2, S,
                            rope_tile=2)
            kv = rms_up_proj(c, 1, od_kv_norm[j], od_w_ukv[j].astype(BF16), cos2, sin2, S,
                             rope_tile=None)
            o = mla_attention(q, kv, kr2, B, S)
            h32, h16 = proj_residual_ln([o], [od_w_out[j].astype(BF16)], h32,
                                        ln_mix_g[layer], ln_mix_b[layer])
        h32, h16 = moe_layer(h32, h16, moe_w_r[layer], moe_b_r[layer], cast_layer_bf16(moe_w1, layer),
                             moe_b1[layer], cast_layer_bf16(moe_w2, layer), moe_b2[layer],
                             ln_ffn_g[layer], ln_ffn_b[layer])
    return h32.reshape(B, S, D)
```

```python
import functools
import math

import jax
import jax.numpy as jnp
import numpy as np
from jax import lax
from jax.experimental import pallas as pl
from jax.experimental.pallas import tpu as pltpu

F32 = jnp.float32
BF16 = jnp.bfloat16
HIGHEST = lax.Precision.HIGHEST

DEPTH = 4
MOBA_HEADS = 16
MOBA_HEAD_DIM = 64
MOBA_DIM = MOBA_HEADS * MOBA_HEAD_DIM
MOBA_BLOCK = 256
MOBA_TOPK = 3
RWKV_HEADS = 16
RWKV_HEAD_DIM = 64
RWKV_DIM = RWKV_HEADS * RWKV_HEAD_DIM
RWKV_DECAY_LORA = 64
RWKV_A_LORA = 64
RWKV_G_LORA = 160
RWKV_LORA = RWKV_DECAY_LORA + RWKV_A_LORA + RWKV_G_LORA
RWKV_GN_EPS = 64e-5
RWKV_CHUNK = 64
MLA_HEADS = 16
MLA_Q_LORA = 512
MLA_KV_LORA = 512
MLA_NOPE_DIM = 128
MLA_ROPE_DIM = 64
MLA_V_DIM = 128
ROPE_THETA = 10000.0
N_EXPERTS = 32
TOP_K = 4
D_EXPERT = 1024
SWIGLU_ALPHA = 1.702
SWIGLU_LIMIT = 7.0
MOE_ROWS = 256
DEEPNORM_ALPHA = (2 * DEPTH) ** 0.25
LN_EPS = 1e-5
RMS_EPS = 1e-6
NEG_INF = -1e30

LANES = 128
VMEM_LIMIT = 56 * 1024 * 1024


def _cparams(*sem):
    return pltpu.CompilerParams(dimension_semantics=sem, vmem_limit_bytes=VMEM_LIMIT)


def _dot(a, b, precision=None):
    return jnp.dot(a, b, preferred_element_type=F32, precision=precision)


def _dot_nt(a, b, precision=None):
    return lax.dot_general(a, b, (((1,), (1,)), ((), ())), preferred_element_type=F32,
                           precision=precision)


def _dot_tn(a, b, precision=None):
    return lax.dot_general(a, b, (((0,), (0,)), ((), ())), preferred_element_type=F32,
                           precision=precision)


def _mm_kernel(a_ref, w_ref, o_ref):
    o_ref[...] = _dot(a_ref[...], w_ref[...]).astype(o_ref.dtype)


def matmul(a, w, out_dtype, tm=1024, tn=512):
    M, K = a.shape
    N = w.shape[1]
    tm, tn = min(tm, M), min(tn, N)
    assert M % tm == 0 and N % tn == 0
    return pl.pallas_call(
        _mm_kernel,
        grid=(M // tm, N // tn),
        in_specs=[pl.BlockSpec((tm, K), lambda i, j: (i, 0)),
                  pl.BlockSpec((K, tn), lambda i, j: (0, j))],
        out_specs=pl.BlockSpec((tm, tn), lambda i, j: (i, j)),
        out_shape=jax.ShapeDtypeStruct((M, N), out_dtype),
        compiler_params=_cparams("parallel", "parallel"),
        name="matmul",
    )(a, w)


def _layer_norm_rows(z, g, b):
    mu = jnp.mean(z, axis=-1, keepdims=True)
    zc = z - mu
    var = jnp.mean(zc * zc, axis=-1, keepdims=True)
    return zc * lax.rsqrt(var + LN_EPS) * g + b


def _proj_ln_kernel(n_in, *refs):
    a_refs = refs[:n_in]
    w_refs = refs[n_in:2 * n_in]
    h_ref, g_ref, b_ref, o32_ref, o16_ref = refs[2 * n_in:]
    acc = DEEPNORM_ALPHA * h_ref[...]
    for a_ref, w_ref in zip(a_refs, w_refs):
        acc = acc + _dot(a_ref[...], w_ref[...])
    y = _layer_norm_rows(acc, g_ref[...], b_ref[...])
    o32_ref[...] = y
    o16_ref[...] = y.astype(BF16)


def proj_residual_ln(a_list, w_list, h, g, b, tm=256):
    M, D = h.shape
    n_in = len(a_list)
    in_specs = ([pl.BlockSpec((tm, a.shape[1]), lambda i: (i, 0)) for a in a_list]
                + [pl.BlockSpec(w.shape, lambda i: (0, 0)) for w in w_list]
                + [pl.BlockSpec((tm, D), lambda i: (i, 0)),
                   pl.BlockSpec((1, D), lambda i: (0, 0)),
                   pl.BlockSpec((1, D), lambda i: (0, 0))])
    return pl.pallas_call(
        functools.partial(_proj_ln_kernel, n_in),
        grid=(M // tm,),
        in_specs=in_specs,
        out_specs=[pl.BlockSpec((tm, D), lambda i: (i, 0)),
                   pl.BlockSpec((tm, D), lambda i: (i, 0))],
        out_shape=[jax.ShapeDtypeStruct((M, D), F32), jax.ShapeDtypeStruct((M, D), BF16)],
        compiler_params=_cparams("parallel"),
        name="proj_residual_ln",
    )(*a_list, *w_list, h, g.reshape(1, D), b.reshape(1, D))


def _combine_ln_kernel(y_ref, gate_ref, h_ref, g_ref, b_ref, o32_ref, o16_ref):
    acc = DEEPNORM_ALPHA * h_ref[...]
    gates = gate_ref[...]
    for k in range(TOP_K):
        acc = acc + y_ref[k].astype(F32) * gates[:, k:k + 1]
    y = _layer_norm_rows(acc, g_ref[...], b_ref[...])
    o32_ref[...] = y
    o16_ref[...] = y.astype(BF16)


def combine_residual_ln(y4, gates, h, g, b, tm=256):
    M, D = h.shape
    return pl.pallas_call(
        _combine_ln_kernel,
        grid=(M // tm,),
        in_specs=[pl.BlockSpec((TOP_K, tm, D), lambda i: (0, i, 0)),
                  pl.BlockSpec((tm, TOP_K), lambda i: (i, 0)),
                  pl.BlockSpec((tm, D), lambda i: (i, 0)),
                  pl.BlockSpec((1, D), lambda i: (0, 0)),
                  pl.BlockSpec((1, D), lambda i: (0, 0))],
        out_specs=[pl.BlockSpec((tm, D), lambda i: (i, 0)),
                   pl.BlockSpec((tm, D), lambda i: (i, 0))],
        out_shape=[jax.ShapeDtypeStruct((M, D), F32), jax.ShapeDtypeStruct((M, D), BF16)],
        compiler_params=_cparams("parallel"),
        name="combine_residual_ln",
    )(y4, gates, h, g.reshape(1, D), b.reshape(1, D))


def _router_kernel(h_ref, w_ref, b_ref, o_ref):
    o_ref[...] = _dot(h_ref[...], w_ref[...], precision=HIGHEST) + b_ref[...]


def router_logits(h, w_r, b_r, tm=512):
    M, D = h.shape
    w = jnp.zeros((D, LANES), F32).at[:, :N_EXPERTS].set(w_r)
    b = jnp.zeros((1, LANES), F32).at[0, :N_EXPERTS].set(b_r)
    out = pl.pallas_call(
        _router_kernel,
        grid=(M // tm,),
        in_specs=[pl.BlockSpec((tm, D), lambda i: (i, 0)),
                  pl.BlockSpec((D, LANES), lambda i: (0, 0)),
                  pl.BlockSpec((1, LANES), lambda i: (0, 0))],
        out_specs=pl.BlockSpec((tm, LANES), lambda i: (i, 0)),
        out_shape=jax.ShapeDtypeStruct((M, LANES), F32),
        compiler_params=_cparams("parallel"),
        name="router_logits",
    )(h, w, b)
    return out[:, :N_EXPERTS]


MOE_W_CHUNKS = 4


def _expert_kernel(cb_ref, ce_ref, le_ref, lc_ref, fl_ref, x_ref, w1c_ref, w2c_ref, b1_ref, b2_ref,
                   o_ref, w1a_ref, w2a_ref, w1b_ref, w2b_ref):
    s = pl.program_id(0)
    fl = fl_ref[s]
    comp = (fl & 1) == 1
    load = (fl & 2) == 2
    par = (fl & 4) == 4
    c = lc_ref[s]
    C1 = w1c_ref.shape[0]
    C2 = w2c_ref.shape[0]

    def cast_chunk(w1_dst, w2_dst):
        w1_dst[pl.ds(pl.multiple_of(c * C1, C1), C1), :] = w1c_ref[...].astype(BF16)
        w2_dst[pl.ds(pl.multiple_of(c * C2, C2), C2), :] = w2c_ref[...].astype(BF16)

    def ffn(w1_src, w2_src):
        hgu = _dot(x_ref[...], w1_src[...]) + b1_ref[0]
        gate = jnp.minimum(hgu[:, :D_EXPERT], SWIGLU_LIMIT)
        up = jnp.clip(hgu[:, D_EXPERT:], -SWIGLU_LIMIT, SWIGLU_LIMIT)
        act = gate * jax.nn.sigmoid(SWIGLU_ALPHA * gate) * (up + 1.0)
        o_ref[...] = (_dot(act.astype(BF16), w2_src[...]) + b2_ref[0]).astype(o_ref.dtype)

    bufs = ((w1a_ref, w2a_ref), (w1b_ref, w2b_ref))
    for p in (0, 1):
        use, fill = bufs[p], bufs[1 - p]
        in_phase = par == (p == 1)

        @pl.when(comp & in_phase)
        def _():
            cast_chunk(*fill)
            ffn(*use)

        @pl.when(load & jnp.logical_not(comp) & in_phase)
        def _():
            cast_chunk(*fill)


def _expert_schedule(nblk, n_blocks):
    E, NC = N_EXPERTS, MOE_W_CHUNKS
    ph_ids = jnp.arange(E + 1, dtype=jnp.int32)
    prev_n = jnp.concatenate([jnp.zeros((1,), jnp.int32), nblk.astype(jnp.int32)])
    plen = jnp.where(ph_ids == 0, NC, jnp.where(ph_ids == E, prev_n, jnp.maximum(prev_n, NC)))
    pend = jnp.cumsum(plen)
    pstart = pend - plen
    n_steps = n_blocks + E * NC
    s = jnp.arange(n_steps, dtype=jnp.int32)
    ph = jnp.minimum(jnp.sum(pend[None, :] <= s[:, None], axis=1), E).astype(jnp.int32)
    t = s - pstart[ph]
    comp = (ph >= 1) & (t < prev_n[ph])
    load = (ph < E) & (t < NC)
    comp_blk = jnp.maximum(jnp.cumsum(comp.astype(jnp.int32)) - 1, 0)
    comp_e = jnp.clip(ph - 1, 0, E - 1)
    load_e = jnp.minimum(ph, E - 1)
    load_c = jnp.where(ph >= E, NC - 1, jnp.minimum(t, NC - 1))
    flags = comp.astype(jnp.int32) + 2 * load.astype(jnp.int32) + 4 * ((ph - 1) % 2 == 1).astype(jnp.int32)
    return comp_blk.astype(jnp.int32), comp_e.astype(jnp.int32), load_e, load_c.astype(jnp.int32), flags


def expert_ffn(xbuf, nblk, w1, b1, w2, b2, layer):
    rows, D = xbuf.shape
    R = MOE_ROWS
    NC = MOE_W_CHUNKS
    n_blocks = rows // R
    H2 = 2 * D_EXPERT
    C1, C2 = D // NC, D_EXPERT // NC
    sched = _expert_schedule(nblk, n_blocks)
    n_steps = sched[0].shape[0]
    grid_spec = pltpu.PrefetchScalarGridSpec(
        num_scalar_prefetch=5,
        grid=(n_steps,),
        in_specs=[pl.BlockSpec((R, D), lambda s, cb, ce, le, lc, fl: (cb[s], 0)),
                  pl.BlockSpec((None, None, C1, H2), lambda s, cb, ce, le, lc, fl: (layer, le[s], lc[s], 0)),
                  pl.BlockSpec((None, None, C2, D), lambda s, cb, ce, le, lc, fl: (layer, le[s], lc[s], 0)),
                  pl.BlockSpec((1, 1, H2), lambda s, cb, ce, le, lc, fl: (ce[s], 0, 0)),
                  pl.BlockSpec((1, 1, D), lambda s, cb, ce, le, lc, fl: (ce[s], 0, 0))],
        out_specs=pl.BlockSpec((R, D), lambda s, cb, ce, le, lc, fl: (cb[s], 0)),
        scratch_shapes=[pltpu.VMEM((D, H2), BF16), pltpu.VMEM((D_EXPERT, D), BF16),
                        pltpu.VMEM((D, H2), BF16), pltpu.VMEM((D_EXPERT, D), BF16)],
    )
    return pl.pallas_call(
        _expert_kernel,
        grid_spec=grid_spec,
        out_shape=jax.ShapeDtypeStruct((rows, D), BF16),
        compiler_params=_cparams("arbitrary"),
        name="expert_ffn",
    )(*sched, xbuf, w1, w2, b1.reshape(N_EXPERTS, 1, -1), b2.reshape(N_EXPERTS, 1, -1))


def moe_layer(h32, h16, w_r, b_r, w1, b1, w2, b2, ln_g, ln_b, layer):
    T, D = h32.shape
    R = MOE_ROWS
    M = T * TOP_K
    logits = router_logits(h32, w_r, b_r)
    top_val, top_idx = lax.top_k(logits, TOP_K)
    gates = jax.nn.softmax(top_val, axis=-1)
    onehot = top_idx[:, :, None] == jnp.arange(N_EXPERTS, dtype=jnp.int32)[None, None, :]
    csum = jnp.cumsum(jnp.sum(onehot, axis=1, dtype=jnp.int32), axis=0)
    counts = csum[-1]
    padded = (counts + R - 1) // R * R
    pad_end = jnp.cumsum(padded)
    pad_start = pad_end - padded
    pos = jnp.sum(jnp.where(onehot, (csum - 1 + pad_start[None, :])[:, None, :], 0), axis=-1)
    n_blocks = (M + N_EXPERTS * (R - 1) + R - 1) // R
    rows = n_blocks * R
    src_tok = (jnp.arange(rows, dtype=jnp.int32) % T).at[pos.reshape(M)].set(
        jnp.arange(M, dtype=jnp.int32) // TOP_K)
    xbuf = h16[src_tok]
    ybuf = expert_ffn(xbuf, padded // R, w1, b1, w2, b2, layer)
    y4 = ybuf[pos.T]
    return combine_residual_ln(y4, gates, h32, ln_g, ln_b)


MOBA_GROUP = 2
MOBA_GATE_ROWS = 8


def _col_groups(x, op):
    rows, cols = x.shape
    return op(x.reshape(rows // 8, 8, cols), axis=0)


def _moba_kernel(q_ref, k_ref, v_ref, o_ref, kmean_ref, vt_ref, bias_ref):
    pg = pl.program_id(1)
    cur = pl.program_id(2)
    BLK = MOBA_BLOCK
    GP = MOBA_GROUP
    NR = MOBA_GATE_ROWS
    nb = k_ref.shape[0] // BLK
    assert nb <= NR
    lane = lax.broadcasted_iota(jnp.int32, (1, LANES), 1)
    LOG2E = math.log2(math.e)
    c2 = MOBA_HEAD_DIM ** -0.5 * LOG2E

    @pl.when(cur == 0)
    def _():
        kmean_ref[...] = jnp.zeros_like(kmean_ref)
        for g in range(GP):
            for n in range(nb):
                rows = slice(n * BLK, (n + 1) * BLK)
                kmean_ref[g, n:n + 1, :] = jnp.mean(k_ref[rows, g * LANES:(g + 1) * LANES], axis=0,
                                                    keepdims=True)
                vt_ref[g, :, rows] = v_ref[rows, g * LANES:(g + 1) * LANES].T.astype(BF16)

    key = lax.broadcasted_iota(jnp.int32, (BLK, BLK), 0)
    qry = lax.broadcasted_iota(jnp.int32, (BLK, BLK), 1)
    krow = lax.broadcasted_iota(jnp.int32, (BLK, LANES), 0)
    klane = lax.broadcasted_iota(jnp.int32, (BLK, LANES), 1)
    k_extra = jnp.where(klane < 2, krow, 0).astype(F32).astype(BF16)
    srow = lax.broadcasted_iota(jnp.int32, (LANES, BLK), 0)
    blk_row = lax.broadcasted_iota(jnp.int32, (NR, BLK), 0)

    qts = []
    for g in range(GP):
        q = q_ref[:, g * LANES:(g + 1) * LANES]
        for hh in range(2):
            j = 2 * g + hh
            own = (lane // MOBA_HEAD_DIM) == hh
            qm_t = jnp.where(own, q, 0.0).T
            head_idx = 2 * (pg * GP + g) + hh + 1
            sl2 = jnp.exp(jnp.full((1, 1), head_idx, jnp.int32).astype(F32)
                          * (-8.0 * math.log(2.0) / MOBA_HEADS)) * LOG2E
            s_hi = sl2.astype(BF16).astype(F32)
            s_lo = (sl2 - s_hi).astype(BF16).astype(F32)
            q_extra = jnp.where(srow == 0, s_hi, jnp.where(srow == 1, s_lo, 0.0))
            qts.append(jnp.concatenate([qm_t * c2, q_extra], axis=0).astype(BF16))
            gate = _dot(kmean_ref[g], qm_t, precision=HIGHEST)
            rank = jnp.zeros((NR, BLK), jnp.int32)
            for m in range(nb):
                gm = gate[m:m + 1, :]
                ahead = (gm > gate) | ((gm == gate) & (m < blk_row))
                rank = rank + jnp.where(ahead, 1, 0) * (m < cur).astype(jnp.int32)
            sel = (rank < MOBA_TOPK) & (blk_row < cur)
            bias_ref[j] = jnp.where(sel, -sl2 * ((cur - blk_row) * BLK).astype(F32), NEG_INF)

    def scores(j, off):
        g = j // 2
        kc = jnp.concatenate([k_ref[pl.ds(off, BLK), g * LANES:(g + 1) * LANES].astype(BF16), k_extra],
                             axis=1)
        return _dot(kc, qts[j])

    def softmax_step(t, m_run, l_run, bias):
        if bias is None:
            t = jnp.where(key <= qry, t, NEG_INF)
            m_new = jnp.maximum(m_run, jnp.max(_col_groups(t, jnp.max), axis=0, keepdims=True))
            shift = m_new
        else:
            m_new = jnp.maximum(m_run, jnp.max(_col_groups(t, jnp.max), axis=0, keepdims=True) + bias)
            shift = m_new - bias
        alpha = jnp.exp2(m_run - m_new)
        pr = jnp.exp2(t - shift)
        l_new = alpha * l_run + _col_groups(pr, jnp.sum)
        return m_new, l_new, alpha, pr.astype(BF16)

    def weighted_values(j, off, acc, alpha, pr):
        return alpha * acc + _dot(vt_ref[j // 2, :, pl.ds(off, BLK)], pr)

    NJ = 2 * GP
    off_d = pl.multiple_of(cur * BLK, BLK)
    carry = []
    for j in range(NJ):
        m0 = jnp.full((1, BLK), NEG_INF, F32)
        m_new, l_new, alpha, pr = softmax_step(scores(j, off_d), m0, jnp.zeros((8, BLK), F32), None)
        carry.append((scores(j, 0), m_new, l_new, jnp.zeros((LANES, BLK), F32), alpha, pr))

    def body(n, carry):
        off_next = pl.multiple_of((n + 1) * BLK, BLK)
        off_prev = pl.multiple_of(jnp.where(n == 0, cur, n - 1) * BLK, BLK)
        out = []
        for j in range(NJ):
            t, m_run, l_run, acc, alpha_p, pr_p = carry[j]
            t_next = scores(j, off_next)
            m_new, l_new, alpha, pr = softmax_step(t, m_run, l_run, bias_ref[j, pl.ds(n, 1), :])
            acc_new = weighted_values(j, off_prev, acc, alpha_p, pr_p)
            out.append((t_next, m_new, l_new, acc_new, alpha, pr))
        return tuple(out)

    carry = lax.fori_loop(0, cur, body, tuple(carry))
    off_last = pl.multiple_of(jnp.where(cur == 0, cur, cur - 1) * BLK, BLK)
    row128 = lax.broadcasted_iota(jnp.int32, (LANES, 1), 0)
    for g in range(GP):
        outs = []
        for hh in range(2):
            j = 2 * g + hh
            _, m_run, l_run, acc, alpha_p, pr_p = carry[j]
            acc = weighted_values(j, off_last, acc, alpha_p, pr_p)
            outs.append(acc / jnp.sum(l_run, axis=0, keepdims=True))
        out_t = jnp.where(row128 < MOBA_HEAD_DIM, outs[0], outs[1])
        o_ref[:, g * LANES:(g + 1) * LANES] = out_t.T.astype(o_ref.dtype)


def moba_attention(proj, B, S):
    T = B * S
    BLK = MOBA_BLOCK
    GP = MOBA_GROUP
    W = GP * LANES
    n_grp = MOBA_DIM // W
    nq = S // BLK
    return pl.pallas_call(
        _moba_kernel,
        grid=(B, n_grp, nq),
        in_specs=[pl.BlockSpec((BLK, W), lambda b, p, c: (b * nq + c, p)),
                  pl.BlockSpec((S, W), lambda b, p, c: (b, n_grp + p)),
                  pl.BlockSpec((S, W), lambda b, p, c: (b, 2 * n_grp + p))],
        out_specs=pl.BlockSpec((BLK, W), lambda b, p, c: (b * nq + c, p)),
        out_shape=jax.ShapeDtypeStruct((T, MOBA_DIM), BF16),
        scratch_shapes=[pltpu.VMEM((GP, MOBA_GATE_ROWS, LANES), F32),
                        pltpu.VMEM((GP, LANES, S), BF16),
                        pltpu.VMEM((2 * GP, MOBA_GATE_ROWS, BLK), F32)],
        compiler_params=_cparams("parallel", "parallel", "arbitrary"),
        name="moba_attention",
    )(proj, proj, proj)


def _rwkv_prep_kernel(has_vres, S, *refs):
    if has_vres:
        (pm_ref, pl_ref, pm_prev_ref, pl_prev_ref, mu_m_ref, mu_l_ref, w0_ref, w2_ref, a0_ref, a2_ref,
         g2_ref, kk_ref, ka_ref, vfirst_ref, v0_ref, v1_ref, v2_ref,
         r_o, lw_o, k_o, v_o, kkn_o, a_o, g_o) = refs
    else:
        (pm_ref, pl_ref, pm_prev_ref, pl_prev_ref, mu_m_ref, mu_l_ref, w0_ref, w2_ref, a0_ref, a2_ref,
         g2_ref, kk_ref, ka_ref,
         r_o, lw_o, k_o, v_o, kkn_o, a_o, g_o) = refs
    i = pl.program_id(0)
    tm = pm_ref.shape[0]
    C = RWKV_DIM
    row = lax.broadcasted_iota(jnp.int32, (tm, 1), 0)
    seq_start = (i * tm) % S == 0

    def shifted(cur_ref, prev_ref, mu_ref):
        x = cur_ref[...]
        prev_row = jnp.where(seq_start, 0.0, prev_ref[7:8, :])
        xs = jnp.where(row == 0, prev_row, pltpu.roll(x, 1, 0))
        return x + (xs - x) * mu_ref[...]

    pm = shifted(pm_ref, pm_prev_ref, mu_m_ref)
    plo = shifted(pl_ref, pl_prev_ref, mu_l_ref)
    r = pm[:, :C]
    k = pm[:, C:2 * C]
    v = pm[:, 2 * C:]
    wd = plo[:, :RWKV_DECAY_LORA]
    ad = plo[:, RWKV_DECAY_LORA:RWKV_DECAY_LORA + RWKV_A_LORA]
    gd = plo[:, RWKV_DECAY_LORA + RWKV_A_LORA:RWKV_LORA]
    w = -jax.nn.softplus(-(w0_ref[...] + _dot(jnp.tanh(wd).astype(BF16), w2_ref[...]))) - 0.5
    a = jax.nn.sigmoid(a0_ref[...] + _dot(ad.astype(BF16), a2_ref[...]))
    g = _dot(jax.nn.sigmoid(gd).astype(BF16), g2_ref[...])
    if has_vres:
        lo = _dot(_dot(v.astype(BF16), v1_ref[...]).astype(BF16), v2_ref[...])
        v = v + (vfirst_ref[...] - v) * jax.nn.sigmoid(v0_ref[...] + lo)
    kk = k * kk_ref[...]
    hid_r = lax.broadcasted_iota(jnp.int32, (LANES, LANES), 0) // RWKV_HEAD_DIM
    hid_c = lax.broadcasted_iota(jnp.int32, (LANES, LANES), 1) // RWKV_HEAD_DIM
    ones_bd = (hid_r == hid_c).astype(F32)
    for pp in range(C // LANES):
        sl = slice(pp * LANES, (pp + 1) * LANES)
        kkp = kk[:, sl]
        ss = _dot(kkp * kkp, ones_bd, precision=HIGHEST)
        kkn_o[:, sl] = kkp * lax.rsqrt(jnp.maximum(ss, 1e-24))
    r_o[...] = r
    lw_o[...] = -jnp.exp(w)
    k_o[...] = k * (1.0 + (a - 1.0) * ka_ref[...])
    v_o[...] = v
    a_o[...] = a
    g_o[...] = g


def rwkv_prep(proj, plora, S, mu, w0, w2, a0, a2, g2, k_k, k_a, v_first, v_lora, tm=256):
    T = proj.shape[0]
    C = RWKV_DIM
    LP = plora.shape[1]
    has_vres = v_lora is not None
    mu_m = mu[:3 * C].reshape(1, 3 * C)
    mu_l = jnp.zeros((1, LP), F32).at[0, :RWKV_LORA].set(mu[3 * C:])
    row = lambda z: z.reshape(1, -1)
    full = lambda z: pl.BlockSpec(z.shape, lambda i: (0,) * z.ndim)
    args = [proj, plora, proj, plora, mu_m, mu_l, row(w0), w2.astype(BF16), row(a0), a2.astype(BF16),
            g2.astype(BF16), row(k_k), row(k_a)]
    in_specs = [pl.BlockSpec((tm, 3 * C), lambda i: (i, 1)),
                pl.BlockSpec((tm, LP), lambda i: (i, 0)),
                pl.BlockSpec((8, 3 * C), lambda i: (jnp.maximum(i * (tm // 8) - 1, 0), 1)),
                pl.BlockSpec((8, LP), lambda i: (jnp.maximum(i * (tm // 8) - 1, 0), 0))]
    in_specs += [full(z) for z in args[4:]]
    if has_vres:
        v0, v1, v2 = v_lora
        extra = [v_first, row(v0), v1.astype(BF16), v2.astype(BF16)]
        args += extra
        in_specs += [pl.BlockSpec((tm, C), lambda i: (i, 0))] + [full(z) for z in extra[1:]]
    out_spec = pl.BlockSpec((tm, C), lambda i: (i, 0))
    return pl.pallas_call(
        functools.partial(_rwkv_prep_kernel, has_vres, S),
        grid=(T // tm,),
        in_specs=in_specs,
        out_specs=[out_spec] * 7,
        out_shape=[jax.ShapeDtypeStruct((T, C), F32)] * 7,
        compiler_params=_cparams("parallel"),
        name="rwkv_prep",
    )(*args)


def _split3_bf16(x):
    x0 = x.astype(BF16)
    r1 = x - x0.astype(F32)
    x1 = r1.astype(BF16)
    x2 = (r1 - x1.astype(F32)).astype(BF16)
    return x0, x1, x2


def _sum01_left(m01, x):
    x0, x1, x2 = _split3_bf16(x)
    return _dot(m01, x0) + (_dot(m01, x1) + _dot(m01, x2))


def _sum01_right(x, m01):
    x0, x1, x2 = _split3_bf16(x)
    return _dot(x0, m01) + (_dot(x1, m01) + _dot(x2, m01))


_NN = (((1,), (0,)), ((), ()))
_NT = (((1,), (1,)), ((), ()))
_TN = (((0,), (0,)), ((), ()))


def _mm(a, b, passes, dims=_NN):
    dg = lambda x, y: lax.dot_general(x, y, dims, preferred_element_type=F32)
    if passes == 1:
        return dg(a.astype(BF16), b.astype(BF16))
    a_hi = a.astype(BF16)
    a_lo = (a - a_hi.astype(F32)).astype(BF16)
    b_hi = b.astype(BF16)
    b_lo = (b - b_hi.astype(F32)).astype(BF16)
    return dg(a_hi, b_hi) + (dg(a_hi, b_lo) + dg(a_lo, b_hi))


RWKV_PASSES = dict(gram=1, inv=1, apply=1, state=3)


def _rwkv_scan_kernel(r_ref, lw_ref, k_ref, v_ref, kk_ref, a_ref, g_ref, rk_ref, lnw_ref, lnb_ref,
                      o_ref, state_ref):
    c = pl.program_id(1)
    CH = RWKV_CHUNK
    N = RWKV_HEAD_DIM
    P2 = 2 * CH
    NP = RWKV_DIM // LANES
    pg, pi, pa, ps = (RWKV_PASSES[n] for n in ("gram", "inv", "apply", "state"))

    @pl.when(c == 0)
    def _():
        state_ref[...] = jnp.zeros_like(state_ref)

    lane = lax.broadcasted_iota(jnp.int32, (1, LANES), 1)
    head0 = lane < N
    ri = lax.broadcasted_iota(jnp.int32, (P2, P2), 0)
    ci = lax.broadcasted_iota(jnp.int32, (P2, P2), 1)
    same_head = (ri // CH) == (ci // CH)
    strict = (ri % CH) > (ci % CH)
    incl = (ri % CH) >= (ci % CH)
    eye = (ri == ci).astype(F32)
    ones_bd = same_head.astype(BF16)
    ti = lax.broadcasted_iota(jnp.int32, (CH, CH), 0)
    si = lax.broadcasted_iota(jnp.int32, (CH, CH), 1)
    tril_incl = (ti >= si).astype(BF16)

    def stack(x):
        return jnp.concatenate([jnp.where(head0, x, 0.0), jnp.where(head0, 0.0, x)], axis=0)

    pairs = range(NP)
    sls = [slice(p * LANES, (p + 1) * LANES) for p in pairs]
    r = [r_ref[:, sl] for sl in sls]
    k = [k_ref[:, sl] for sl in sls]
    v = [v_ref[:, sl] for sl in sls]
    kk = [kk_ref[:, sl] for sl in sls]
    lw = [lw_ref[:, sl] for sl in sls]
    cum = [_sum01_left(tril_incl, lw[p]) for p in pairs]
    cum_end = [cum[p][CH - 1:CH, :] for p in pairs]
    b = [kk[p] * a_ref[:, sls[p]] for p in pairs]
    e_neg = [jnp.exp(-cum[p]) for p in pairs]
    e_end = [jnp.exp(cum_end[p] - cum[p]) for p in pairs]
    A_st = [stack(-kk[p] * jnp.exp(cum[p] - lw[p])) for p in pairs]
    R_st = [stack(r[p] * jnp.exp(cum[p])) for p in pairs]
    BK = [jnp.concatenate([stack(b[p] * e_neg[p]), stack(k[p] * e_neg[p])], axis=0) for p in pairs]
    Bend_st = [stack(b[p] * e_end[p]) for p in pairs]
    Kend_st = [stack(k[p] * e_end[p]) for p in pairs]
    V_st = [stack(v[p]) for p in pairs]

    G = [_mm(jnp.concatenate([A_st[p], R_st[p]], axis=0), BK[p], pg, _NT) for p in pairs]
    Lab = [jnp.where(strict, G[p][:P2, :P2], 0.0) for p in pairs]
    Lak = [jnp.where(strict, G[p][:P2, P2:], 0.0) for p in pairs]
    Mrb = [jnp.where(incl, G[p][P2:, :P2], 0.0) for p in pairs]
    Mrk = [jnp.where(incl, G[p][P2:, P2:], 0.0) for p in pairs]

    Tinv = [eye + Lab[p] for p in pairs]
    Lp = Lab
    for _ in range(int(math.log2(CH)) - 1):
        Lp = [_mm(Lp[p], Lp[p], pi) for p in pairs]
        Tinv = [Tinv[p] + _mm(Tinv[p], Lp[p], pi) for p in pairs]

    LakV = [_mm(Lak[p], V_st[p], pa) for p in pairs]
    AU = [_mm(Tinv[p], jnp.concatenate([A_st[p], LakV[p]], axis=1), pa) for p in pairs]
    MM = [_mm(Mrb[p], AU[p], pa) for p in pairs]
    MV = [_mm(Mrk[p], V_st[p], pa) for p in pairs]
    BT = [_mm(Bend_st[p], AU[p], pa, _TN) for p in pairs]
    KV = [_mm(Kend_st[p], V_st[p], pa, _TN) for p in pairs]
    Rhat = [R_st[p] + MM[p][:, :LANES] for p in pairs]
    Mmat = [eye * jnp.exp(cum_end[p]) + BT[p][:, :LANES] for p in pairs]

    RS = [_mm(jnp.concatenate([Rhat[p], Mmat[p]], axis=0), state_ref[p], ps) for p in pairs]
    for p in pairs:
        state_ref[p] = jnp.where(same_head, RS[p][P2:] + BT[p][:, LANES:] + KV[p], 0.0)
    Y_st = [RS[p][:P2] + MM[p][:, LANES:] + MV[p] for p in pairs]
    y = [Y_st[p][:CH] + Y_st[p][CH:] for p in pairs]

    mean = [_sum01_right(y[p], ones_bd) * (1.0 / N) for p in pairs]
    yc = [y[p] - mean[p] for p in pairs]
    var = [_sum01_right(yc[p] * yc[p], ones_bd) * (1.0 / N) for p in pairs]
    bonus = [_sum01_right(r[p] * k[p] * rk_ref[:, sls[p]], ones_bd) for p in pairs]
    for p in pairs:
        yn = yc[p] * lax.rsqrt(var[p] + RWKV_GN_EPS) * lnw_ref[:, sls[p]] + lnb_ref[:, sls[p]]
        o_ref[:, sls[p]] = ((yn + bonus[p] * v[p]) * g_ref[:, sls[p]]).astype(o_ref.dtype)


def rwkv_scan(r, lw, k, v, kk, a, g, r_k, ln_w, ln_b, B, S):
    T, C = r.shape
    CH = RWKV_CHUNK
    nc = S // CH
    blk = pl.BlockSpec((CH, C), lambda b, c: (b * nc + c, 0))
    par = pl.BlockSpec((1, C), lambda b, c: (0, 0))
    return pl.pallas_call(
        _rwkv_scan_kernel,
        grid=(B, nc),
        in_specs=[blk] * 7 + [par] * 3,
        out_specs=blk,
        out_shape=jax.ShapeDtypeStruct((T, C), BF16),
        scratch_shapes=[pltpu.VMEM((C // LANES, LANES, LANES), F32)],
        compiler_params=_cparams("parallel", "arbitrary"),
        name="rwkv_scan",
    )(r, lw, k, v, kk, a, g, r_k.reshape(1, C), ln_w.reshape(1, C), ln_b.reshape(1, C))


def _rope_pairs(x, cos2, sin2):
    lane = lax.broadcasted_iota(jnp.int32, (1, LANES), 1)
    first_half = (lane % MLA_ROPE_DIM) < (MLA_ROPE_DIM // 2)
    partner = jnp.where(first_half, pltpu.roll(x, LANES - MLA_ROPE_DIM // 2, 1),
                        pltpu.roll(x, MLA_ROPE_DIM // 2, 1))
    return x * cos2 + partner * sin2


def _mla_cproj_kernel(h_ref, w_ref, cos_ref, sin_ref, c_ref, kr_ref):
    acc = _dot(h_ref[...], w_ref[...])
    NC = MLA_Q_LORA + MLA_KV_LORA
    c_ref[...] = acc[:, :NC]
    kr_ref[...] = _rope_pairs(acc[:, NC:], cos_ref[...], sin_ref[...]).astype(kr_ref.dtype)


def mla_cproj(h16, w_in, cos2, sin2, S, tm=512):
    T, D = h16.shape
    NC = MLA_Q_LORA + MLA_KV_LORA
    w = jnp.concatenate([w_in, w_in[:, NC:]], axis=1).astype(BF16)
    ns = S // tm
    return pl.pallas_call(
        _mla_cproj_kernel,
        grid=(T // tm,),
        in_specs=[pl.BlockSpec((tm, D), lambda i: (i, 0)),
                  pl.BlockSpec((D, NC + LANES), lambda i: (0, 0)),
                  pl.BlockSpec((tm, LANES), lambda i: (i % ns, 0)),
                  pl.BlockSpec((tm, LANES), lambda i: (i % ns, 0))],
        out_specs=[pl.BlockSpec((tm, NC), lambda i: (i, 0)),
                   pl.BlockSpec((tm, LANES), lambda i: (i, 0))],
        out_shape=[jax.ShapeDtypeStruct((T, NC), F32), jax.ShapeDtypeStruct((T, LANES), BF16)],
        compiler_params=_cparams("parallel"),
        name="mla_cproj",
    )(h16, w, cos2, sin2)


def _rms_up_kernel(rope_tile, out_scale, c_ref, g_ref, w_ref, cos_ref, sin_ref, o_ref, cn_ref):
    j = pl.program_id(1)

    @pl.when(j == 0)
    def _():
        x = c_ref[...]
        ms = jnp.mean(x * x, axis=-1, keepdims=True)
        cn_ref[...] = (x * lax.rsqrt(ms + RMS_EPS) * g_ref[...]).astype(BF16)

    acc = _dot(cn_ref[...], w_ref[...])
    if out_scale != 1.0:
        acc = acc * out_scale

    if rope_tile is None:
        o_ref[...] = acc.astype(o_ref.dtype)
    else:
        @pl.when(j != rope_tile)
        def _():
            o_ref[...] = acc.astype(o_ref.dtype)

        @pl.when(j == rope_tile)
        def _():
            cos2, sin2 = cos_ref[...], sin_ref[...]
            for gidx in range(acc.shape[1] // LANES):
                sl = slice(gidx * LANES, (gidx + 1) * LANES)
                o_ref[:, sl] = _rope_pairs(acc[:, sl], cos2, sin2).astype(o_ref.dtype)


def rms_up_proj(c, col_block, gain, w, cos2, sin2, S, rope_tile, out_scale=1.0, tm=512, tn=1024):
    T = c.shape[0]
    K, N = w.shape
    ns = S // tm
    return pl.pallas_call(
        functools.partial(_rms_up_kernel, rope_tile, out_scale),
        grid=(T // tm, N // tn),
        in_specs=[pl.BlockSpec((tm, K), lambda i, j: (i, col_block)),
                  pl.BlockSpec((1, K), lambda i, j: (0, 0)),
                  pl.BlockSpec((K, tn), lambda i, j: (0, j)),
                  pl.BlockSpec((tm, LANES), lambda i, j: (i % ns, 0)),
                  pl.BlockSpec((tm, LANES), lambda i, j: (i % ns, 0))],
        out_specs=pl.BlockSpec((tm, tn), lambda i, j: (i, j)),
        out_shape=jax.ShapeDtypeStruct((T, N), BF16),
        scratch_shapes=[pltpu.VMEM((tm, K), BF16)],
        compiler_params=_cparams("parallel", "arbitrary"),
        name="rms_up_proj",
    )(c, gain.reshape(1, K), w, cos2, sin2)


MLA_GROUP = 4
MLA_SCORE_SCALE = (MLA_NOPE_DIM + MLA_ROPE_DIM) ** -0.5 * math.log2(math.e)


def _mla_attn_kernel(qn_ref, qr_ref, kv_ref, kr_ref, o_ref, vt_ref):
    qi = pl.program_id(2)
    TQ = qn_ref.shape[0]
    S = kv_ref.shape[0]
    G = MLA_GROUP
    lane = lax.broadcasted_iota(jnp.int32, (1, LANES), 1)

    @pl.when(qi == 0)
    def _():
        for j in range(G):
            for n in range(S // TQ):
                rows = slice(n * TQ, (n + 1) * TQ)
                v_t = kv_ref[rows, (2 * j + 1) * LANES:(2 * j + 2) * LANES]
                vt_ref[j, :, rows] = v_t.astype(F32).T.astype(BF16)

    key = lax.broadcasted_iota(jnp.int32, (TQ, TQ), 0)
    qry = lax.broadcasted_iota(jnp.int32, (TQ, TQ), 1)
    qts = []
    for j in range(G):
        own = (lane // MLA_ROPE_DIM) == (j % 2)
        qr = qr_ref[:, (j // 2) * LANES:(j // 2 + 1) * LANES]
        qc = jnp.concatenate([qn_ref[:, j * LANES:(j + 1) * LANES],
                              jnp.where(own, qr, jnp.zeros_like(qr))], axis=1)
        qts.append(qc.astype(F32).T.astype(BF16))

    def scores(j, off):
        kc = jnp.concatenate([kv_ref[pl.ds(off, TQ), 2 * j * LANES:(2 * j + 1) * LANES],
                              kr_ref[pl.ds(off, TQ), :]], axis=1)
        return _dot(kc, qts[j])

    def softmax_step(t, m_run, l_run, diag):
        if diag:
            t = jnp.where(key <= qry, t, NEG_INF)
        m_new = jnp.maximum(m_run, jnp.max(_col_groups(t, jnp.max), axis=0, keepdims=True))
        alpha = jnp.exp2(m_run - m_new)
        pr = jnp.exp2(t - m_new)
        l_new = alpha * l_run + _col_groups(pr, jnp.sum)
        return m_new, l_new, alpha, pr.astype(BF16)

    def weighted_values(j, off, acc, alpha, pr):
        return alpha * acc + _dot(vt_ref[j, :, pl.ds(off, TQ)], pr)

    def body(n, carry):
        off_next = pl.multiple_of((n + 1) * TQ, TQ)
        off_prev = pl.multiple_of(jnp.maximum(n - 1, 0) * TQ, TQ)
        out = []
        for j in range(G):
            t, m_run, l_run, acc, alpha_p, pr_p = carry[j]
            t_next = scores(j, off_next)
            m_new, l_new, alpha, pr = softmax_step(t, m_run, l_run, False)
            acc_new = weighted_values(j, off_prev, acc, alpha_p, pr_p)
            out.append((t_next, m_new, l_new, acc_new, alpha, pr))
        return tuple(out)

    carry = tuple((scores(j, 0), jnp.full((1, TQ), NEG_INF, F32), jnp.zeros((8, TQ), F32),
                   jnp.zeros((LANES, TQ), F32), jnp.ones((1, TQ), F32), jnp.zeros((TQ, TQ), BF16))
                  for j in range(G))
    carry = lax.fori_loop(0, qi, body, carry)
    off_d = pl.multiple_of(qi * TQ, TQ)
    off_p = pl.multiple_of(jnp.maximum(qi - 1, 0) * TQ, TQ)
    for j in range(G):
        t, m_run, l_run, acc, alpha_p, pr_p = carry[j]
        m_new, l_new, alpha, pr = softmax_step(t, m_run, l_run, True)
        acc = weighted_values(j, off_p, acc, alpha_p, pr_p)
        acc = weighted_values(j, off_d, acc, alpha, pr)
        out_t = acc / jnp.sum(l_new, axis=0, keepdims=True)
        o_ref[:, j * LANES:(j + 1) * LANES] = out_t.T.astype(o_ref.dtype)


def mla_attention(q, kv, kr2, B, S, tq=256):
    T = B * S
    H, G = MLA_HEADS, MLA_GROUP
    nq = S // tq
    n_nope = H // G
    return pl.pallas_call(
        _mla_attn_kernel,
        grid=(B, H // G, nq),
        in_specs=[pl.BlockSpec((tq, G * LANES), lambda b, g, i: (b * nq + i, g)),
                  pl.BlockSpec((tq, G // 2 * LANES), lambda b, g, i: (b * nq + i, 2 * n_nope + g)),
                  pl.BlockSpec((S, 2 * G * LANES), lambda b, g, i: (b, g)),
                  pl.BlockSpec((S, LANES), lambda b, g, i: (b, 0))],
        out_specs=pl.BlockSpec((tq, G * LANES), lambda b, g, i: (b * nq + i, g)),
        out_shape=jax.ShapeDtypeStruct((T, H * MLA_V_DIM), BF16),
        scratch_shapes=[pltpu.VMEM((G, LANES, S), BF16)],
        compiler_params=_cparams("parallel", "parallel", "arbitrary"),
        name="mla_attention",
    )(q, q, kv, kr2)


def _rope_tables(S):
    half = MLA_ROPE_DIM // 2
    inv = ROPE_THETA ** (-jnp.arange(0, MLA_ROPE_DIM, 2, dtype=F32) / MLA_ROPE_DIM)
    ang = jnp.arange(S, dtype=F32)[:, None] * inv[None, :]
    cos, sin = jnp.cos(ang), jnp.sin(ang)
    reps = LANES // MLA_ROPE_DIM
    cos2 = jnp.tile(jnp.concatenate([cos, cos], axis=1), (1, reps))
    sin2 = jnp.tile(jnp.concatenate([-sin, sin], axis=1), (1, reps))
    assert cos2.shape == (S, LANES) and half * 2 * reps == LANES
    return cos2, sin2


def kernel(x, ev_w_in, ev_w_out, rw_mu, rw_w0, rw_w2, rw_a0, rw_a2, rw_g2, rw_k_k, rw_k_a, rw_r_k,
           rw_ln_w, rw_ln_b, rw_v0, rw_v1, rw_v2, od_w_in, od_q_norm, od_kv_norm, od_w_uq, od_w_ukv,
           od_w_out, ln_mix_g, ln_mix_b, ln_ffn_g, ln_ffn_b, moe_w_r, moe_b_r, moe_w1, moe_b1,
           moe_w2, moe_b2):
    B, S, D = x.shape
    T = B * S
    h32 = x.reshape(T, D)
    h16 = h32.astype(BF16)
    cos2, sin2 = _rope_tables(S)
    n_main = 3 * MOBA_DIM + 3 * RWKV_DIM
    lora_pad = -(-RWKV_LORA // LANES) * LANES
    qd = MLA_NOPE_DIM + MLA_ROPE_DIM
    perm = np.concatenate([
        (np.arange(MLA_HEADS)[:, None] * qd + np.arange(MLA_NOPE_DIM)[None, :]).reshape(-1),
        (np.arange(MLA_HEADS)[:, None] * qd + MLA_NOPE_DIM + np.arange(MLA_ROPE_DIM)[None, :]).reshape(-1)])
    v_first = None
    for layer in range(DEPTH):
        j = layer // 2
        if layer % 2 == 0:
            w_in = ev_w_in[j]
            w_main = w_in[:, :n_main].astype(BF16)
            w_lora = jnp.zeros((D, lora_pad), BF16).at[:, :RWKV_LORA].set(w_in[:, n_main:].astype(BF16))
            proj = matmul(h16, w_main, F32)
            plora = matmul(h16, w_lora, F32)
            a_out = moba_attention(proj, B, S)
            v_lora = None if j == 0 else (rw_v0[j - 1], rw_v1[j - 1], rw_v2[j - 1])
            r, lw, k, v, kk, a, g = rwkv_prep(proj, plora, S, rw_mu[j], rw_w0[j], rw_w2[j], rw_a0[j],
                                             rw_a2[j], rw_g2[j], rw_k_k[j], rw_k_a[j], v_first, v_lora)
            if j == 0:
                v_first = v
            b_out = rwkv_scan(r, lw, k, v, kk, a, g, rw_r_k[j], rw_ln_w[j], rw_ln_b[j], B, S)
            w_out = ev_w_out[j].astype(BF16)
            h32, h16 = proj_residual_ln([a_out, b_out], [w_out[:MOBA_DIM], w_out[MOBA_DIM:]], h32,
                                        ln_mix_g[layer], ln_mix_b[layer])
        else:
            c, kr2 = mla_cproj(h16, od_w_in[j], cos2, sin2, S)
            q = rms_up_proj(c, 0, od_q_norm[j], od_w_uq[j][:, perm].astype(BF16), cos2, sin2, S,
                            rope_tile=2, out_scale=MLA_SCORE_SCALE)
            kv = rms_up_proj(c, 1, od_kv_norm[j], od_w_ukv[j].astype(BF16), cos2, sin2, S,
                             rope_tile=None)
            o = mla_attention(q, kv, kr2, B, S)
            h32, h16 = proj_residual_ln([o], [od_w_out[j].astype(BF16)], h32,
                                        ln_mix_g[layer], ln_mix_b[layer])
        h32, h16 = moe_layer(h32, h16, moe_w_r[layer], moe_b_r[layer], moe_w1, moe_b1[layer], moe_w2,
                             moe_b2[layer], ln_ffn_g[layer], ln_ffn_b[layer], layer)
    return h32.reshape(B, S, D)
```

```python
import functools
import math

import jax
import jax.numpy as jnp
import numpy as np
from jax import lax
from jax.experimental import pallas as pl
from jax.experimental.pallas import tpu as pltpu

F32 = jnp.float32
BF16 = jnp.bfloat16
HIGHEST = lax.Precision.HIGHEST

DEPTH = 4
MOBA_HEADS = 16
MOBA_HEAD_DIM = 64
MOBA_DIM = MOBA_HEADS * MOBA_HEAD_DIM
MOBA_BLOCK = 256
MOBA_TOPK = 3
RWKV_HEADS = 16
RWKV_HEAD_DIM = 64
RWKV_DIM = RWKV_HEADS * RWKV_HEAD_DIM
RWKV_DECAY_LORA = 64
RWKV_A_LORA = 64
RWKV_G_LORA = 160
RWKV_LORA = RWKV_DECAY_LORA + RWKV_A_LORA + RWKV_G_LORA
RWKV_GN_EPS = 64e-5
RWKV_CHUNK = 64
MLA_HEADS = 16
MLA_Q_LORA = 512
MLA_KV_LORA = 512
MLA_NOPE_DIM = 128
MLA_ROPE_DIM = 64
MLA_V_DIM = 128
ROPE_THETA = 10000.0
N_EXPERTS = 32
TOP_K = 4
D_EXPERT = 1024
SWIGLU_ALPHA = 1.702
SWIGLU_LIMIT = 7.0
MOE_ROWS = 256
DEEPNORM_ALPHA = (2 * DEPTH) ** 0.25
LN_EPS = 1e-5
RMS_EPS = 1e-6
NEG_INF = -1e30

LANES = 128
VMEM_LIMIT = 56 * 1024 * 1024


def _cparams(*sem):
    return pltpu.CompilerParams(dimension_semantics=sem, vmem_limit_bytes=VMEM_LIMIT)


def _dot(a, b, precision=None):
    return jnp.dot(a, b, preferred_element_type=F32, precision=precision)


def _dot_nt(a, b, precision=None):
    return lax.dot_general(a, b, (((1,), (1,)), ((), ())), preferred_element_type=F32,
                           precision=precision)


def _dot_tn(a, b, precision=None):
    return lax.dot_general(a, b, (((0,), (0,)), ((), ())), preferred_element_type=F32,
                           precision=precision)


def _mm_kernel(a_ref, w_ref, o_ref):
    o_ref[...] = _dot(a_ref[...], w_ref[...]).astype(o_ref.dtype)


def matmul(a, w, out_dtype, tm=1024, tn=512):
    M, K = a.shape
    N = w.shape[1]
    tm, tn = min(tm, M), min(tn, N)
    assert M % tm == 0 and N % tn == 0
    return pl.pallas_call(
        _mm_kernel,
        grid=(M // tm, N // tn),
        in_specs=[pl.BlockSpec((tm, K), lambda i, j: (i, 0)),
                  pl.BlockSpec((K, tn), lambda i, j: (0, j))],
        out_specs=pl.BlockSpec((tm, tn), lambda i, j: (i, j)),
        out_shape=jax.ShapeDtypeStruct((M, N), out_dtype),
        compiler_params=_cparams("parallel", "parallel"),
        name="matmul",
    )(a, w)


def _layer_norm_rows(z, g, b):
    mu = jnp.mean(z, axis=-1, keepdims=True)
    zc = z - mu
    var = jnp.mean(zc * zc, axis=-1, keepdims=True)
    return zc * lax.rsqrt(var + LN_EPS) * g + b


def _proj_ln_kernel(n_in, *refs):
    a_refs = refs[:n_in]
    w_refs = refs[n_in:2 * n_in]
    h_ref, g_ref, b_ref, o32_ref, o16_ref = refs[2 * n_in:]
    acc = DEEPNORM_ALPHA * h_ref[...]
    for a_ref, w_ref in zip(a_refs, w_refs):
        acc = acc + _dot(a_ref[...], w_ref[...])
    y = _layer_norm_rows(acc, g_ref[...], b_ref[...])
    o32_ref[...] = y
    o16_ref[...] = y.astype(BF16)


def proj_residual_ln(a_list, w_list, h, g, b, tm=256):
    M, D = h.shape
    n_in = len(a_list)
    in_specs = ([pl.BlockSpec((tm, a.shape[1]), lambda i: (i, 0)) for a in a_list]
                + [pl.BlockSpec(w.shape, lambda i: (0, 0)) for w in w_list]
                + [pl.BlockSpec((tm, D), lambda i: (i, 0)),
                   pl.BlockSpec((1, D), lambda i: (0, 0)),
                   pl.BlockSpec((1, D), lambda i: (0, 0))])
    return pl.pallas_call(
        functools.partial(_proj_ln_kernel, n_in),
        grid=(M // tm,),
        in_specs=in_specs,
        out_specs=[pl.BlockSpec((tm, D), lambda i: (i, 0)),
                   pl.BlockSpec((tm, D), lambda i: (i, 0))],
        out_shape=[jax.ShapeDtypeStruct((M, D), F32), jax.ShapeDtypeStruct((M, D), BF16)],
        compiler_params=_cparams("parallel"),
        name="proj_residual_ln",
    )(*a_list, *w_list, h, g.reshape(1, D), b.reshape(1, D))


def _combine_ln_kernel(y_ref, gate_ref, h_ref, g_ref, b_ref, o32_ref, o16_ref):
    acc = DEEPNORM_ALPHA * h_ref[...]
    gates = gate_ref[...]
    for k in range(TOP_K):
        acc = acc + y_ref[k].astype(F32) * gates[:, k:k + 1]
    y = _layer_norm_rows(acc, g_ref[...], b_ref[...])
    o32_ref[...] = y
    o16_ref[...] = y.astype(BF16)


def combine_residual_ln(y4, gates, h, g, b, tm=256):
    M, D = h.shape
    return pl.pallas_call(
        _combine_ln_kernel,
        grid=(M // tm,),
        in_specs=[pl.BlockSpec((TOP_K, tm, D), lambda i: (0, i, 0)),
                  pl.BlockSpec((tm, TOP_K), lambda i: (i, 0)),
                  pl.BlockSpec((tm, D), lambda i: (i, 0)),
                  pl.BlockSpec((1, D), lambda i: (0, 0)),
                  pl.BlockSpec((1, D), lambda i: (0, 0))],
        out_specs=[pl.BlockSpec((tm, D), lambda i: (i, 0)),
                   pl.BlockSpec((tm, D), lambda i: (i, 0))],
        out_shape=[jax.ShapeDtypeStruct((M, D), F32), jax.ShapeDtypeStruct((M, D), BF16)],
        compiler_params=_cparams("parallel"),
        name="combine_residual_ln",
    )(y4, gates, h, g.reshape(1, D), b.reshape(1, D))


def _router_kernel(h_ref, w_ref, b_ref, o_ref):
    o_ref[...] = _dot(h_ref[...], w_ref[...], precision=HIGHEST) + b_ref[...]


def router_logits(h, w_r, b_r, tm=512):
    M, D = h.shape
    w = jnp.zeros((D, LANES), F32).at[:, :N_EXPERTS].set(w_r)
    b = jnp.zeros((1, LANES), F32).at[0, :N_EXPERTS].set(b_r)
    out = pl.pallas_call(
        _router_kernel,
        grid=(M // tm,),
        in_specs=[pl.BlockSpec((tm, D), lambda i: (i, 0)),
                  pl.BlockSpec((D, LANES), lambda i: (0, 0)),
                  pl.BlockSpec((1, LANES), lambda i: (0, 0))],
        out_specs=pl.BlockSpec((tm, LANES), lambda i: (i, 0)),
        out_shape=jax.ShapeDtypeStruct((M, LANES), F32),
        compiler_params=_cparams("parallel"),
        name="router_logits",
    )(h, w, b)
    return out[:, :N_EXPERTS]


MOE_W_CHUNKS = 4


def _expert_kernel(cb_ref, ce_ref, le_ref, lc_ref, fl_ref, x_ref, w1c_ref, w2c_ref, b1_ref, b2_ref,
                   o_ref, w1a_ref, w2a_ref, w1b_ref, w2b_ref):
    s = pl.program_id(0)
    fl = fl_ref[s]
    comp = (fl & 1) == 1
    load = (fl & 2) == 2
    par = (fl & 4) == 4
    c = lc_ref[s]
    C1 = w1c_ref.shape[0]
    C2 = w2c_ref.shape[0]

    def cast_chunk(w1_dst, w2_dst):
        w1_dst[pl.ds(pl.multiple_of(c * C1, C1), C1), :] = w1c_ref[...].astype(BF16)
        w2_dst[pl.ds(pl.multiple_of(c * C2, C2), C2), :] = w2c_ref[...].astype(BF16)

    def ffn(w1_src, w2_src):
        hgu = _dot(x_ref[...], w1_src[...]) + b1_ref[0]
        gate = jnp.minimum(hgu[:, :D_EXPERT], SWIGLU_LIMIT)
        up = jnp.clip(hgu[:, D_EXPERT:], -SWIGLU_LIMIT, SWIGLU_LIMIT)
        act = gate * jax.nn.sigmoid(SWIGLU_ALPHA * gate) * (up + 1.0)
        o_ref[...] = (_dot(act.astype(BF16), w2_src[...]) + b2_ref[0]).astype(o_ref.dtype)

    bufs = ((w1a_ref, w2a_ref), (w1b_ref, w2b_ref))
    for p in (0, 1):
        use, fill = bufs[p], bufs[1 - p]
        in_phase = par == (p == 1)

        @pl.when(comp & in_phase)
        def _():
            cast_chunk(*fill)
            ffn(*use)

        @pl.when(load & jnp.logical_not(comp) & in_phase)
        def _():
            cast_chunk(*fill)

    @pl.when((fl & 8) == 8)
    def _():
        o_ref[...] = jnp.zeros_like(o_ref)


def _expert_schedule(nblk, n_blocks):
    E, NC = N_EXPERTS, MOE_W_CHUNKS
    ph_ids = jnp.arange(E + 1, dtype=jnp.int32)
    prev_n = jnp.concatenate([jnp.zeros((1,), jnp.int32), nblk.astype(jnp.int32)])
    plen = jnp.where(ph_ids == 0, NC, jnp.where(ph_ids == E, prev_n, jnp.maximum(prev_n, NC)))
    pend = jnp.cumsum(plen)
    pstart = pend - plen
    n_steps = n_blocks + E * NC
    s = jnp.arange(n_steps, dtype=jnp.int32)
    ph = jnp.minimum(jnp.sum(pend[None, :] <= s[:, None], axis=1), E).astype(jnp.int32)
    t = s - pstart[ph]
    comp = (ph >= 1) & (t < prev_n[ph])
    load = (ph < E) & (t < NC)
    n_used = jnp.sum(nblk).astype(jnp.int32)
    tail_blk = n_used + (s - pend[E])
    fill = (s >= pend[E]) & (tail_blk < n_blocks)
    comp_blk = jnp.where(s >= pend[E], jnp.minimum(tail_blk, n_blocks - 1),
                         jnp.maximum(jnp.cumsum(comp.astype(jnp.int32)) - 1, 0))
    comp_e = jnp.clip(ph - 1, 0, E - 1)
    load_e = jnp.minimum(ph, E - 1)
    load_c = jnp.where(ph >= E, NC - 1, jnp.minimum(t, NC - 1))
    flags = (comp.astype(jnp.int32) + 2 * load.astype(jnp.int32)
             + 4 * ((ph - 1) % 2 == 1).astype(jnp.int32) + 8 * fill.astype(jnp.int32))
    return comp_blk.astype(jnp.int32), comp_e.astype(jnp.int32), load_e, load_c.astype(jnp.int32), flags


def expert_ffn(xbuf, nblk, w1, b1, w2, b2, layer):
    rows, D = xbuf.shape
    R = MOE_ROWS
    NC = MOE_W_CHUNKS
    n_blocks = rows // R
    H2 = 2 * D_EXPERT
    C1, C2 = D // NC, D_EXPERT // NC
    sched = _expert_schedule(nblk, n_blocks)
    n_steps = sched[0].shape[0]
    grid_spec = pltpu.PrefetchScalarGridSpec(
        num_scalar_prefetch=5,
        grid=(n_steps,),
        in_specs=[pl.BlockSpec((R, D), lambda s, cb, ce, le, lc, fl: (cb[s], 0)),
                  pl.BlockSpec((None, None, C1, H2), lambda s, cb, ce, le, lc, fl: (layer, le[s], lc[s], 0)),
                  pl.BlockSpec((None, None, C2, D), lambda s, cb, ce, le, lc, fl: (layer, le[s], lc[s], 0)),
                  pl.BlockSpec((1, 1, H2), lambda s, cb, ce, le, lc, fl: (ce[s], 0, 0)),
                  pl.BlockSpec((1, 1, D), lambda s, cb, ce, le, lc, fl: (ce[s], 0, 0))],
        out_specs=pl.BlockSpec((R, D), lambda s, cb, ce, le, lc, fl: (cb[s], 0)),
        scratch_shapes=[pltpu.VMEM((D, H2), BF16), pltpu.VMEM((D_EXPERT, D), BF16),
                        pltpu.VMEM((D, H2), BF16), pltpu.VMEM((D_EXPERT, D), BF16)],
    )
    return pl.pallas_call(
        _expert_kernel,
        grid_spec=grid_spec,
        out_shape=jax.ShapeDtypeStruct((rows, D), BF16),
        compiler_params=_cparams("arbitrary"),
        name="expert_ffn",
    )(*sched, xbuf, w1, w2, b1.reshape(N_EXPERTS, 1, -1), b2.reshape(N_EXPERTS, 1, -1))


def moe_layer(h32, h16, w_r, b_r, w1, b1, w2, b2, ln_g, ln_b, layer):
    T, D = h32.shape
    R = MOE_ROWS
    M = T * TOP_K
    logits = router_logits(h32, w_r, b_r)
    top_val, top_idx = lax.top_k(logits, TOP_K)
    gates = jax.nn.softmax(top_val, axis=-1)
    onehot = top_idx[:, :, None] == jnp.arange(N_EXPERTS, dtype=jnp.int32)[None, None, :]
    csum = jnp.cumsum(jnp.sum(onehot, axis=1, dtype=jnp.int32), axis=0)
    counts = csum[-1]
    padded = (counts + R - 1) // R * R
    pad_end = jnp.cumsum(padded)
    pad_start = pad_end - padded
    pos = jnp.sum(jnp.where(onehot, (csum - 1 + pad_start[None, :])[:, None, :], 0), axis=-1)
    n_blocks = (M + N_EXPERTS * (R - 1) + R - 1) // R
    rows = n_blocks * R
    src_tok = (jnp.arange(rows, dtype=jnp.int32) % T).at[pos.reshape(M)].set(
        jnp.arange(M, dtype=jnp.int32) // TOP_K, unique_indices=True)
    xbuf = h16[src_tok]
    ybuf = expert_ffn(xbuf, padded // R, w1, b1, w2, b2, layer)
    y4 = ybuf[pos.T]
    return combine_residual_ln(y4, gates, h32, ln_g, ln_b)


MOBA_GROUP = 4
MOBA_GATE_ROWS = 8


def _col_groups(x, op):
    rows, cols = x.shape
    return op(x.reshape(rows // 8, 8, cols), axis=0)


def _moba_kernel(q_ref, k_ref, v_ref, o_ref, kmean_ref, vt_ref, bias_ref):
    pg = pl.program_id(1)
    cur = pl.program_id(2)
    BLK = MOBA_BLOCK
    GP = MOBA_GROUP
    NR = MOBA_GATE_ROWS
    nb = k_ref.shape[0] // BLK
    assert nb <= NR
    lane = lax.broadcasted_iota(jnp.int32, (1, LANES), 1)
    LOG2E = math.log2(math.e)
    c2 = MOBA_HEAD_DIM ** -0.5 * LOG2E

    @pl.when(cur == 0)
    def _():
        kmean_ref[...] = jnp.zeros_like(kmean_ref)
        for g in range(GP):
            for n in range(nb):
                rows = slice(n * BLK, (n + 1) * BLK)
                kmean_ref[g, n:n + 1, :] = jnp.mean(k_ref[rows, g * LANES:(g + 1) * LANES], axis=0,
                                                    keepdims=True)
                vt_ref[g, :, rows] = v_ref[rows, g * LANES:(g + 1) * LANES].T.astype(BF16)

    Q2 = 2 * BLK
    key = lax.broadcasted_iota(jnp.int32, (BLK, Q2), 0)
    qry = lax.broadcasted_iota(jnp.int32, (BLK, Q2), 1) % BLK
    krow = lax.broadcasted_iota(jnp.int32, (BLK, LANES), 0)
    klane = lax.broadcasted_iota(jnp.int32, (BLK, LANES), 1)
    k_extra = jnp.where(klane < 2, krow, 0).astype(F32).astype(BF16)
    srow = lax.broadcasted_iota(jnp.int32, (LANES, BLK), 0)
    blk_row = lax.broadcasted_iota(jnp.int32, (NR, BLK), 0)

    qts = []
    for g in range(GP):
        q = q_ref[:, g * LANES:(g + 1) * LANES]
        halves, biases = [], []
        for hh in range(2):
            own = (lane // MOBA_HEAD_DIM) == hh
            qm_t = jnp.where(own, q, 0.0).T
            head_idx = 2 * (pg * GP + g) + hh + 1
            sl2 = jnp.exp(jnp.full((1, 1), head_idx, jnp.int32).astype(F32)
                          * (-8.0 * math.log(2.0) / MOBA_HEADS)) * LOG2E
            s_hi = sl2.astype(BF16).astype(F32)
            s_lo = (sl2 - s_hi).astype(BF16).astype(F32)
            q_extra = jnp.where(srow == 0, s_hi, jnp.where(srow == 1, s_lo, 0.0))
            halves.append(jnp.concatenate([qm_t * c2, q_extra], axis=0).astype(BF16))
            gate = _dot(kmean_ref[g], qm_t, precision=HIGHEST)
            rank = jnp.zeros((NR, BLK), jnp.int32)
            for m in range(nb):
                gm = gate[m:m + 1, :]
                ahead = (gm > gate) | ((gm == gate) & (m < blk_row))
                rank = rank + jnp.where(ahead, 1, 0) * (m < cur).astype(jnp.int32)
            sel = (rank < MOBA_TOPK) & (blk_row < cur)
            biases.append(jnp.where(sel, -sl2 * ((cur - blk_row) * BLK).astype(F32), NEG_INF))
        qts.append(jnp.concatenate(halves, axis=1))
        bias_ref[g] = jnp.concatenate(biases, axis=1)

    def scores(g, off):
        kc = jnp.concatenate([k_ref[pl.ds(off, BLK), g * LANES:(g + 1) * LANES].astype(BF16), k_extra],
                             axis=1)
        return _dot(kc, qts[g])

    def softmax_step(t, m_run, l_run, bias):
        if bias is None:
            t = jnp.where(key <= qry, t, NEG_INF)
            m_new = jnp.maximum(m_run, jnp.max(_col_groups(t, jnp.max), axis=0, keepdims=True))
            shift = m_new
        else:
            m_new = jnp.maximum(m_run, jnp.max(_col_groups(t, jnp.max), axis=0, keepdims=True) + bias)
            shift = m_new - bias
        alpha = jnp.exp2(m_run - m_new)
        pr = jnp.exp2(t - shift)
        l_new = alpha * l_run + _col_groups(pr, jnp.sum)
        return m_new, l_new, alpha, pr.astype(BF16)

    def block(g, n, carry, bias):
        m_run, l_run, acc = carry
        off = pl.multiple_of(n * BLK, BLK)
        m_new, l_new, alpha, pr = softmax_step(scores(g, off), m_run, l_run, bias)
        return m_new, l_new, alpha * acc + _dot(vt_ref[g, :, pl.ds(off, BLK)], pr)

    init = (jnp.full((1, Q2), NEG_INF, F32), jnp.zeros((8, Q2), F32), jnp.zeros((LANES, Q2), F32))
    carry = tuple(block(g, cur, init, None) for g in range(GP))

    def body(n, carry):
        return tuple(block(g, n, carry[g], bias_ref[g, pl.ds(n, 1), :]) for g in range(GP))

    carry = lax.fori_loop(0, cur, body, carry)
    row128 = lax.broadcasted_iota(jnp.int32, (LANES, 1), 0)
    for g in range(GP):
        m_run, l_run, acc = carry[g]
        out = acc / jnp.sum(l_run, axis=0, keepdims=True)
        out_t = jnp.where(row128 < MOBA_HEAD_DIM, out[:, :BLK], out[:, BLK:])
        o_ref[:, g * LANES:(g + 1) * LANES] = out_t.T.astype(o_ref.dtype)


def moba_attention(proj, B, S):
    T = B * S
    BLK = MOBA_BLOCK
    GP = MOBA_GROUP
    W = GP * LANES
    n_grp = MOBA_DIM // W
    nq = S // BLK
    return pl.pallas_call(
        _moba_kernel,
        grid=(B, n_grp, nq),
        in_specs=[pl.BlockSpec((BLK, W), lambda b, p, c: (b * nq + c, p)),
                  pl.BlockSpec((S, W), lambda b, p, c: (b, n_grp + p)),
                  pl.BlockSpec((S, W), lambda b, p, c: (b, 2 * n_grp + p))],
        out_specs=pl.BlockSpec((BLK, W), lambda b, p, c: (b * nq + c, p)),
        out_shape=jax.ShapeDtypeStruct((T, MOBA_DIM), BF16),
        scratch_shapes=[pltpu.VMEM((GP, MOBA_GATE_ROWS, LANES), F32),
                        pltpu.VMEM((GP, LANES, S), BF16),
                        pltpu.VMEM((GP, MOBA_GATE_ROWS, 2 * BLK), F32)],
        compiler_params=_cparams("parallel", "parallel", "arbitrary"),
        name="moba_attention",
    )(proj, proj, proj)


def _rwkv_prep_kernel(has_vres, S, *refs):
    if has_vres:
        (pm_ref, pl_ref, pm_prev_ref, pl_prev_ref, mu_m_ref, mu_l_ref, w0_ref, w2_ref, a0_ref, a2_ref,
         g2_ref, kk_ref, ka_ref, vfirst_ref, v0_ref, v1_ref, v2_ref,
         r_o, lw_o, k_o, v_o, kkn_o, a_o, g_o) = refs
    else:
        (pm_ref, pl_ref, pm_prev_ref, pl_prev_ref, mu_m_ref, mu_l_ref, w0_ref, w2_ref, a0_ref, a2_ref,
         g2_ref, kk_ref, ka_ref,
         r_o, lw_o, k_o, v_o, kkn_o, a_o, g_o) = refs
    i = pl.program_id(0)
    tm = pm_ref.shape[0]
    C = RWKV_DIM
    row = lax.broadcasted_iota(jnp.int32, (tm, 1), 0)
    seq_start = (i * tm) % S == 0

    def shifted(cur_ref, prev_ref, mu_ref):
        x = cur_ref[...]
        prev_row = jnp.where(seq_start, 0.0, prev_ref[7:8, :])
        xs = jnp.where(row == 0, prev_row, pltpu.roll(x, 1, 0))
        return x + (xs - x) * mu_ref[...]

    pm = shifted(pm_ref, pm_prev_ref, mu_m_ref)
    plo = shifted(pl_ref, pl_prev_ref, mu_l_ref)
    r = pm[:, :C]
    k = pm[:, C:2 * C]
    v = pm[:, 2 * C:]
    wd = plo[:, :RWKV_DECAY_LORA]
    ad = plo[:, RWKV_DECAY_LORA:RWKV_DECAY_LORA + RWKV_A_LORA]
    gd = plo[:, RWKV_DECAY_LORA + RWKV_A_LORA:RWKV_LORA]
    w = -jax.nn.softplus(-(w0_ref[...] + _dot(jnp.tanh(wd).astype(BF16), w2_ref[...]))) - 0.5
    a = jax.nn.sigmoid(a0_ref[...] + _dot(ad.astype(BF16), a2_ref[...]))
    g = _dot(jax.nn.sigmoid(gd).astype(BF16), g2_ref[...])
    if has_vres:
        lo = _dot(_dot(v.astype(BF16), v1_ref[...]).astype(BF16), v2_ref[...])
        v = v + (vfirst_ref[...] - v) * jax.nn.sigmoid(v0_ref[...] + lo)
    kk = k * kk_ref[...]
    hid_r = lax.broadcasted_iota(jnp.int32, (LANES, LANES), 0) // RWKV_HEAD_DIM
    hid_c = lax.broadcasted_iota(jnp.int32, (LANES, LANES), 1) // RWKV_HEAD_DIM
    ones_bd = (hid_r == hid_c).astype(F32)
    for pp in range(C // LANES):
        sl = slice(pp * LANES, (pp + 1) * LANES)
        kkp = kk[:, sl]
        ss = _dot(kkp * kkp, ones_bd, precision=HIGHEST)
        kkn_o[:, sl] = kkp * lax.rsqrt(jnp.maximum(ss, 1e-24))
    r_o[...] = r
    lw_o[...] = -jnp.exp(w)
    k_o[...] = k * (1.0 + (a - 1.0) * ka_ref[...])
    v_o[...] = v
    a_o[...] = a
    g_o[...] = g


def rwkv_prep(proj, plora, S, mu, w0, w2, a0, a2, g2, k_k, k_a, v_first, v_lora, tm=256):
    T = proj.shape[0]
    C = RWKV_DIM
    LP = plora.shape[1]
    has_vres = v_lora is not None
    mu_m = mu[:3 * C].reshape(1, 3 * C)
    mu_l = jnp.zeros((1, LP), F32).at[0, :RWKV_LORA].set(mu[3 * C:])
    row = lambda z: z.reshape(1, -1)
    full = lambda z: pl.BlockSpec(z.shape, lambda i: (0,) * z.ndim)
    args = [proj, plora, proj, plora, mu_m, mu_l, row(w0), w2.astype(BF16), row(a0), a2.astype(BF16),
            g2.astype(BF16), row(k_k), row(k_a)]
    in_specs = [pl.BlockSpec((tm, 3 * C), lambda i: (i, 1)),
                pl.BlockSpec((tm, LP), lambda i: (i, 0)),
                pl.BlockSpec((8, 3 * C), lambda i: (jnp.maximum(i * (tm // 8) - 1, 0), 1)),
                pl.BlockSpec((8, LP), lambda i: (jnp.maximum(i * (tm // 8) - 1, 0), 0))]
    in_specs += [full(z) for z in args[4:]]
    if has_vres:
        v0, v1, v2 = v_lora
        extra = [v_first, row(v0), v1.astype(BF16), v2.astype(BF16)]
        args += extra
        in_specs += [pl.BlockSpec((tm, C), lambda i: (i, 0))] + [full(z) for z in extra[1:]]
    out_spec = pl.BlockSpec((tm, C), lambda i: (i, 0))
    return pl.pallas_call(
        functools.partial(_rwkv_prep_kernel, has_vres, S),
        grid=(T // tm,),
        in_specs=in_specs,
        out_specs=[out_spec] * 7,
        out_shape=[jax.ShapeDtypeStruct((T, C), F32)] * 7,
        compiler_params=_cparams("parallel"),
        name="rwkv_prep",
    )(*args)


def _split3_bf16(x):
    x0 = x.astype(BF16)
    r1 = x - x0.astype(F32)
    x1 = r1.astype(BF16)
    x2 = (r1 - x1.astype(F32)).astype(BF16)
    return x0, x1, x2


def _sum01_left(m01, x):
    x0, x1, x2 = _split3_bf16(x)
    return _dot(m01, x0) + (_dot(m01, x1) + _dot(m01, x2))


def _sum01_right(x, m01):
    x0, x1, x2 = _split3_bf16(x)
    return _dot(x0, m01) + (_dot(x1, m01) + _dot(x2, m01))


_NN = (((1,), (0,)), ((), ()))
_NT = (((1,), (1,)), ((), ()))
_TN = (((0,), (0,)), ((), ()))


def _mm(a, b, passes, dims=_NN):
    dg = lambda x, y: lax.dot_general(x, y, dims, preferred_element_type=F32)
    if passes == 1:
        return dg(a.astype(BF16), b.astype(BF16))
    a_hi = a.astype(BF16)
    a_lo = (a - a_hi.astype(F32)).astype(BF16)
    b_hi = b.astype(BF16)
    b_lo = (b - b_hi.astype(F32)).astype(BF16)
    return dg(a_hi, b_hi) + (dg(a_hi, b_lo) + dg(a_lo, b_hi))


RWKV_PASSES = dict(gram=1, inv=1, apply=1, state=3)


def _rwkv_scan_kernel(r_ref, lw_ref, k_ref, v_ref, kk_ref, a_ref, g_ref, rk_ref, lnw_ref, lnb_ref,
                      o_ref, state_ref):
    c = pl.program_id(1)
    CH = RWKV_CHUNK
    N = RWKV_HEAD_DIM
    P2 = 2 * CH
    NP = RWKV_DIM // LANES
    pg, pi, pa, ps = (RWKV_PASSES[n] for n in ("gram", "inv", "apply", "state"))

    @pl.when(c == 0)
    def _():
        state_ref[...] = jnp.zeros_like(state_ref)

    lane = lax.broadcasted_iota(jnp.int32, (1, LANES), 1)
    head0 = lane < N
    ri = lax.broadcasted_iota(jnp.int32, (P2, P2), 0)
    ci = lax.broadcasted_iota(jnp.int32, (P2, P2), 1)
    same_head = (ri // CH) == (ci // CH)
    strict = (ri % CH) > (ci % CH)
    incl = (ri % CH) >= (ci % CH)
    eye = (ri == ci).astype(F32)
    ones_bd = same_head.astype(BF16)
    ti = lax.broadcasted_iota(jnp.int32, (CH, CH), 0)
    si = lax.broadcasted_iota(jnp.int32, (CH, CH), 1)
    tril_incl = (ti >= si).astype(BF16)

    def stack(x):
        return jnp.concatenate([jnp.where(head0, x, 0.0), jnp.where(head0, 0.0, x)], axis=0)

    pairs = range(NP)
    sls = [slice(p * LANES, (p + 1) * LANES) for p in pairs]
    r = [r_ref[:, sl] for sl in sls]
    k = [k_ref[:, sl] for sl in sls]
    v = [v_ref[:, sl] for sl in sls]
    kk = [kk_ref[:, sl] for sl in sls]
    lw = [lw_ref[:, sl] for sl in sls]
    cum = [_sum01_left(tril_incl, lw[p]) for p in pairs]
    cum_end = [cum[p][CH - 1:CH, :] for p in pairs]
    b = [kk[p] * a_ref[:, sls[p]] for p in pairs]
    e_neg = [jnp.exp(-cum[p]) for p in pairs]
    e_end = [jnp.exp(cum_end[p] - cum[p]) for p in pairs]
    A_st = [stack(-kk[p] * jnp.exp(cum[p] - lw[p])) for p in pairs]
    R_st = [stack(r[p] * jnp.exp(cum[p])) for p in pairs]
    BK = [jnp.concatenate([stack(b[p] * e_neg[p]), stack(k[p] * e_neg[p])], axis=0) for p in pairs]
    Bend_st = [stack(b[p] * e_end[p]) for p in pairs]
    Kend_st = [stack(k[p] * e_end[p]) for p in pairs]
    V_st = [stack(v[p]) for p in pairs]

    G = [_mm(jnp.concatenate([A_st[p], R_st[p]], axis=0), BK[p], pg, _NT) for p in pairs]
    Lab = [jnp.where(strict, G[p][:P2, :P2], 0.0) for p in pairs]
    Lak = [jnp.where(strict, G[p][:P2, P2:], 0.0) for p in pairs]
    Mrb = [jnp.where(incl, G[p][P2:, :P2], 0.0) for p in pairs]
    Mrk = [jnp.where(incl, G[p][P2:, P2:], 0.0) for p in pairs]

    Tinv = [eye + Lab[p] for p in pairs]
    Lp = Lab
    for _ in range(int(math.log2(CH)) - 1):
        Lp = [_mm(Lp[p], Lp[p], pi) for p in pairs]
        Tinv = [Tinv[p] + _mm(Tinv[p], Lp[p], pi) for p in pairs]

    LakV = [_mm(Lak[p], V_st[p], pa) for p in pairs]
    AU = [_mm(Tinv[p], jnp.concatenate([A_st[p], LakV[p]], axis=1), pa) for p in pairs]
    MM = [_mm(Mrb[p], AU[p], pa) for p in pairs]
    MV = [_mm(Mrk[p], V_st[p], pa) for p in pairs]
    BT = [_mm(Bend_st[p], AU[p], pa, _TN) for p in pairs]
    KV = [_mm(Kend_st[p], V_st[p], pa, _TN) for p in pairs]
    Rhat = [R_st[p] + MM[p][:, :LANES] for p in pairs]
    Mmat = [eye * jnp.exp(cum_end[p]) + BT[p][:, :LANES] for p in pairs]

    RS = [_mm(jnp.concatenate([Rhat[p], Mmat[p]], axis=0), state_ref[p], ps) for p in pairs]
    for p in pairs:
        state_ref[p] = jnp.where(same_head, RS[p][P2:] + BT[p][:, LANES:] + KV[p], 0.0)
    Y_st = [RS[p][:P2] + MM[p][:, LANES:] + MV[p] for p in pairs]
    y = [Y_st[p][:CH] + Y_st[p][CH:] for p in pairs]

    sums = [_sum01_right(jnp.concatenate([y[p], r[p] * k[p] * rk_ref[:, sls[p]]], axis=0), ones_bd)
            for p in pairs]
    yc = [y[p] - sums[p][:CH] * (1.0 / N) for p in pairs]
    var = [_sum01_right(yc[p] * yc[p], ones_bd) * (1.0 / N) for p in pairs]
    for p in pairs:
        yn = yc[p] * lax.rsqrt(var[p] + RWKV_GN_EPS) * lnw_ref[:, sls[p]] + lnb_ref[:, sls[p]]
        o_ref[:, sls[p]] = ((yn + sums[p][CH:] * v[p]) * g_ref[:, sls[p]]).astype(o_ref.dtype)


def rwkv_scan(r, lw, k, v, kk, a, g, r_k, ln_w, ln_b, B, S):
    T, C = r.shape
    CH = RWKV_CHUNK
    nc = S // CH
    blk = pl.BlockSpec((CH, C), lambda b, c: (b * nc + c, 0))
    par = pl.BlockSpec((1, C), lambda b, c: (0, 0))
    return pl.pallas_call(
        _rwkv_scan_kernel,
        grid=(B, nc),
        in_specs=[blk] * 7 + [par] * 3,
        out_specs=blk,
        out_shape=jax.ShapeDtypeStruct((T, C), BF16),
        scratch_shapes=[pltpu.VMEM((C // LANES, LANES, LANES), F32)],
        compiler_params=_cparams("parallel", "arbitrary"),
        name="rwkv_scan",
    )(r, lw, k, v, kk, a, g, r_k.reshape(1, C), ln_w.reshape(1, C), ln_b.reshape(1, C))


def _rope_pairs(x, cos2, sin2):
    lane = lax.broadcasted_iota(jnp.int32, (1, LANES), 1)
    first_half = (lane % MLA_ROPE_DIM) < (MLA_ROPE_DIM // 2)
    partner = jnp.where(first_half, pltpu.roll(x, LANES - MLA_ROPE_DIM // 2, 1),
                        pltpu.roll(x, MLA_ROPE_DIM // 2, 1))
    return x * cos2 + partner * sin2


def _mla_cproj_kernel(h_ref, w_ref, cos_ref, sin_ref, c_ref, kr_ref):
    acc = _dot(h_ref[...], w_ref[...])
    NC = MLA_Q_LORA + MLA_KV_LORA
    c_ref[...] = acc[:, :NC]
    kr_ref[...] = _rope_pairs(acc[:, NC:], cos_ref[...], sin_ref[...]).astype(kr_ref.dtype)


def mla_cproj(h16, w_in, cos2, sin2, S, tm=512):
    T, D = h16.shape
    NC = MLA_Q_LORA + MLA_KV_LORA
    w = jnp.concatenate([w_in, w_in[:, NC:]], axis=1).astype(BF16)
    ns = S // tm
    return pl.pallas_call(
        _mla_cproj_kernel,
        grid=(T // tm,),
        in_specs=[pl.BlockSpec((tm, D), lambda i: (i, 0)),
                  pl.BlockSpec((D, NC + LANES), lambda i: (0, 0)),
                  pl.BlockSpec((tm, LANES), lambda i: (i % ns, 0)),
                  pl.BlockSpec((tm, LANES), lambda i: (i % ns, 0))],
        out_specs=[pl.BlockSpec((tm, NC), lambda i: (i, 0)),
                   pl.BlockSpec((tm, LANES), lambda i: (i, 0))],
        out_shape=[jax.ShapeDtypeStruct((T, NC), F32), jax.ShapeDtypeStruct((T, LANES), BF16)],
        compiler_params=_cparams("parallel"),
        name="mla_cproj",
    )(h16, w, cos2, sin2)


def _rms_up_kernel(rope_tile, out_scale, c_ref, g_ref, w_ref, cos_ref, sin_ref, o_ref, cn_ref):
    j = pl.program_id(1)

    @pl.when(j == 0)
    def _():
        x = c_ref[...]
        ms = jnp.mean(x * x, axis=-1, keepdims=True)
        cn_ref[...] = (x * lax.rsqrt(ms + RMS_EPS) * g_ref[...]).astype(BF16)

    acc = _dot(cn_ref[...], w_ref[...])
    if out_scale != 1.0:
        acc = acc * out_scale

    if rope_tile is None:
        o_ref[...] = acc.astype(o_ref.dtype)
    else:
        @pl.when(j != rope_tile)
        def _():
            o_ref[...] = acc.astype(o_ref.dtype)

        @pl.when(j == rope_tile)
        def _():
            cos2, sin2 = cos_ref[...], sin_ref[...]
            for gidx in range(acc.shape[1] // LANES):
                sl = slice(gidx * LANES, (gidx + 1) * LANES)
                o_ref[:, sl] = _rope_pairs(acc[:, sl], cos2, sin2).astype(o_ref.dtype)


def rms_up_proj(c, col_block, gain, w, cos2, sin2, S, rope_tile, out_scale=1.0, tm=512, tn=1024):
    T = c.shape[0]
    K, N = w.shape
    ns = S // tm
    return pl.pallas_call(
        functools.partial(_rms_up_kernel, rope_tile, out_scale),
        grid=(T // tm, N // tn),
        in_specs=[pl.BlockSpec((tm, K), lambda i, j: (i, col_block)),
                  pl.BlockSpec((1, K), lambda i, j: (0, 0)),
                  pl.BlockSpec((K, tn), lambda i, j: (0, j)),
                  pl.BlockSpec((tm, LANES), lambda i, j: (i % ns, 0)),
                  pl.BlockSpec((tm, LANES), lambda i, j: (i % ns, 0))],
        out_specs=pl.BlockSpec((tm, tn), lambda i, j: (i, j)),
        out_shape=jax.ShapeDtypeStruct((T, N), BF16),
        scratch_shapes=[pltpu.VMEM((tm, K), BF16)],
        compiler_params=_cparams("parallel", "arbitrary"),
        name="rms_up_proj",
    )(c, gain.reshape(1, K), w, cos2, sin2)


MLA_GROUP = 4
MLA_SCORE_SCALE = (MLA_NOPE_DIM + MLA_ROPE_DIM) ** -0.5 * math.log2(math.e)


def _mla_attn_kernel(qn_ref, qr_ref, kv_ref, kr_ref, o_ref, vt_ref):
    qi = pl.program_id(2)
    TQ = qn_ref.shape[0]
    S = kv_ref.shape[0]
    G = MLA_GROUP
    lane = lax.broadcasted_iota(jnp.int32, (1, LANES), 1)

    @pl.when(qi == 0)
    def _():
        for j in range(G):
            for n in range(S // TQ):
                rows = slice(n * TQ, (n + 1) * TQ)
                v_t = kv_ref[rows, (2 * j + 1) * LANES:(2 * j + 2) * LANES]
                vt_ref[j, :, rows] = v_t.astype(F32).T.astype(BF16)

    key = lax.broadcasted_iota(jnp.int32, (TQ, TQ), 0)
    qry = lax.broadcasted_iota(jnp.int32, (TQ, TQ), 1)
    qts = []
    for j in range(G):
        own = (lane // MLA_ROPE_DIM) == (j % 2)
        qr = qr_ref[:, (j // 2) * LANES:(j // 2 + 1) * LANES]
        qc = jnp.concatenate([qn_ref[:, j * LANES:(j + 1) * LANES],
                              jnp.where(own, qr, jnp.zeros_like(qr))], axis=1)
        qts.append(qc.astype(F32).T.astype(BF16))

    def scores(j, off):
        kc = jnp.concatenate([kv_ref[pl.ds(off, TQ), 2 * j * LANES:(2 * j + 1) * LANES],
                              kr_ref[pl.ds(off, TQ), :]], axis=1)
        return _dot(kc, qts[j])

    def softmax_step(t, m_run, l_run, diag):
        if diag:
            t = jnp.where(key <= qry, t, NEG_INF)
        m_new = jnp.maximum(m_run, jnp.max(_col_groups(t, jnp.max), axis=0, keepdims=True))
        alpha = jnp.exp2(m_run - m_new)
        pr = jnp.exp2(t - m_new)
        l_new = alpha * l_run + _col_groups(pr, jnp.sum)
        return m_new, l_new, alpha, pr.astype(BF16)

    def weighted_values(j, off, acc, alpha, pr):
        return alpha * acc + _dot(vt_ref[j, :, pl.ds(off, TQ)], pr)

    def body(n, carry):
        off = pl.multiple_of(n * TQ, TQ)
        out = []
        for j in range(G):
            m_run, l_run, acc = carry[j]
            m_new, l_new, alpha, pr = softmax_step(scores(j, off), m_run, l_run, False)
            out.append((m_new, l_new, weighted_values(j, off, acc, alpha, pr)))
        return tuple(out)

    carry = tuple((jnp.full((1, TQ), NEG_INF, F32), jnp.zeros((8, TQ), F32),
                   jnp.zeros((LANES, TQ), F32)) for j in range(G))
    carry = lax.fori_loop(0, qi, body, carry)
    off_d = pl.multiple_of(qi * TQ, TQ)
    for j in range(G):
        m_run, l_run, acc = carry[j]
        m_new, l_new, alpha, pr = softmax_step(scores(j, off_d), m_run, l_run, True)
        acc = weighted_values(j, off_d, acc, alpha, pr)
        out_t = acc / jnp.sum(l_new, axis=0, keepdims=True)
        o_ref[:, j * LANES:(j + 1) * LANES] = out_t.T.astype(o_ref.dtype)


def mla_attention(q, kv, kr2, B, S, tq=512):
    T = B * S
    H, G = MLA_HEADS, MLA_GROUP
    nq = S // tq
    n_nope = H // G
    return pl.pallas_call(
        _mla_attn_kernel,
        grid=(B, H // G, nq),
        in_specs=[pl.BlockSpec((tq, G * LANES), lambda b, g, i: (b * nq + i, g)),
                  pl.BlockSpec((tq, G // 2 * LANES), lambda b, g, i: (b * nq + i, 2 * n_nope + g)),
                  pl.BlockSpec((S, 2 * G * LANES), lambda b, g, i: (b, g)),
                  pl.BlockSpec((S, LANES), lambda b, g, i: (b, 0))],
        out_specs=pl.BlockSpec((tq, G * LANES), lambda b, g, i: (b * nq + i, g)),
        out_shape=jax.ShapeDtypeStruct((T, H * MLA_V_DIM), BF16),
        scratch_shapes=[pltpu.VMEM((G, LANES, S), BF16)],
        compiler_params=_cparams("parallel", "parallel", "arbitrary"),
        name="mla_attention",
    )(q, q, kv, kr2)


def _rope_tables(S):
    half = MLA_ROPE_DIM // 2
    inv = ROPE_THETA ** (-jnp.arange(0, MLA_ROPE_DIM, 2, dtype=F32) / MLA_ROPE_DIM)
    ang = jnp.arange(S, dtype=F32)[:, None] * inv[None, :]
    cos, sin = jnp.cos(ang), jnp.sin(ang)
    reps = LANES // MLA_ROPE_DIM
    cos2 = jnp.tile(jnp.concatenate([cos, cos], axis=1), (1, reps))
    sin2 = jnp.tile(jnp.concatenate([-sin, sin], axis=1), (1, reps))
    assert cos2.shape == (S, LANES) and half * 2 * reps == LANES
    return cos2, sin2


def kernel(x, ev_w_in, ev_w_out, rw_mu, rw_w0, rw_w2, rw_a0, rw_a2, rw_g2, rw_k_k, rw_k_a, rw_r_k,
           rw_ln_w, rw_ln_b, rw_v0, rw_v1, rw_v2, od_w_in, od_q_norm, od_kv_norm, od_w_uq, od_w_ukv,
           od_w_out, ln_mix_g, ln_mix_b, ln_ffn_g, ln_ffn_b, moe_w_r, moe_b_r, moe_w1, moe_b1,
           moe_w2, moe_b2):
    B, S, D = x.shape
    T = B * S
    h32 = x.reshape(T, D)
    h16 = h32.astype(BF16)
    cos2, sin2 = _rope_tables(S)
    n_main = 3 * MOBA_DIM + 3 * RWKV_DIM
    lora_pad = -(-RWKV_LORA // LANES) * LANES
    qd = MLA_NOPE_DIM + MLA_ROPE_DIM
    perm = np.concatenate([
        (np.arange(MLA_HEADS)[:, None] * qd + np.arange(MLA_NOPE_DIM)[None, :]).reshape(-1),
        (np.arange(MLA_HEADS)[:, None] * qd + MLA_NOPE_DIM + np.arange(MLA_ROPE_DIM)[None, :]).reshape(-1)])
    v_first = None
    for layer in range(DEPTH):
        j = layer // 2
        if layer % 2 == 0:
            w_in = ev_w_in[j]
            w_main = w_in[:, :n_main].astype(BF16)
            w_lora = jnp.zeros((D, lora_pad), BF16).at[:, :RWKV_LORA].set(w_in[:, n_main:].astype(BF16))
            proj = matmul(h16, w_main, F32)
            plora = matmul(h16, w_lora, F32)
            a_out = moba_attention(proj, B, S)
            v_lora = None if j == 0 else (rw_v0[j - 1], rw_v1[j - 1], rw_v2[j - 1])
            r, lw, k, v, kk, a, g = rwkv_prep(proj, plora, S, rw_mu[j], rw_w0[j], rw_w2[j], rw_a0[j],
                                             rw_a2[j], rw_g2[j], rw_k_k[j], rw_k_a[j], v_first, v_lora)
            if j == 0:
                v_first = v
            b_out = rwkv_scan(r, lw, k, v, kk, a, g, rw_r_k[j], rw_ln_w[j], rw_ln_b[j], B, S)
            w_out = ev_w_out[j].astype(BF16)
            h32, h16 = proj_residual_ln([a_out, b_out], [w_out[:MOBA_DIM], w_out[MOBA_DIM:]], h32,
                                        ln_mix_g[layer], ln_mix_b[layer])
        else:
            c, kr2 = mla_cproj(h16, od_w_in[j], cos2, sin2, S)
            q = rms_up_proj(c, 0, od_q_norm[j], od_w_uq[j][:, perm].astype(BF16), cos2, sin2, S,
                            rope_tile=2, out_scale=MLA_SCORE_SCALE)
            kv = rms_up_proj(c, 1, od_kv_norm[j], od_w_ukv[j].astype(BF16), cos2, sin2, S,
                             rope_tile=None)
            o = mla_attention(q, kv, kr2, B, S)
            h32, h16 = proj_residual_ln([o], [od_w_out[j].astype(BF16)], h32,
                                        ln_mix_g[layer], ln_mix_b[layer])
        h32, h16 = moe_layer(h32, h16, moe_w_r[layer], moe_b_r[layer], moe_w1, moe_b1[layer], moe_w2,
                             moe_b2[layer], ln_ffn_g[layer], ln_ffn_b[layer], layer)
    return h32.reshape(B, S, D)
```

```python
import functools
import math

import jax
import jax.numpy as jnp
import numpy as np
from jax import lax
from jax.experimental import pallas as pl
from jax.experimental.pallas import tpu as pltpu

F32 = jnp.float32
BF16 = jnp.bfloat16
HIGHEST = lax.Precision.HIGHEST

DEPTH = 4
MOBA_HEADS = 16
MOBA_HEAD_DIM = 64
MOBA_DIM = MOBA_HEADS * MOBA_HEAD_DIM
MOBA_BLOCK = 256
MOBA_TOPK = 3
RWKV_HEADS = 16
RWKV_HEAD_DIM = 64
RWKV_DIM = RWKV_HEADS * RWKV_HEAD_DIM
RWKV_DECAY_LORA = 64
RWKV_A_LORA = 64
RWKV_G_LORA = 160
RWKV_LORA = RWKV_DECAY_LORA + RWKV_A_LORA + RWKV_G_LORA
RWKV_GN_EPS = 64e-5
RWKV_CHUNK = 64
MLA_HEADS = 16
MLA_Q_LORA = 512
MLA_KV_LORA = 512
MLA_NOPE_DIM = 128
MLA_ROPE_DIM = 64
MLA_V_DIM = 128
ROPE_THETA = 10000.0
N_EXPERTS = 32
TOP_K = 4
D_EXPERT = 1024
SWIGLU_ALPHA = 1.702
SWIGLU_LIMIT = 7.0
MOE_ROWS = 256
DEEPNORM_ALPHA = (2 * DEPTH) ** 0.25
LN_EPS = 1e-5
RMS_EPS = 1e-6
NEG_INF = -1e30

LANES = 128
VMEM_LIMIT = 56 * 1024 * 1024


def _cparams(*sem):
    return pltpu.CompilerParams(dimension_semantics=sem, vmem_limit_bytes=VMEM_LIMIT)


def _dot(a, b, precision=None):
    return jnp.dot(a, b, preferred_element_type=F32, precision=precision)


def _dot_nt(a, b, precision=None):
    return lax.dot_general(a, b, (((1,), (1,)), ((), ())), preferred_element_type=F32,
                           precision=precision)


def _dot_tn(a, b, precision=None):
    return lax.dot_general(a, b, (((0,), (0,)), ((), ())), preferred_element_type=F32,
                           precision=precision)


def _mm_kernel(a_ref, w_ref, o_ref):
    o_ref[...] = _dot(a_ref[...], w_ref[...]).astype(o_ref.dtype)


def matmul(a, w, out_dtype, tm=1024, tn=512):
    M, K = a.shape
    N = w.shape[1]
    tm, tn = min(tm, M), min(tn, N)
    assert M % tm == 0 and N % tn == 0
    return pl.pallas_call(
        _mm_kernel,
        grid=(M // tm, N // tn),
        in_specs=[pl.BlockSpec((tm, K), lambda i, j: (i, 0)),
                  pl.BlockSpec((K, tn), lambda i, j: (0, j))],
        out_specs=pl.BlockSpec((tm, tn), lambda i, j: (i, j)),
        out_shape=jax.ShapeDtypeStruct((M, N), out_dtype),
        compiler_params=_cparams("parallel", "parallel"),
        name="matmul",
    )(a, w)


def _layer_norm_rows(z, g, b):
    mu = jnp.mean(z, axis=-1, keepdims=True)
    zc = z - mu
    var = jnp.mean(zc * zc, axis=-1, keepdims=True)
    return zc * lax.rsqrt(var + LN_EPS) * g + b


def _proj_ln_kernel(n_in, *refs):
    a_refs = refs[:n_in]
    w_refs = refs[n_in:2 * n_in]
    h_ref, g_ref, b_ref, o32_ref, o16_ref = refs[2 * n_in:]
    acc = DEEPNORM_ALPHA * h_ref[...]
    for a_ref, w_ref in zip(a_refs, w_refs):
        acc = acc + _dot(a_ref[...], w_ref[...])
    y = _layer_norm_rows(acc, g_ref[...], b_ref[...])
    o32_ref[...] = y
    o16_ref[...] = y.astype(BF16)


def proj_residual_ln(a_list, w_list, h, g, b, tm=256):
    M, D = h.shape
    n_in = len(a_list)
    in_specs = ([pl.BlockSpec((tm, a.shape[1]), lambda i: (i, 0)) for a in a_list]
                + [pl.BlockSpec(w.shape, lambda i: (0, 0)) for w in w_list]
                + [pl.BlockSpec((tm, D), lambda i: (i, 0)),
                   pl.BlockSpec((1, D), lambda i: (0, 0)),
                   pl.BlockSpec((1, D), lambda i: (0, 0))])
    return pl.pallas_call(
        functools.partial(_proj_ln_kernel, n_in),
        grid=(M // tm,),
        in_specs=in_specs,
        out_specs=[pl.BlockSpec((tm, D), lambda i: (i, 0)),
                   pl.BlockSpec((tm, D), lambda i: (i, 0))],
        out_shape=[jax.ShapeDtypeStruct((M, D), F32), jax.ShapeDtypeStruct((M, D), BF16)],
        compiler_params=_cparams("parallel"),
        name="proj_residual_ln",
    )(*a_list, *w_list, h, g.reshape(1, D), b.reshape(1, D))


def _combine_ln_kernel(y_ref, gate_ref, h_ref, g_ref, b_ref, o32_ref, o16_ref):
    acc = DEEPNORM_ALPHA * h_ref[...]
    gates = gate_ref[...]
    for k in range(TOP_K):
        acc = acc + y_ref[k].astype(F32) * gates[:, k:k + 1]
    y = _layer_norm_rows(acc, g_ref[...], b_ref[...])
    o32_ref[...] = y
    o16_ref[...] = y.astype(BF16)


def combine_residual_ln(y4, gates, h, g, b, tm=256):
    M, D = h.shape
    return pl.pallas_call(
        _combine_ln_kernel,
        grid=(M // tm,),
        in_specs=[pl.BlockSpec((TOP_K, tm, D), lambda i: (0, i, 0)),
                  pl.BlockSpec((tm, TOP_K), lambda i: (i, 0)),
                  pl.BlockSpec((tm, D), lambda i: (i, 0)),
                  pl.BlockSpec((1, D), lambda i: (0, 0)),
                  pl.BlockSpec((1, D), lambda i: (0, 0))],
        out_specs=[pl.BlockSpec((tm, D), lambda i: (i, 0)),
                   pl.BlockSpec((tm, D), lambda i: (i, 0))],
        out_shape=[jax.ShapeDtypeStruct((M, D), F32), jax.ShapeDtypeStruct((M, D), BF16)],
        compiler_params=_cparams("parallel"),
        name="combine_residual_ln",
    )(y4, gates, h, g.reshape(1, D), b.reshape(1, D))


def _router_kernel(h_ref, w_ref, b_ref, gate_ref, idx_ref, cnt_ref, base_ref):
    i = pl.program_id(0)
    tm = h_ref.shape[0]

    @pl.when(i == 0)
    def _():
        base_ref[...] = jnp.zeros_like(base_ref)

    lane = lax.broadcasted_iota(jnp.int32, (tm, LANES), 1)
    logits = _dot(h_ref[...], w_ref[...], precision=HIGHEST) + b_ref[...]
    val = jnp.where(lane < N_EXPERTS, logits, -jnp.inf)
    tops, idxs = [], []
    member = jnp.zeros((tm, LANES), F32)
    lane_f = lane.astype(F32)
    for _ in range(TOP_K):
        mx = jnp.max(val, axis=-1, keepdims=True)
        ix = jnp.min(jnp.where(val == mx, lane_f, float(LANES)), axis=-1, keepdims=True).astype(jnp.int32)
        hit = lane == ix
        member = member + jnp.where(hit, 1.0, 0.0)
        val = jnp.where(hit, -jnp.inf, val)
        tops.append(mx)
        idxs.append(ix)
    ex = [jnp.exp(t - tops[0]) for t in tops]
    den = ex[0] + ex[1] + ex[2] + ex[3]
    ti = lax.broadcasted_iota(jnp.int32, (tm, tm), 0)
    si = lax.broadcasted_iota(jnp.int32, (tm, tm), 1)
    csum = base_ref[...] + _dot((ti >= si).astype(BF16), member.astype(BF16))
    gate_out = jnp.zeros((tm, LANES), F32)
    idx_out = jnp.zeros((tm, LANES), jnp.int32)
    for k in range(TOP_K):
        rank = jnp.sum(jnp.where(lane == idxs[k], csum - 1.0, 0.0), axis=-1, keepdims=True)
        gate_out = jnp.where(lane == k, ex[k] / den, gate_out)
        idx_out = jnp.where(lane == k, idxs[k], idx_out)
        idx_out = jnp.where(lane == TOP_K + k, rank.astype(jnp.int32), idx_out)
    gate_ref[...] = gate_out
    idx_ref[...] = idx_out
    base_ref[...] = csum[tm - 1:tm, :]
    cnt_ref[...] = csum[tm - 1:tm, :].astype(jnp.int32)


def route_tokens(h, w_r, b_r, tm=512):
    M, D = h.shape
    tm = min(tm, M)
    w = jnp.zeros((D, LANES), F32).at[:, :N_EXPERTS].set(w_r)
    b = jnp.zeros((1, LANES), F32).at[0, :N_EXPERTS].set(b_r)
    gate, idx, cnt = pl.pallas_call(
        _router_kernel,
        grid=(M // tm,),
        in_specs=[pl.BlockSpec((tm, D), lambda i: (i, 0)),
                  pl.BlockSpec((D, LANES), lambda i: (0, 0)),
                  pl.BlockSpec((1, LANES), lambda i: (0, 0))],
        out_specs=[pl.BlockSpec((tm, LANES), lambda i: (i, 0)),
                   pl.BlockSpec((tm, LANES), lambda i: (i, 0)),
                   pl.BlockSpec((1, LANES), lambda i: (0, 0))],
        out_shape=[jax.ShapeDtypeStruct((M, LANES), F32), jax.ShapeDtypeStruct((M, LANES), jnp.int32),
                   jax.ShapeDtypeStruct((1, LANES), jnp.int32)],
        scratch_shapes=[pltpu.VMEM((1, LANES), F32)],
        compiler_params=_cparams("arbitrary"),
        name="route_tokens",
    )(h, w, b)
    return gate[:, :TOP_K], idx[:, :TOP_K], idx[:, TOP_K:2 * TOP_K], cnt[0, :N_EXPERTS]


MOE_W_CHUNKS = 4


def _expert_kernel(cb_ref, ce_ref, le_ref, lc_ref, fl_ref, x_ref, w1c_ref, w2c_ref, b1_ref, b2_ref,
                   o_ref, w1a_ref, w2a_ref, w1b_ref, w2b_ref):
    s = pl.program_id(0)
    fl = fl_ref[s]
    comp = (fl & 1) == 1
    load = (fl & 2) == 2
    par = (fl & 4) == 4
    c = lc_ref[s]
    C1 = w1c_ref.shape[0]
    C2 = w2c_ref.shape[0]

    def cast_chunk(w1_dst, w2_dst):
        w1_dst[pl.ds(pl.multiple_of(c * C1, C1), C1), :] = w1c_ref[...].astype(BF16)
        w2_dst[pl.ds(pl.multiple_of(c * C2, C2), C2), :] = w2c_ref[...].astype(BF16)

    def ffn(w1_src, w2_src):
        hgu = _dot(x_ref[...], w1_src[...]) + b1_ref[0]
        gate = jnp.minimum(hgu[:, :D_EXPERT], SWIGLU_LIMIT)
        up = jnp.clip(hgu[:, D_EXPERT:], -SWIGLU_LIMIT, SWIGLU_LIMIT)
        act = gate * jax.nn.sigmoid(SWIGLU_ALPHA * gate) * (up + 1.0)
        o_ref[...] = (_dot(act.astype(BF16), w2_src[...]) + b2_ref[0]).astype(o_ref.dtype)

    bufs = ((w1a_ref, w2a_ref), (w1b_ref, w2b_ref))
    for p in (0, 1):
        use, fill = bufs[p], bufs[1 - p]
        in_phase = par == (p == 1)

        @pl.when(comp & in_phase)
        def _():
            cast_chunk(*fill)
            ffn(*use)

        @pl.when(load & jnp.logical_not(comp) & in_phase)
        def _():
            cast_chunk(*fill)

    @pl.when((fl & 8) == 8)
    def _():
        o_ref[...] = jnp.zeros_like(o_ref)


def _expert_schedule(nblk, n_blocks):
    E, NC = N_EXPERTS, MOE_W_CHUNKS
    ph_ids = jnp.arange(E + 1, dtype=jnp.int32)
    prev_n = jnp.concatenate([jnp.zeros((1,), jnp.int32), nblk.astype(jnp.int32)])
    plen = jnp.where(ph_ids == 0, NC, jnp.where(ph_ids == E, prev_n, jnp.maximum(prev_n, NC)))
    pend = jnp.cumsum(plen)
    pstart = pend - plen
    n_steps = n_blocks + E * NC
    s = jnp.arange(n_steps, dtype=jnp.int32)
    ph = jnp.minimum(jnp.sum(pend[None, :] <= s[:, None], axis=1), E).astype(jnp.int32)
    t = s - pstart[ph]
    comp = (ph >= 1) & (t < prev_n[ph])
    load = (ph < E) & (t < NC)
    n_used = jnp.sum(nblk).astype(jnp.int32)
    tail_blk = n_used + (s - pend[E])
    fill = (s >= pend[E]) & (tail_blk < n_blocks)
    comp_blk = jnp.where(s >= pend[E], jnp.minimum(tail_blk, n_blocks - 1),
                         jnp.maximum(jnp.cumsum(comp.astype(jnp.int32)) - 1, 0))
    comp_e = jnp.clip(ph - 1, 0, E - 1)
    load_e = jnp.minimum(ph, E - 1)
    load_c = jnp.where(ph >= E, NC - 1, jnp.minimum(t, NC - 1))
    flags = (comp.astype(jnp.int32) + 2 * load.astype(jnp.int32)
             + 4 * ((ph - 1) % 2 == 1).astype(jnp.int32) + 8 * fill.astype(jnp.int32))
    return comp_blk.astype(jnp.int32), comp_e.astype(jnp.int32), load_e, load_c.astype(jnp.int32), flags


def expert_ffn(xbuf, nblk, w1, b1, w2, b2, layer):
    rows, D = xbuf.shape
    R = MOE_ROWS
    NC = MOE_W_CHUNKS
    n_blocks = rows // R
    H2 = 2 * D_EXPERT
    C1, C2 = D // NC, D_EXPERT // NC
    sched = _expert_schedule(nblk, n_blocks)
    n_steps = sched[0].shape[0]
    grid_spec = pltpu.PrefetchScalarGridSpec(
        num_scalar_prefetch=5,
        grid=(n_steps,),
        in_specs=[pl.BlockSpec((R, D), lambda s, cb, ce, le, lc, fl: (cb[s], 0)),
                  pl.BlockSpec((None, None, C1, H2), lambda s, cb, ce, le, lc, fl: (layer, le[s], lc[s], 0)),
                  pl.BlockSpec((None, None, C2, D), lambda s, cb, ce, le, lc, fl: (layer, le[s], lc[s], 0)),
                  pl.BlockSpec((1, 1, H2), lambda s, cb, ce, le, lc, fl: (ce[s], 0, 0)),
                  pl.BlockSpec((1, 1, D), lambda s, cb, ce, le, lc, fl: (ce[s], 0, 0))],
        out_specs=pl.BlockSpec((R, D), lambda s, cb, ce, le, lc, fl: (cb[s], 0)),
        scratch_shapes=[pltpu.VMEM((D, H2), BF16), pltpu.VMEM((D_EXPERT, D), BF16),
                        pltpu.VMEM((D, H2), BF16), pltpu.VMEM((D_EXPERT, D), BF16)],
    )
    return pl.pallas_call(
        _expert_kernel,
        grid_spec=grid_spec,
        out_shape=jax.ShapeDtypeStruct((rows, D), BF16),
        compiler_params=_cparams("arbitrary"),
        name="expert_ffn",
    )(*sched, xbuf, w1, w2, b1.reshape(N_EXPERTS, 1, -1), b2.reshape(N_EXPERTS, 1, -1))


def moe_layer(h32, h16, w_r, b_r, w1, b1, w2, b2, ln_g, ln_b, layer):
    T, D = h32.shape
    R = MOE_ROWS
    M = T * TOP_K
    gates, top_idx, rank, counts = route_tokens(h32, w_r, b_r)
    padded = (counts + R - 1) // R * R
    pad_end = jnp.cumsum(padded)
    pad_start = pad_end - padded
    onehot = top_idx[:, :, None] == jnp.arange(N_EXPERTS, dtype=jnp.int32)[None, None, :]
    pos = rank + jnp.sum(jnp.where(onehot, pad_start[None, None, :], 0), axis=-1)
    n_blocks = (M + N_EXPERTS * (R - 1) + R - 1) // R
    rows = n_blocks * R
    src_tok = (jnp.arange(rows, dtype=jnp.int32) % T).at[pos.reshape(M)].set(
        jnp.arange(M, dtype=jnp.int32) // TOP_K, unique_indices=True)
    xbuf = h16[src_tok]
    ybuf = expert_ffn(xbuf, padded // R, w1, b1, w2, b2, layer)
    y4 = ybuf[pos.T]
    return combine_residual_ln(y4, gates, h32, ln_g, ln_b)


MOBA_GROUP = 4
MOBA_GATE_ROWS = 8


def _col_groups(x, op):
    rows, cols = x.shape
    return op(x.reshape(rows // 8, 8, cols), axis=0)


def _moba_kernel(q_ref, k_ref, v_ref, o_ref, kmean_ref, vt_ref, bias_ref):
    pg = pl.program_id(1)
    cur = pl.program_id(2)
    BLK = MOBA_BLOCK
    GP = MOBA_GROUP
    NR = MOBA_GATE_ROWS
    nb = k_ref.shape[0] // BLK
    assert nb <= NR
    lane = lax.broadcasted_iota(jnp.int32, (1, LANES), 1)
    LOG2E = math.log2(math.e)
    c2 = MOBA_HEAD_DIM ** -0.5 * LOG2E

    @pl.when(cur == 0)
    def _():
        kmean_ref[...] = jnp.zeros_like(kmean_ref)
        for g in range(GP):
            for n in range(nb):
                rows = slice(n * BLK, (n + 1) * BLK)
                kmean_ref[g, n:n + 1, :] = jnp.mean(k_ref[rows, g * LANES:(g + 1) * LANES], axis=0,
                                                    keepdims=True)
                vt_ref[g, :, rows] = v_ref[rows, g * LANES:(g + 1) * LANES].T.astype(BF16)

    Q2 = 2 * BLK
    key = lax.broadcasted_iota(jnp.int32, (BLK, Q2), 0)
    qry = lax.broadcasted_iota(jnp.int32, (BLK, Q2), 1) % BLK
    krow = lax.broadcasted_iota(jnp.int32, (BLK, LANES), 0)
    klane = lax.broadcasted_iota(jnp.int32, (BLK, LANES), 1)
    k_extra = jnp.where(klane < 2, krow, 0).astype(F32).astype(BF16)
    srow = lax.broadcasted_iota(jnp.int32, (LANES, BLK), 0)
    blk_row = lax.broadcasted_iota(jnp.int32, (NR, BLK), 0)

    qts = []
    for g in range(GP):
        q = q_ref[:, g * LANES:(g + 1) * LANES]
        halves, biases = [], []
        for hh in range(2):
            own = (lane // MOBA_HEAD_DIM) == hh
            qm_t = jnp.where(own, q, 0.0).T
            head_idx = 2 * (pg * GP + g) + hh + 1
            sl2 = jnp.exp(jnp.full((1, 1), head_idx, jnp.int32).astype(F32)
                          * (-8.0 * math.log(2.0) / MOBA_HEADS)) * LOG2E
            s_hi = sl2.astype(BF16).astype(F32)
            s_lo = (sl2 - s_hi).astype(BF16).astype(F32)
            q_extra = jnp.where(srow == 0, s_hi, jnp.where(srow == 1, s_lo, 0.0))
            halves.append(jnp.concatenate([qm_t * c2, q_extra], axis=0).astype(BF16))
            gate = _dot(kmean_ref[g], qm_t, precision=HIGHEST)
            rank = jnp.zeros((NR, BLK), jnp.int32)
            for m in range(nb):
                gm = gate[m:m + 1, :]
                ahead = (gm > gate) | ((gm == gate) & (m < blk_row))
                rank = rank + jnp.where(ahead, 1, 0) * (m < cur).astype(jnp.int32)
            sel = (rank < MOBA_TOPK) & (blk_row < cur)
            biases.append(jnp.where(sel, -sl2 * ((cur - blk_row) * BLK).astype(F32), NEG_INF))
        qts.append(jnp.concatenate(halves, axis=1))
        bias_ref[g] = jnp.concatenate(biases, axis=1)

    def scores(g, off):
        kc = jnp.concatenate([k_ref[pl.ds(off, BLK), g * LANES:(g + 1) * LANES].astype(BF16), k_extra],
                             axis=1)
        return _dot(kc, qts[g])

    def softmax_step(t, m_run, l_run, bias):
        if bias is None:
            t = jnp.where(key <= qry, t, NEG_INF)
            m_new = jnp.maximum(m_run, jnp.max(_col_groups(t, jnp.max), axis=0, keepdims=True))
            shift = m_new
        else:
            m_new = jnp.maximum(m_run, jnp.max(_col_groups(t, jnp.max), axis=0, keepdims=True) + bias)
            shift = m_new - bias
        alpha = jnp.exp2(m_run - m_new)
        pr = jnp.exp2(t - shift)
        l_new = alpha * l_run + _col_groups(pr, jnp.sum)
        return m_new, l_new, alpha, pr.astype(BF16)

    def block(g, n, carry, bias):
        m_run, l_run, acc = carry
        off = pl.multiple_of(n * BLK, BLK)
        m_new, l_new, alpha, pr = softmax_step(scores(g, off), m_run, l_run, bias)
        return m_new, l_new, alpha * acc + _dot(vt_ref[g, :, pl.ds(off, BLK)], pr)

    init = (jnp.full((1, Q2), NEG_INF, F32), jnp.zeros((8, Q2), F32), jnp.zeros((LANES, Q2), F32))
    carry = tuple(block(g, cur, init, None) for g in range(GP))

    def body(n, carry):
        return tuple(block(g, n, carry[g], bias_ref[g, pl.ds(n, 1), :]) for g in range(GP))

    carry = lax.fori_loop(0, cur, body, carry)
    row128 = lax.broadcasted_iota(jnp.int32, (LANES, 1), 0)
    for g in range(GP):
        m_run, l_run, acc = carry[g]
        out = acc / jnp.sum(l_run, axis=0, keepdims=True)
        out_t = jnp.where(row128 < MOBA_HEAD_DIM, out[:, :BLK], out[:, BLK:])
        o_ref[:, g * LANES:(g + 1) * LANES] = out_t.T.astype(o_ref.dtype)


def moba_attention(proj, B, S):
    T = B * S
    BLK = MOBA_BLOCK
    GP = MOBA_GROUP
    W = GP * LANES
    n_grp = MOBA_DIM // W
    nq = S // BLK
    return pl.pallas_call(
        _moba_kernel,
        grid=(B, n_grp, nq),
        in_specs=[pl.BlockSpec((BLK, W), lambda b, p, c: (b * nq + c, p)),
                  pl.BlockSpec((S, W), lambda b, p, c: (b, n_grp + p)),
                  pl.BlockSpec((S, W), lambda b, p, c: (b, 2 * n_grp + p))],
        out_specs=pl.BlockSpec((BLK, W), lambda b, p, c: (b * nq + c, p)),
        out_shape=jax.ShapeDtypeStruct((T, MOBA_DIM), BF16),
        scratch_shapes=[pltpu.VMEM((GP, MOBA_GATE_ROWS, LANES), F32),
                        pltpu.VMEM((GP, LANES, S), BF16),
                        pltpu.VMEM((GP, MOBA_GATE_ROWS, 2 * BLK), F32)],
        compiler_params=_cparams("parallel", "parallel", "arbitrary"),
        name="moba_attention",
    )(proj, proj, proj)


def _rwkv_prep_kernel(has_vres, S, *refs):
    if has_vres:
        (pm_ref, pl_ref, pm_prev_ref, pl_prev_ref, mu_m_ref, mu_l_ref, w0_ref, w2_ref, a0_ref, a2_ref,
         g2_ref, kk_ref, ka_ref, vfirst_ref, v0_ref, v1_ref, v2_ref,
         r_o, lw_o, k_o, v_o, kkn_o, a_o, g_o) = refs
    else:
        (pm_ref, pl_ref, pm_prev_ref, pl_prev_ref, mu_m_ref, mu_l_ref, w0_ref, w2_ref, a0_ref, a2_ref,
         g2_ref, kk_ref, ka_ref,
         r_o, lw_o, k_o, v_o, kkn_o, a_o, g_o) = refs
    i = pl.program_id(0)
    tm = pm_ref.shape[0]
    C = RWKV_DIM
    row = lax.broadcasted_iota(jnp.int32, (tm, 1), 0)
    seq_start = (i * tm) % S == 0

    def shifted(cur_ref, prev_ref, mu_ref):
        x = cur_ref[...]
        prev_row = jnp.where(seq_start, 0.0, prev_ref[7:8, :])
        xs = jnp.where(row == 0, prev_row, pltpu.roll(x, 1, 0))
        return x + (xs - x) * mu_ref[...]

    pm = shifted(pm_ref, pm_prev_ref, mu_m_ref)
    plo = shifted(pl_ref, pl_prev_ref, mu_l_ref)
    r = pm[:, :C]
    k = pm[:, C:2 * C]
    v = pm[:, 2 * C:]
    wd = plo[:, :RWKV_DECAY_LORA]
    ad = plo[:, RWKV_DECAY_LORA:RWKV_DECAY_LORA + RWKV_A_LORA]
    gd = plo[:, RWKV_DECAY_LORA + RWKV_A_LORA:RWKV_LORA]
    w = -jax.nn.softplus(-(w0_ref[...] + _dot(jnp.tanh(wd).astype(BF16), w2_ref[...]))) - 0.5
    a = jax.nn.sigmoid(a0_ref[...] + _dot(ad.astype(BF16), a2_ref[...]))
    g = _dot(jax.nn.sigmoid(gd).astype(BF16), g2_ref[...])
    if has_vres:
        lo = _dot(_dot(v.astype(BF16), v1_ref[...]).astype(BF16), v2_ref[...])
        v = v + (vfirst_ref[...] - v) * jax.nn.sigmoid(v0_ref[...] + lo)
    kk = k * kk_ref[...]
    hid_r = lax.broadcasted_iota(jnp.int32, (LANES, LANES), 0) // RWKV_HEAD_DIM
    hid_c = lax.broadcasted_iota(jnp.int32, (LANES, LANES), 1) // RWKV_HEAD_DIM
    ones_bd = (hid_r == hid_c).astype(F32)
    for pp in range(C // LANES):
        sl = slice(pp * LANES, (pp + 1) * LANES)
        kkp = kk[:, sl]
        ss = _dot(kkp * kkp, ones_bd, precision=HIGHEST)
        kkn_o[:, sl] = kkp * lax.rsqrt(jnp.maximum(ss, 1e-24))
    r_o[...] = r
    lw_o[...] = -jnp.exp(w)
    k_o[...] = k * (1.0 + (a - 1.0) * ka_ref[...])
    v_o[...] = v
    a_o[...] = a
    g_o[...] = g


def rwkv_prep(proj, plora, S, mu, w0, w2, a0, a2, g2, k_k, k_a, v_first, v_lora, tm=256):
    T = proj.shape[0]
    C = RWKV_DIM
    LP = plora.shape[1]
    has_vres = v_lora is not None
    mu_m = mu[:3 * C].reshape(1, 3 * C)
    mu_l = jnp.zeros((1, LP), F32).at[0, :RWKV_LORA].set(mu[3 * C:])
    row = lambda z: z.reshape(1, -1)
    full = lambda z: pl.BlockSpec(z.shape, lambda i: (0,) * z.ndim)
    args = [proj, plora, proj, plora, mu_m, mu_l, row(w0), w2.astype(BF16), row(a0), a2.astype(BF16),
            g2.astype(BF16), row(k_k), row(k_a)]
    in_specs = [pl.BlockSpec((tm, 3 * C), lambda i: (i, 1)),
                pl.BlockSpec((tm, LP), lambda i: (i, 0)),
                pl.BlockSpec((8, 3 * C), lambda i: (jnp.maximum(i * (tm // 8) - 1, 0), 1)),
                pl.BlockSpec((8, LP), lambda i: (jnp.maximum(i * (tm // 8) - 1, 0), 0))]
    in_specs += [full(z) for z in args[4:]]
    if has_vres:
        v0, v1, v2 = v_lora
        extra = [v_first, row(v0), v1.astype(BF16), v2.astype(BF16)]
        args += extra
        in_specs += [pl.BlockSpec((tm, C), lambda i: (i, 0))] + [full(z) for z in extra[1:]]
    out_spec = pl.BlockSpec((tm, C), lambda i: (i, 0))
    return pl.pallas_call(
        functools.partial(_rwkv_prep_kernel, has_vres, S),
        grid=(T // tm,),
        in_specs=in_specs,
        out_specs=[out_spec] * 7,
        out_shape=[jax.ShapeDtypeStruct((T, C), F32)] * 7,
        compiler_params=_cparams("parallel"),
        name="rwkv_prep",
    )(*args)


def _split3_bf16(x):
    x0 = x.astype(BF16)
    r1 = x - x0.astype(F32)
    x1 = r1.astype(BF16)
    x2 = (r1 - x1.astype(F32)).astype(BF16)
    return x0, x1, x2


def _sum01_left(m01, x):
    x0, x1, x2 = _split3_bf16(x)
    return _dot(m01, x0) + (_dot(m01, x1) + _dot(m01, x2))


def _sum01_right(x, m01):
    x0, x1, x2 = _split3_bf16(x)
    return _dot(x0, m01) + (_dot(x1, m01) + _dot(x2, m01))


_NN = (((1,), (0,)), ((), ()))
_NT = (((1,), (1,)), ((), ()))
_TN = (((0,), (0,)), ((), ()))


def _mm(a, b, passes, dims=_NN):
    dg = lambda x, y: lax.dot_general(x, y, dims, preferred_element_type=F32)
    if passes == 1:
        return dg(a.astype(BF16), b.astype(BF16))
    a_hi = a.astype(BF16)
    a_lo = (a - a_hi.astype(F32)).astype(BF16)
    b_hi = b.astype(BF16)
    b_lo = (b - b_hi.astype(F32)).astype(BF16)
    return dg(a_hi, b_hi) + (dg(a_hi, b_lo) + dg(a_lo, b_hi))


RWKV_PASSES = dict(gram=1, inv=1, apply=1, state=3)


def _rwkv_scan_kernel(r_ref, lw_ref, k_ref, v_ref, kk_ref, a_ref, g_ref, rk_ref, lnw_ref, lnb_ref,
                      o_ref, state_ref):
    c = pl.program_id(1)
    CH = RWKV_CHUNK
    N = RWKV_HEAD_DIM
    P2 = 2 * CH
    NP = RWKV_DIM // LANES
    pg, pi, pa, ps = (RWKV_PASSES[n] for n in ("gram", "inv", "apply", "state"))

    @pl.when(c == 0)
    def _():
        state_ref[...] = jnp.zeros_like(state_ref)

    lane = lax.broadcasted_iota(jnp.int32, (1, LANES), 1)
    head0 = lane < N
    ri = lax.broadcasted_iota(jnp.int32, (P2, P2), 0)
    ci = lax.broadcasted_iota(jnp.int32, (P2, P2), 1)
    same_head = (ri // CH) == (ci // CH)
    strict = (ri % CH) > (ci % CH)
    incl = (ri % CH) >= (ci % CH)
    eye = (ri == ci).astype(F32)
    ones_bd = same_head.astype(BF16)
    ti = lax.broadcasted_iota(jnp.int32, (CH, CH), 0)
    si = lax.broadcasted_iota(jnp.int32, (CH, CH), 1)
    tril_incl = (ti >= si).astype(BF16)

    def stack(x):
        return jnp.concatenate([jnp.where(head0, x, 0.0), jnp.where(head0, 0.0, x)], axis=0)

    pairs = range(NP)
    sls = [slice(p * LANES, (p + 1) * LANES) for p in pairs]
    r = [r_ref[:, sl] for sl in sls]
    k = [k_ref[:, sl] for sl in sls]
    v = [v_ref[:, sl] for sl in sls]
    kk = [kk_ref[:, sl] for sl in sls]
    lw = [lw_ref[:, sl] for sl in sls]
    cum = [_sum01_left(tril_incl, lw[p]) for p in pairs]
    cum_end = [cum[p][CH - 1:CH, :] for p in pairs]
    b = [kk[p] * a_ref[:, sls[p]] for p in pairs]
    e_neg = [jnp.exp(-cum[p]) for p in pairs]
    e_end = [jnp.exp(cum_end[p] - cum[p]) for p in pairs]
    A_st = [stack(-kk[p] * jnp.exp(cum[p] - lw[p])) for p in pairs]
    R_st = [stack(r[p] * jnp.exp(cum[p])) for p in pairs]
    BK = [jnp.concatenate([stack(b[p] * e_neg[p]), stack(k[p] * e_neg[p])], axis=0) for p in pairs]
    Bend_st = [stack(b[p] * e_end[p]) for p in pairs]
    Kend_st = [stack(k[p] * e_end[p]) for p in pairs]
    V_st = [stack(v[p]) for p in pairs]

    G = [_mm(jnp.concatenate([A_st[p], R_st[p]], axis=0), BK[p], pg, _NT) for p in pairs]
    Lab = [jnp.where(strict, G[p][:P2, :P2], 0.0) for p in pairs]
    Lak = [jnp.where(strict, G[p][:P2, P2:], 0.0) for p in pairs]
    Mrb = [jnp.where(incl, G[p][P2:, :P2], 0.0) for p in pairs]
    Mrk = [jnp.where(incl, G[p][P2:, P2:], 0.0) for p in pairs]

    Tinv = [eye + Lab[p] for p in pairs]
    Lp = Lab
    for _ in range(int(math.log2(CH)) - 1):
        Lp = [_mm(Lp[p], Lp[p], pi) for p in pairs]
        Tinv = [Tinv[p] + _mm(Tinv[p], Lp[p], pi) for p in pairs]

    LakV = [_mm(Lak[p], V_st[p], pa) for p in pairs]
    AU = [_mm(Tinv[p], jnp.concatenate([A_st[p], LakV[p]], axis=1), pa) for p in pairs]
    MM = [_mm(Mrb[p], AU[p], pa) for p in pairs]
    MV = [_mm(Mrk[p], V_st[p], pa) for p in pairs]
    BT = [_mm(Bend_st[p], AU[p], pa, _TN) for p in pairs]
    KV = [_mm(Kend_st[p], V_st[p], pa, _TN) for p in pairs]
    Rhat = [R_st[p] + MM[p][:, :LANES] for p in pairs]
    Mmat = [eye * jnp.exp(cum_end[p]) + BT[p][:, :LANES] for p in pairs]

    RS = [_mm(jnp.concatenate([Rhat[p], Mmat[p]], axis=0), state_ref[p], ps) for p in pairs]
    for p in pairs:
        state_ref[p] = jnp.where(same_head, RS[p][P2:] + BT[p][:, LANES:] + KV[p], 0.0)
    Y_st = [RS[p][:P2] + MM[p][:, LANES:] + MV[p] for p in pairs]
    y = [Y_st[p][:CH] + Y_st[p][CH:] for p in pairs]

    sums = [_sum01_right(jnp.concatenate([y[p], r[p] * k[p] * rk_ref[:, sls[p]]], axis=0), ones_bd)
            for p in pairs]
    yc = [y[p] - sums[p][:CH] * (1.0 / N) for p in pairs]
    var = [_sum01_right(yc[p] * yc[p], ones_bd) * (1.0 / N) for p in pairs]
    for p in pairs:
        yn = yc[p] * lax.rsqrt(var[p] + RWKV_GN_EPS) * lnw_ref[:, sls[p]] + lnb_ref[:, sls[p]]
        o_ref[:, sls[p]] = ((yn + sums[p][CH:] * v[p]) * g_ref[:, sls[p]]).astype(o_ref.dtype)


def rwkv_scan(r, lw, k, v, kk, a, g, r_k, ln_w, ln_b, B, S):
    T, C = r.shape
    CH = RWKV_CHUNK
    nc = S // CH
    blk = pl.BlockSpec((CH, C), lambda b, c: (b * nc + c, 0))
    par = pl.BlockSpec((1, C), lambda b, c: (0, 0))
    return pl.pallas_call(
        _rwkv_scan_kernel,
        grid=(B, nc),
        in_specs=[blk] * 7 + [par] * 3,
        out_specs=blk,
        out_shape=jax.ShapeDtypeStruct((T, C), BF16),
        scratch_shapes=[pltpu.VMEM((C // LANES, LANES, LANES), F32)],
        compiler_params=_cparams("parallel", "arbitrary"),
        name="rwkv_scan",
    )(r, lw, k, v, kk, a, g, r_k.reshape(1, C), ln_w.reshape(1, C), ln_b.reshape(1, C))


def _rope_pairs(x, cos2, sin2):
    lane = lax.broadcasted_iota(jnp.int32, (1, LANES), 1)
    first_half = (lane % MLA_ROPE_DIM) < (MLA_ROPE_DIM // 2)
    partner = jnp.where(first_half, pltpu.roll(x, LANES - MLA_ROPE_DIM // 2, 1),
                        pltpu.roll(x, MLA_ROPE_DIM // 2, 1))
    return x * cos2 + partner * sin2


def _mla_cproj_kernel(h_ref, w_ref, cos_ref, sin_ref, c_ref, kr_ref):
    acc = _dot(h_ref[...], w_ref[...])
    NC = MLA_Q_LORA + MLA_KV_LORA
    c_ref[...] = acc[:, :NC]
    kr_ref[...] = _rope_pairs(acc[:, NC:], cos_ref[...], sin_ref[...]).astype(kr_ref.dtype)


def mla_cproj(h16, w_in, cos2, sin2, S, tm=512):
    T, D = h16.shape
    NC = MLA_Q_LORA + MLA_KV_LORA
    w = jnp.concatenate([w_in, w_in[:, NC:]], axis=1).astype(BF16)
    ns = S // tm
    return pl.pallas_call(
        _mla_cproj_kernel,
        grid=(T // tm,),
        in_specs=[pl.BlockSpec((tm, D), lambda i: (i, 0)),
                  pl.BlockSpec((D, NC + LANES), lambda i: (0, 0)),
                  pl.BlockSpec((tm, LANES), lambda i: (i % ns, 0)),
                  pl.BlockSpec((tm, LANES), lambda i: (i % ns, 0))],
        out_specs=[pl.BlockSpec((tm, NC), lambda i: (i, 0)),
                   pl.BlockSpec((tm, LANES), lambda i: (i, 0))],
        out_shape=[jax.ShapeDtypeStruct((T, NC), F32), jax.ShapeDtypeStruct((T, LANES), BF16)],
        compiler_params=_cparams("parallel"),
        name="mla_cproj",
    )(h16, w, cos2, sin2)


def _rms_up_kernel(rope_tile, out_scale, c_ref, g_ref, w_ref, cos_ref, sin_ref, o_ref, cn_ref):
    j = pl.program_id(1)

    @pl.when(j == 0)
    def _():
        x = c_ref[...]
        ms = jnp.mean(x * x, axis=-1, keepdims=True)
        cn_ref[...] = (x * lax.rsqrt(ms + RMS_EPS) * g_ref[...]).astype(BF16)

    acc = _dot(cn_ref[...], w_ref[...])
    if out_scale != 1.0:
        acc = acc * out_scale

    if rope_tile is None:
        o_ref[...] = acc.astype(o_ref.dtype)
    else:
        @pl.when(j != rope_tile)
        def _():
            o_ref[...] = acc.astype(o_ref.dtype)

        @pl.when(j == rope_tile)
        def _():
            cos2, sin2 = cos_ref[...], sin_ref[...]
            for gidx in range(acc.shape[1] // LANES):
                sl = slice(gidx * LANES, (gidx + 1) * LANES)
                o_ref[:, sl] = _rope_pairs(acc[:, sl], cos2, sin2).astype(o_ref.dtype)


def rms_up_proj(c, col_block, gain, w, cos2, sin2, S, rope_tile, out_scale=1.0, tm=512, tn=1024):
    T = c.shape[0]
    K, N = w.shape
    ns = S // tm
    return pl.pallas_call(
        functools.partial(_rms_up_kernel, rope_tile, out_scale),
        grid=(T // tm, N // tn),
        in_specs=[pl.BlockSpec((tm, K), lambda i, j: (i, col_block)),
                  pl.BlockSpec((1, K), lambda i, j: (0, 0)),
                  pl.BlockSpec((K, tn), lambda i, j: (0, j)),
                  pl.BlockSpec((tm, LANES), lambda i, j: (i % ns, 0)),
                  pl.BlockSpec((tm, LANES), lambda i, j: (i % ns, 0))],
        out_specs=pl.BlockSpec((tm, tn), lambda i, j: (i, j)),
        out_shape=jax.ShapeDtypeStruct((T, N), BF16),
        scratch_shapes=[pltpu.VMEM((tm, K), BF16)],
        compiler_params=_cparams("parallel", "arbitrary"),
        name="rms_up_proj",
    )(c, gain.reshape(1, K), w, cos2, sin2)


MLA_GROUP = 4
MLA_SCORE_SCALE = (MLA_NOPE_DIM + MLA_ROPE_DIM) ** -0.5 * math.log2(math.e)


def _mla_attn_kernel(qn_ref, qr_ref, kv_ref, kr_ref, o_ref, vt_ref):
    qi = pl.program_id(2)
    TQ = qn_ref.shape[0]
    S = kv_ref.shape[0]
    G = MLA_GROUP
    lane = lax.broadcasted_iota(jnp.int32, (1, LANES), 1)

    @pl.when(qi == 0)
    def _():
        for j in range(G):
            for n in range(S // TQ):
                rows = slice(n * TQ, (n + 1) * TQ)
                v_t = kv_ref[rows, (2 * j + 1) * LANES:(2 * j + 2) * LANES]
                vt_ref[j, :, rows] = v_t.astype(F32).T.astype(BF16)

    key = lax.broadcasted_iota(jnp.int32, (TQ, TQ), 0)
    qry = lax.broadcasted_iota(jnp.int32, (TQ, TQ), 1)
    qts = []
    for j in range(G):
        own = (lane // MLA_ROPE_DIM) == (j % 2)
        qr = qr_ref[:, (j // 2) * LANES:(j // 2 + 1) * LANES]
        qc = jnp.concatenate([qn_ref[:, j * LANES:(j + 1) * LANES],
                              jnp.where(own, qr, jnp.zeros_like(qr))], axis=1)
        qts.append(qc.astype(F32).T.astype(BF16))

    def scores(j, off):
        kc = jnp.concatenate([kv_ref[pl.ds(off, TQ), 2 * j * LANES:(2 * j + 1) * LANES],
                              kr_ref[pl.ds(off, TQ), :]], axis=1)
        return _dot(kc, qts[j])

    def softmax_step(t, m_run, l_run, diag):
        if diag:
            t = jnp.where(key <= qry, t, NEG_INF)
        m_new = jnp.maximum(m_run, jnp.max(_col_groups(t, jnp.max), axis=0, keepdims=True))
        alpha = jnp.exp2(m_run - m_new)
        pr = jnp.exp2(t - m_new)
        l_new = alpha * l_run + _col_groups(pr, jnp.sum)
        return m_new, l_new, alpha, pr.astype(BF16)

    def weighted_values(j, off, acc, alpha, pr):
        return alpha * acc + _dot(vt_ref[j, :, pl.ds(off, TQ)], pr)

    def body(n, carry):
        off = pl.multiple_of(n * TQ, TQ)
        out = []
        for j in range(G):
            m_run, l_run, acc = carry[j]
            m_new, l_new, alpha, pr = softmax_step(scores(j, off), m_run, l_run, False)
            out.append((m_new, l_new, weighted_values(j, off, acc, alpha, pr)))
        return tuple(out)

    carry = tuple((jnp.full((1, TQ), NEG_INF, F32), jnp.zeros((8, TQ), F32),
                   jnp.zeros((LANES, TQ), F32)) for j in range(G))
    carry = lax.fori_loop(0, qi, body, carry)
    off_d = pl.multiple_of(qi * TQ, TQ)
    for j in range(G):
        m_run, l_run, acc = carry[j]
        m_new, l_new, alpha, pr = softmax_step(scores(j, off_d), m_run, l_run, True)
        acc = weighted_values(j, off_d, acc, alpha, pr)
        out_t = acc / jnp.sum(l_new, axis=0, keepdims=True)
        o_ref[:, j * LANES:(j + 1) * LANES] = out_t.T.astype(o_ref.dtype)


def mla_attention(q, kv, kr2, B, S, tq=512):
    T = B * S
    H, G = MLA_HEADS, MLA_GROUP
    nq = S // tq
    n_nope = H // G
    return pl.pallas_call(
        _mla_attn_kernel,
        grid=(B, H // G, nq),
        in_specs=[pl.BlockSpec((tq, G * LANES), lambda b, g, i: (b * nq + i, g)),
                  pl.BlockSpec((tq, G // 2 * LANES), lambda b, g, i: (b * nq + i, 2 * n_nope + g)),
                  pl.BlockSpec((S, 2 * G * LANES), lambda b, g, i: (b, g)),
                  pl.BlockSpec((S, LANES), lambda b, g, i: (b, 0))],
        out_specs=pl.BlockSpec((tq, G * LANES), lambda b, g, i: (b * nq + i, g)),
        out_shape=jax.ShapeDtypeStruct((T, H * MLA_V_DIM), BF16),
        scratch_shapes=[pltpu.VMEM((G, LANES, S), BF16)],
        compiler_params=_cparams("parallel", "parallel", "arbitrary"),
        name="mla_attention",
    )(q, q, kv, kr2)


def _rope_tables(S):
    half = MLA_ROPE_DIM // 2
    inv = ROPE_THETA ** (-jnp.arange(0, MLA_ROPE_DIM, 2, dtype=F32) / MLA_ROPE_DIM)
    ang = jnp.arange(S, dtype=F32)[:, None] * inv[None, :]
    cos, sin = jnp.cos(ang), jnp.sin(ang)
    reps = LANES // MLA_ROPE_DIM
    cos2 = jnp.tile(jnp.concatenate([cos, cos], axis=1), (1, reps))
    sin2 = jnp.tile(jnp.concatenate([-sin, sin], axis=1), (1, reps))
    assert cos2.shape == (S, LANES) and half * 2 * reps == LANES
    return cos2, sin2


def kernel(x, ev_w_in, ev_w_out, rw_mu, rw_w0, rw_w2, rw_a0, rw_a2, rw_g2, rw_k_k, rw_k_a, rw_r_k,
           rw_ln_w, rw_ln_b, rw_v0, rw_v1, rw_v2, od_w_in, od_q_norm, od_kv_norm, od_w_uq, od_w_ukv,
           od_w_out, ln_mix_g, ln_mix_b, ln_ffn_g, ln_ffn_b, moe_w_r, moe_b_r, moe_w1, moe_b1,
           moe_w2, moe_b2):
    B, S, D = x.shape
    T = B * S
    h32 = x.reshape(T, D)
    h16 = h32.astype(BF16)
    cos2, sin2 = _rope_tables(S)
    n_main = 3 * MOBA_DIM + 3 * RWKV_DIM
    lora_pad = -(-RWKV_LORA // LANES) * LANES
    qd = MLA_NOPE_DIM + MLA_ROPE_DIM
    perm = np.concatenate([
        (np.arange(MLA_HEADS)[:, None] * qd + np.arange(MLA_NOPE_DIM)[None, :]).reshape(-1),
        (np.arange(MLA_HEADS)[:, None] * qd + MLA_NOPE_DIM + np.arange(MLA_ROPE_DIM)[None, :]).reshape(-1)])
    v_first = None
    for layer in range(DEPTH):
        j = layer // 2
        if layer % 2 == 0:
            w_in = ev_w_in[j]
            w_main = w_in[:, :n_main].astype(BF16)
            w_lora = jnp.zeros((D, lora_pad), BF16).at[:, :RWKV_LORA].set(w_in[:, n_main:].astype(BF16))
            proj = matmul(h16, w_main, F32)
            plora = matmul(h16, w_lora, F32)
            a_out = moba_attention(proj, B, S)
            v_lora = None if j == 0 else (rw_v0[j - 1], rw_v1[j - 1], rw_v2[j - 1])
            r, lw, k, v, kk, a, g = rwkv_prep(proj, plora, S, rw_mu[j], rw_w0[j], rw_w2[j], rw_a0[j],
                                             rw_a2[j], rw_g2[j], rw_k_k[j], rw_k_a[j], v_first, v_lora)
            if j == 0:
                v_first = v
            b_out = rwkv_scan(r, lw, k, v, kk, a, g, rw_r_k[j], rw_ln_w[j], rw_ln_b[j], B, S)
            w_out = ev_w_out[j].astype(BF16)
            h32, h16 = proj_residual_ln([a_out, b_out], [w_out[:MOBA_DIM], w_out[MOBA_DIM:]], h32,
                                        ln_mix_g[layer], ln_mix_b[layer])
        else:
            c, kr2 = mla_cproj(h16, od_w_in[j], cos2, sin2, S)
            q = rms_up_proj(c, 0, od_q_norm[j], od_w_uq[j][:, perm].astype(BF16), cos2, sin2, S,
                            rope_tile=2, out_scale=MLA_SCORE_SCALE)
            kv = rms_up_proj(c, 1, od_kv_norm[j], od_w_ukv[j].astype(BF16), cos2, sin2, S,
                             rope_tile=None)
            o = mla_attention(q, kv, kr2, B, S)
            h32, h16 = proj_residual_ln([o], [od_w_out[j].astype(BF16)], h32,
                                        ln_mix_g[layer], ln_mix_b[layer])
        h32, h16 = moe_layer(h32, h16, moe_w_r[layer], moe_b_r[layer], moe_w1, moe_b1[layer], moe_w2,
                             moe_b2[layer], ln_ffn_g[layer], ln_ffn_b[layer], layer)
    return h32.reshape(B, S, D)
```

```python
import functools
import math

import jax
import jax.numpy as jnp
import numpy as np
from jax import lax
from jax.experimental import pallas as pl
from jax.experimental.pallas import tpu as pltpu

F32 = jnp.float32
BF16 = jnp.bfloat16
HIGHEST = lax.Precision.HIGHEST

DEPTH = 4
MOBA_HEADS = 16
MOBA_HEAD_DIM = 64
MOBA_DIM = MOBA_HEADS * MOBA_HEAD_DIM
MOBA_BLOCK = 256
MOBA_TOPK = 3
RWKV_HEADS = 16
RWKV_HEAD_DIM = 64
RWKV_DIM = RWKV_HEADS * RWKV_HEAD_DIM
RWKV_DECAY_LORA = 64
RWKV_A_LORA = 64
RWKV_G_LORA = 160
RWKV_LORA = RWKV_DECAY_LORA + RWKV_A_LORA + RWKV_G_LORA
RWKV_GN_EPS = 64e-5
RWKV_CHUNK = 64
MLA_HEADS = 16
MLA_Q_LORA = 512
MLA_KV_LORA = 512
MLA_NOPE_DIM = 128
MLA_ROPE_DIM = 64
MLA_V_DIM = 128
ROPE_THETA = 10000.0
N_EXPERTS = 32
TOP_K = 4
D_EXPERT = 1024
SWIGLU_ALPHA = 1.702
SWIGLU_LIMIT = 7.0
MOE_ROWS = 256
DEEPNORM_ALPHA = (2 * DEPTH) ** 0.25
LN_EPS = 1e-5
RMS_EPS = 1e-6
NEG_INF = -1e30

LANES = 128
VMEM_LIMIT = 56 * 1024 * 1024


def _cparams(*sem):
    return pltpu.CompilerParams(dimension_semantics=sem, vmem_limit_bytes=VMEM_LIMIT)


def _dot(a, b, precision=None):
    return jnp.dot(a, b, preferred_element_type=F32, precision=precision)


def _dot_nt(a, b, precision=None):
    return lax.dot_general(a, b, (((1,), (1,)), ((), ())), preferred_element_type=F32,
                           precision=precision)


def _dot_tn(a, b, precision=None):
    return lax.dot_general(a, b, (((0,), (0,)), ((), ())), preferred_element_type=F32,
                           precision=precision)


def _mm_kernel(a_ref, w_ref, o_ref):
    o_ref[...] = _dot(a_ref[...], w_ref[...]).astype(o_ref.dtype)


def matmul(a, w, out_dtype, tm=1024, tn=512):
    M, K = a.shape
    N = w.shape[1]
    tm, tn = min(tm, M), min(tn, N)
    assert M % tm == 0 and N % tn == 0
    return pl.pallas_call(
        _mm_kernel,
        grid=(M // tm, N // tn),
        in_specs=[pl.BlockSpec((tm, K), lambda i, j: (i, 0)),
                  pl.BlockSpec((K, tn), lambda i, j: (0, j))],
        out_specs=pl.BlockSpec((tm, tn), lambda i, j: (i, j)),
        out_shape=jax.ShapeDtypeStruct((M, N), out_dtype),
        compiler_params=_cparams("parallel", "parallel"),
        name="matmul",
    )(a, w)


def _layer_norm_rows(z, g, b):
    mu = jnp.mean(z, axis=-1, keepdims=True)
    zc = z - mu
    var = jnp.mean(zc * zc, axis=-1, keepdims=True)
    return zc * lax.rsqrt(var + LN_EPS) * g + b


def _proj_ln_kernel(n_in, *refs):
    a_refs = refs[:n_in]
    w_refs = refs[n_in:2 * n_in]
    h_ref, g_ref, b_ref, o32_ref, o16_ref = refs[2 * n_in:]
    acc = DEEPNORM_ALPHA * h_ref[...]
    for a_ref, w_ref in zip(a_refs, w_refs):
        acc = acc + _dot(a_ref[...], w_ref[...])
    y = _layer_norm_rows(acc, g_ref[...], b_ref[...])
    o32_ref[...] = y
    o16_ref[...] = y.astype(BF16)


def proj_residual_ln(a_list, w_list, h, g, b, tm=256):
    M, D = h.shape
    n_in = len(a_list)
    in_specs = ([pl.BlockSpec((tm, a.shape[1]), lambda i: (i, 0)) for a in a_list]
                + [pl.BlockSpec(w.shape, lambda i: (0, 0)) for w in w_list]
                + [pl.BlockSpec((tm, D), lambda i: (i, 0)),
                   pl.BlockSpec((1, D), lambda i: (0, 0)),
                   pl.BlockSpec((1, D), lambda i: (0, 0))])
    return pl.pallas_call(
        functools.partial(_proj_ln_kernel, n_in),
        grid=(M // tm,),
        in_specs=in_specs,
        out_specs=[pl.BlockSpec((tm, D), lambda i: (i, 0)),
                   pl.BlockSpec((tm, D), lambda i: (i, 0))],
        out_shape=[jax.ShapeDtypeStruct((M, D), F32), jax.ShapeDtypeStruct((M, D), BF16)],
        compiler_params=_cparams("parallel"),
        name="proj_residual_ln",
    )(*a_list, *w_list, h, g.reshape(1, D), b.reshape(1, D))


def _combine_ln_kernel(y_ref, gate_ref, h_ref, g_ref, b_ref, o32_ref, o16_ref):
    acc = DEEPNORM_ALPHA * h_ref[...]
    gates = gate_ref[...]
    for k in range(TOP_K):
        acc = acc + y_ref[k].astype(F32) * gates[:, k:k + 1]
    y = _layer_norm_rows(acc, g_ref[...], b_ref[...])
    o32_ref[...] = y
    o16_ref[...] = y.astype(BF16)


def combine_residual_ln(y4, gates, h, g, b, tm=256):
    M, D = h.shape
    return pl.pallas_call(
        _combine_ln_kernel,
        grid=(M // tm,),
        in_specs=[pl.BlockSpec((TOP_K, tm, D), lambda i: (0, i, 0)),
                  pl.BlockSpec((tm, TOP_K), lambda i: (i, 0)),
                  pl.BlockSpec((tm, D), lambda i: (i, 0)),
                  pl.BlockSpec((1, D), lambda i: (0, 0)),
                  pl.BlockSpec((1, D), lambda i: (0, 0))],
        out_specs=[pl.BlockSpec((tm, D), lambda i: (i, 0)),
                   pl.BlockSpec((tm, D), lambda i: (i, 0))],
        out_shape=[jax.ShapeDtypeStruct((M, D), F32), jax.ShapeDtypeStruct((M, D), BF16)],
        compiler_params=_cparams("parallel"),
        name="combine_residual_ln",
    )(y4, gates, h, g.reshape(1, D), b.reshape(1, D))


def _router_kernel(h_ref, w_ref, b_ref, gate_ref, idx_ref, cnt_ref, base_ref):
    i = pl.program_id(0)
    tm = h_ref.shape[0]

    @pl.when(i == 0)
    def _():
        base_ref[...] = jnp.zeros_like(base_ref)

    lane = lax.broadcasted_iota(jnp.int32, (tm, LANES), 1)
    logits = _mm(h_ref[...], w_ref[...], 3) + b_ref[...]
    val = jnp.where(lane < N_EXPERTS, logits, -jnp.inf)
    tops, idxs = [], []
    member = jnp.zeros((tm, LANES), F32)
    lane_f = lane.astype(F32)
    for _ in range(TOP_K):
        mx = jnp.max(val, axis=-1, keepdims=True)
        ix = jnp.min(jnp.where(val == mx, lane_f, float(LANES)), axis=-1, keepdims=True).astype(jnp.int32)
        hit = lane == ix
        member = member + jnp.where(hit, 1.0, 0.0)
        val = jnp.where(hit, -jnp.inf, val)
        tops.append(mx)
        idxs.append(ix)
    ex = [jnp.exp(t - tops[0]) for t in tops]
    den = ex[0] + ex[1] + ex[2] + ex[3]
    ti = lax.broadcasted_iota(jnp.int32, (tm, tm), 0)
    si = lax.broadcasted_iota(jnp.int32, (tm, tm), 1)
    csum = base_ref[...] + _dot((ti >= si).astype(BF16), member.astype(BF16))
    gate_out = jnp.zeros((tm, LANES), F32)
    idx_out = jnp.zeros((tm, LANES), jnp.int32)
    for k in range(TOP_K):
        rank = jnp.sum(jnp.where(lane == idxs[k], csum - 1.0, 0.0), axis=-1, keepdims=True)
        gate_out = jnp.where(lane == k, ex[k] / den, gate_out)
        idx_out = jnp.where(lane == k, idxs[k], idx_out)
        idx_out = jnp.where(lane == TOP_K + k, rank.astype(jnp.int32), idx_out)
    gate_ref[...] = gate_out
    idx_ref[...] = idx_out
    base_ref[...] = csum[tm - 1:tm, :]
    cnt_ref[...] = csum[tm - 1:tm, :].astype(jnp.int32)


def route_tokens(h, w_r, b_r, tm=512):
    M, D = h.shape
    tm = min(tm, M)
    w = jnp.zeros((D, LANES), F32).at[:, :N_EXPERTS].set(w_r)
    b = jnp.zeros((1, LANES), F32).at[0, :N_EXPERTS].set(b_r)
    gate, idx, cnt = pl.pallas_call(
        _router_kernel,
        grid=(M // tm,),
        in_specs=[pl.BlockSpec((tm, D), lambda i: (i, 0)),
                  pl.BlockSpec((D, LANES), lambda i: (0, 0)),
                  pl.BlockSpec((1, LANES), lambda i: (0, 0))],
        out_specs=[pl.BlockSpec((tm, LANES), lambda i: (i, 0)),
                   pl.BlockSpec((tm, LANES), lambda i: (i, 0)),
                   pl.BlockSpec((1, LANES), lambda i: (0, 0))],
        out_shape=[jax.ShapeDtypeStruct((M, LANES), F32), jax.ShapeDtypeStruct((M, LANES), jnp.int32),
                   jax.ShapeDtypeStruct((1, LANES), jnp.int32)],
        scratch_shapes=[pltpu.VMEM((1, LANES), F32)],
        compiler_params=_cparams("arbitrary"),
        name="route_tokens",
    )(h, w, b)
    return gate[:, :TOP_K], idx[:, :TOP_K], idx[:, TOP_K:2 * TOP_K], cnt[0, :N_EXPERTS]


MOE_W_CHUNKS = 4


def _expert_kernel(cb_ref, ce_ref, le_ref, lc_ref, fl_ref, x_ref, w1c_ref, w2c_ref, b1_ref, b2_ref,
                   o_ref, w1a_ref, w2a_ref, w1b_ref, w2b_ref):
    s = pl.program_id(0)
    fl = fl_ref[s]
    comp = (fl & 1) == 1
    load = (fl & 2) == 2
    par = (fl & 4) == 4
    c = lc_ref[s]
    C1 = w1c_ref.shape[0]
    C2 = w2c_ref.shape[0]

    def cast_chunk(w1_dst, w2_dst):
        w1_dst[pl.ds(pl.multiple_of(c * C1, C1), C1), :] = w1c_ref[...].astype(BF16)
        w2_dst[pl.ds(pl.multiple_of(c * C2, C2), C2), :] = w2c_ref[...].astype(BF16)

    def ffn(w1_src, w2_src):
        hgu = _dot(x_ref[...], w1_src[...]) + b1_ref[0]
        gate = jnp.minimum(hgu[:, :D_EXPERT], SWIGLU_LIMIT)
        up = jnp.clip(hgu[:, D_EXPERT:], -SWIGLU_LIMIT, SWIGLU_LIMIT)
        act = gate * jax.nn.sigmoid(SWIGLU_ALPHA * gate) * (up + 1.0)
        o_ref[...] = (_dot(act.astype(BF16), w2_src[...]) + b2_ref[0]).astype(o_ref.dtype)

    bufs = ((w1a_ref, w2a_ref), (w1b_ref, w2b_ref))
    for p in (0, 1):
        use, fill = bufs[p], bufs[1 - p]
        in_phase = par == (p == 1)

        @pl.when(comp & in_phase)
        def _():
            cast_chunk(*fill)
            ffn(*use)

        @pl.when(load & jnp.logical_not(comp) & in_phase)
        def _():
            cast_chunk(*fill)

    @pl.when((fl & 8) == 8)
    def _():
        o_ref[...] = jnp.zeros_like(o_ref)


def _expert_schedule(nblk, n_blocks):
    E, NC = N_EXPERTS, MOE_W_CHUNKS
    ph_ids = jnp.arange(E + 1, dtype=jnp.int32)
    prev_n = jnp.concatenate([jnp.zeros((1,), jnp.int32), nblk.astype(jnp.int32)])
    plen = jnp.where(ph_ids == 0, NC, jnp.where(ph_ids == E, prev_n, jnp.maximum(prev_n, NC)))
    pend = jnp.cumsum(plen)
    pstart = pend - plen
    n_steps = n_blocks + E * NC
    s = jnp.arange(n_steps, dtype=jnp.int32)
    ph = jnp.minimum(jnp.sum(pend[None, :] <= s[:, None], axis=1), E).astype(jnp.int32)
    t = s - pstart[ph]
    comp = (ph >= 1) & (t < prev_n[ph])
    load = (ph < E) & (t < NC)
    n_used = jnp.sum(nblk).astype(jnp.int32)
    tail_blk = n_used + (s - pend[E])
    fill = (s >= pend[E]) & (tail_blk < n_blocks)
    comp_blk = jnp.where(s >= pend[E], jnp.minimum(tail_blk, n_blocks - 1),
                         jnp.maximum(jnp.cumsum(comp.astype(jnp.int32)) - 1, 0))
    comp_e = jnp.clip(ph - 1, 0, E - 1)
    load_e = jnp.minimum(ph, E - 1)
    load_c = jnp.where(ph >= E, NC - 1, jnp.minimum(t, NC - 1))
    flags = (comp.astype(jnp.int32) + 2 * load.astype(jnp.int32)
             + 4 * ((ph - 1) % 2 == 1).astype(jnp.int32) + 8 * fill.astype(jnp.int32))
    return comp_blk.astype(jnp.int32), comp_e.astype(jnp.int32), load_e, load_c.astype(jnp.int32), flags


def expert_ffn(xbuf, nblk, w1, b1, w2, b2, layer):
    rows, D = xbuf.shape
    R = MOE_ROWS
    NC = MOE_W_CHUNKS
    n_blocks = rows // R
    H2 = 2 * D_EXPERT
    C1, C2 = D // NC, D_EXPERT // NC
    sched = _expert_schedule(nblk, n_blocks)
    n_steps = sched[0].shape[0]
    grid_spec = pltpu.PrefetchScalarGridSpec(
        num_scalar_prefetch=5,
        grid=(n_steps,),
        in_specs=[pl.BlockSpec((R, D), lambda s, cb, ce, le, lc, fl: (cb[s], 0)),
                  pl.BlockSpec((None, None, C1, H2), lambda s, cb, ce, le, lc, fl: (layer, le[s], lc[s], 0)),
                  pl.BlockSpec((None, None, C2, D), lambda s, cb, ce, le, lc, fl: (layer, le[s], lc[s], 0)),
                  pl.BlockSpec((1, 1, H2), lambda s, cb, ce, le, lc, fl: (ce[s], 0, 0)),
                  pl.BlockSpec((1, 1, D), lambda s, cb, ce, le, lc, fl: (ce[s], 0, 0))],
        out_specs=pl.BlockSpec((R, D), lambda s, cb, ce, le, lc, fl: (cb[s], 0)),
        scratch_shapes=[pltpu.VMEM((D, H2), BF16), pltpu.VMEM((D_EXPERT, D), BF16),
                        pltpu.VMEM((D, H2), BF16), pltpu.VMEM((D_EXPERT, D), BF16)],
    )
    return pl.pallas_call(
        _expert_kernel,
        grid_spec=grid_spec,
        out_shape=jax.ShapeDtypeStruct((rows, D), BF16),
        compiler_params=_cparams("arbitrary"),
        name="expert_ffn",
    )(*sched, xbuf, w1, w2, b1.reshape(N_EXPERTS, 1, -1), b2.reshape(N_EXPERTS, 1, -1))


def moe_layer(h32, h16, w_r, b_r, w1, b1, w2, b2, ln_g, ln_b, layer):
    T, D = h32.shape
    R = MOE_ROWS
    M = T * TOP_K
    gates, top_idx, rank, counts = route_tokens(h32, w_r, b_r)
    padded = (counts + R - 1) // R * R
    pad_end = jnp.cumsum(padded)
    pad_start = pad_end - padded
    pos = rank + pad_start[top_idx]
    n_blocks = (M + N_EXPERTS * (R - 1) + R - 1) // R
    rows = n_blocks * R
    src_tok = (jnp.arange(rows, dtype=jnp.int32) % T).at[pos.reshape(M)].set(
        jnp.arange(M, dtype=jnp.int32) // TOP_K, unique_indices=True)
    xbuf = h16[src_tok]
    ybuf = expert_ffn(xbuf, padded // R, w1, b1, w2, b2, layer)
    y4 = ybuf[pos.T]
    return combine_residual_ln(y4, gates, h32, ln_g, ln_b)


MOBA_GROUP = 4
MOBA_GATE_ROWS = 8


def _col_groups(x, op):
    rows, cols = x.shape
    return op(x.reshape(rows // 8, 8, cols), axis=0)


def _moba_kernel(q_ref, k_ref, v_ref, o_ref, kmean_ref, vt_ref, bias_ref):
    pg = pl.program_id(1)
    cur = pl.program_id(2)
    BLK = MOBA_BLOCK
    GP = MOBA_GROUP
    NR = MOBA_GATE_ROWS
    nb = k_ref.shape[0] // BLK
    assert nb <= NR
    lane = lax.broadcasted_iota(jnp.int32, (1, LANES), 1)
    LOG2E = math.log2(math.e)
    c2 = MOBA_HEAD_DIM ** -0.5 * LOG2E

    @pl.when(cur == 0)
    def _():
        kmean_ref[...] = jnp.zeros_like(kmean_ref)
        for g in range(GP):
            for n in range(nb):
                rows = slice(n * BLK, (n + 1) * BLK)
                kmean_ref[g, n:n + 1, :] = jnp.mean(k_ref[rows, g * LANES:(g + 1) * LANES], axis=0,
                                                    keepdims=True)
                vt_ref[g, :, rows] = v_ref[rows, g * LANES:(g + 1) * LANES].T.astype(BF16)

    Q2 = 2 * BLK
    key = lax.broadcasted_iota(jnp.int32, (BLK, Q2), 0)
    qry = lax.broadcasted_iota(jnp.int32, (BLK, Q2), 1) % BLK
    krow = lax.broadcasted_iota(jnp.int32, (BLK, LANES), 0)
    klane = lax.broadcasted_iota(jnp.int32, (BLK, LANES), 1)
    k_extra = jnp.where(klane < 2, krow, 0).astype(F32).astype(BF16)
    srow = lax.broadcasted_iota(jnp.int32, (LANES, BLK), 0)
    blk_row = lax.broadcasted_iota(jnp.int32, (NR, BLK), 0)

    qts = []
    for g in range(GP):
        q = q_ref[:, g * LANES:(g + 1) * LANES]
        halves, biases = [], []
        for hh in range(2):
            own = (lane // MOBA_HEAD_DIM) == hh
            qm_t = jnp.where(own, q, 0.0).T
            head_idx = 2 * (pg * GP + g) + hh + 1
            sl2 = jnp.exp(jnp.full((1, 1), head_idx, jnp.int32).astype(F32)
                          * (-8.0 * math.log(2.0) / MOBA_HEADS)) * LOG2E
            s_hi = sl2.astype(BF16).astype(F32)
            s_lo = (sl2 - s_hi).astype(BF16).astype(F32)
            q_extra = jnp.where(srow == 0, s_hi, jnp.where(srow == 1, s_lo, 0.0))
            halves.append(jnp.concatenate([qm_t * c2, q_extra], axis=0).astype(BF16))
            gate = _dot(kmean_ref[g], qm_t, precision=HIGHEST)
            rank = jnp.zeros((NR, BLK), jnp.int32)
            for m in range(nb):
                gm = gate[m:m + 1, :]
                ahead = (gm > gate) | ((gm == gate) & (m < blk_row))
                rank = rank + jnp.where(ahead, 1, 0) * (m < cur).astype(jnp.int32)
            sel = (rank < MOBA_TOPK) & (blk_row < cur)
            biases.append(jnp.where(sel, -sl2 * ((cur - blk_row) * BLK).astype(F32), NEG_INF))
        qts.append(jnp.concatenate(halves, axis=1))
        bias_ref[g] = jnp.concatenate(biases, axis=1)

    def scores(g, off):
        kc = jnp.concatenate([k_ref[pl.ds(off, BLK), g * LANES:(g + 1) * LANES].astype(BF16), k_extra],
                             axis=1)
        return _dot(kc, qts[g])

    def softmax_step(t, m_run, l_run, bias):
        if bias is None:
            t = jnp.where(key <= qry, t, NEG_INF)
            m_new = jnp.maximum(m_run, jnp.max(_col_groups(t, jnp.max), axis=0, keepdims=True))
            shift = m_new
        else:
            m_new = jnp.maximum(m_run, jnp.max(_col_groups(t, jnp.max), axis=0, keepdims=True) + bias)
            shift = m_new - bias
        alpha = jnp.exp2(m_run - m_new)
        pr = jnp.exp2(t - shift)
        l_new = alpha * l_run + _col_groups(pr, jnp.sum)
        return m_new, l_new, alpha, pr.astype(BF16)

    def block(g, n, carry, bias):
        m_run, l_run, acc = carry
        off = pl.multiple_of(n * BLK, BLK)
        m_new, l_new, alpha, pr = softmax_step(scores(g, off), m_run, l_run, bias)
        return m_new, l_new, alpha * acc + _dot(vt_ref[g, :, pl.ds(off, BLK)], pr)

    init = (jnp.full((1, Q2), NEG_INF, F32), jnp.zeros((8, Q2), F32), jnp.zeros((LANES, Q2), F32))
    carry = tuple(block(g, cur, init, None) for g in range(GP))

    def body(n, carry):
        return tuple(block(g, n, carry[g], bias_ref[g, pl.ds(n, 1), :]) for g in range(GP))

    carry = lax.fori_loop(0, cur, body, carry)
    row128 = lax.broadcasted_iota(jnp.int32, (LANES, 1), 0)
    for g in range(GP):
        m_run, l_run, acc = carry[g]
        out = acc / jnp.sum(l_run, axis=0, keepdims=True)
        out_t = jnp.where(row128 < MOBA_HEAD_DIM, out[:, :BLK], out[:, BLK:])
        o_ref[:, g * LANES:(g + 1) * LANES] = out_t.T.astype(o_ref.dtype)


def moba_attention(proj, B, S):
    T = B * S
    BLK = MOBA_BLOCK
    GP = MOBA_GROUP
    W = GP * LANES
    n_grp = MOBA_DIM // W
    nq = S // BLK
    return pl.pallas_call(
        _moba_kernel,
        grid=(B, n_grp, nq),
        in_specs=[pl.BlockSpec((BLK, W), lambda b, p, c: (b * nq + c, p)),
                  pl.BlockSpec((S, W), lambda b, p, c: (b, n_grp + p)),
                  pl.BlockSpec((S, W), lambda b, p, c: (b, 2 * n_grp + p))],
        out_specs=pl.BlockSpec((BLK, W), lambda b, p, c: (b * nq + c, p)),
        out_shape=jax.ShapeDtypeStruct((T, MOBA_DIM), BF16),
        scratch_shapes=[pltpu.VMEM((GP, MOBA_GATE_ROWS, LANES), F32),
                        pltpu.VMEM((GP, LANES, S), BF16),
                        pltpu.VMEM((GP, MOBA_GATE_ROWS, 2 * BLK), F32)],
        compiler_params=_cparams("parallel", "parallel", "arbitrary"),
        name="moba_attention",
    )(proj, proj, proj)


def _rwkv_prep_kernel(has_vres, S, *refs):
    if has_vres:
        (pm_ref, pl_ref, pm_prev_ref, pl_prev_ref, mu_m_ref, mu_l_ref, w0_ref, w2_ref, a0_ref, a2_ref,
         g2_ref, kk_ref, ka_ref, vfirst_ref, v0_ref, v1_ref, v2_ref,
         r_o, lw_o, k_o, v_o, kkn_o, a_o, g_o) = refs
    else:
        (pm_ref, pl_ref, pm_prev_ref, pl_prev_ref, mu_m_ref, mu_l_ref, w0_ref, w2_ref, a0_ref, a2_ref,
         g2_ref, kk_ref, ka_ref,
         r_o, lw_o, k_o, v_o, kkn_o, a_o, g_o) = refs
    i = pl.program_id(0)
    tm = pm_ref.shape[0]
    C = RWKV_DIM
    row = lax.broadcasted_iota(jnp.int32, (tm, 1), 0)
    seq_start = (i * tm) % S == 0

    def shifted(cur_ref, prev_ref, mu_ref):
        x = cur_ref[...]
        prev_row = jnp.where(seq_start, 0.0, prev_ref[7:8, :])
        xs = jnp.where(row == 0, prev_row, pltpu.roll(x, 1, 0))
        return x + (xs - x) * mu_ref[...]

    pm = shifted(pm_ref, pm_prev_ref, mu_m_ref)
    plo = shifted(pl_ref, pl_prev_ref, mu_l_ref)
    r = pm[:, :C]
    k = pm[:, C:2 * C]
    v = pm[:, 2 * C:]
    wd = plo[:, :RWKV_DECAY_LORA]
    ad = plo[:, RWKV_DECAY_LORA:RWKV_DECAY_LORA + RWKV_A_LORA]
    gd = plo[:, RWKV_DECAY_LORA + RWKV_A_LORA:RWKV_LORA]
    w = -jax.nn.softplus(-(w0_ref[...] + _dot(jnp.tanh(wd).astype(BF16), w2_ref[...]))) - 0.5
    a = jax.nn.sigmoid(a0_ref[...] + _dot(ad.astype(BF16), a2_ref[...]))
    g = _dot(jax.nn.sigmoid(gd).astype(BF16), g2_ref[...])
    if has_vres:
        lo = _dot(_dot(v.astype(BF16), v1_ref[...]).astype(BF16), v2_ref[...])
        v = v + (vfirst_ref[...] - v) * jax.nn.sigmoid(v0_ref[...] + lo)
    kk = k * kk_ref[...]
    hid_r = lax.broadcasted_iota(jnp.int32, (LANES, LANES), 0) // RWKV_HEAD_DIM
    hid_c = lax.broadcasted_iota(jnp.int32, (LANES, LANES), 1) // RWKV_HEAD_DIM
    ones_bd = (hid_r == hid_c).astype(F32)
    for pp in range(C // LANES):
        sl = slice(pp * LANES, (pp + 1) * LANES)
        kkp = kk[:, sl]
        ss = _dot(kkp * kkp, ones_bd, precision=HIGHEST)
        kkn_o[:, sl] = kkp * lax.rsqrt(jnp.maximum(ss, 1e-24))
    r_o[...] = r
    lw_o[...] = -jnp.exp(w)
    k_o[...] = k * (1.0 + (a - 1.0) * ka_ref[...])
    v_o[...] = v
    a_o[...] = a
    g_o[...] = g


def rwkv_prep(proj, plora, S, mu, w0, w2, a0, a2, g2, k_k, k_a, v_first, v_lora, tm=256):
    T = proj.shape[0]
    C = RWKV_DIM
    LP = plora.shape[1]
    has_vres = v_lora is not None
    mu_m = mu[:3 * C].reshape(1, 3 * C)
    mu_l = jnp.zeros((1, LP), F32).at[0, :RWKV_LORA].set(mu[3 * C:])
    row = lambda z: z.reshape(1, -1)
    full = lambda z: pl.BlockSpec(z.shape, lambda i: (0,) * z.ndim)
    args = [proj, plora, proj, plora, mu_m, mu_l, row(w0), w2.astype(BF16), row(a0), a2.astype(BF16),
            g2.astype(BF16), row(k_k), row(k_a)]
    in_specs = [pl.BlockSpec((tm, 3 * C), lambda i: (i, 1)),
                pl.BlockSpec((tm, LP), lambda i: (i, 0)),
                pl.BlockSpec((8, 3 * C), lambda i: (jnp.maximum(i * (tm // 8) - 1, 0), 1)),
                pl.BlockSpec((8, LP), lambda i: (jnp.maximum(i * (tm // 8) - 1, 0), 0))]
    in_specs += [full(z) for z in args[4:]]
    if has_vres:
        v0, v1, v2 = v_lora
        extra = [v_first, row(v0), v1.astype(BF16), v2.astype(BF16)]
        args += extra
        in_specs += [pl.BlockSpec((tm, C), lambda i: (i, 0))] + [full(z) for z in extra[1:]]
    out_spec = pl.BlockSpec((tm, C), lambda i: (i, 0))
    return pl.pallas_call(
        functools.partial(_rwkv_prep_kernel, has_vres, S),
        grid=(T // tm,),
        in_specs=in_specs,
        out_specs=[out_spec] * 7,
        out_shape=[jax.ShapeDtypeStruct((T, C), F32)] * 7,
        compiler_params=_cparams("parallel"),
        name="rwkv_prep",
    )(*args)


def _split3_bf16(x):
    x0 = x.astype(BF16)
    r1 = x - x0.astype(F32)
    x1 = r1.astype(BF16)
    x2 = (r1 - x1.astype(F32)).astype(BF16)
    return x0, x1, x2


def _sum01_left(m01, x):
    x0, x1, x2 = _split3_bf16(x)
    return _dot(m01, x0) + (_dot(m01, x1) + _dot(m01, x2))


def _sum01_right(x, m01):
    x0, x1, x2 = _split3_bf16(x)
    return _dot(x0, m01) + (_dot(x1, m01) + _dot(x2, m01))


_NN = (((1,), (0,)), ((), ()))
_NT = (((1,), (1,)), ((), ()))
_TN = (((0,), (0,)), ((), ()))


def _mm(a, b, passes, dims=_NN):
    dg = lambda x, y: lax.dot_general(x, y, dims, preferred_element_type=F32)
    if passes == 1:
        return dg(a.astype(BF16), b.astype(BF16))
    a_hi = a.astype(BF16)
    a_lo = (a - a_hi.astype(F32)).astype(BF16)
    b_hi = b.astype(BF16)
    b_lo = (b - b_hi.astype(F32)).astype(BF16)
    return dg(a_hi, b_hi) + (dg(a_hi, b_lo) + dg(a_lo, b_hi))


RWKV_PASSES = dict(gram=1, inv=1, apply=1, state=1)


def _rwkv_scan_kernel(r_ref, lw_ref, k_ref, v_ref, kk_ref, a_ref, g_ref, rk_ref, lnw_ref, lnb_ref,
                      o_ref, state_ref):
    c = pl.program_id(1)
    CH = RWKV_CHUNK
    N = RWKV_HEAD_DIM
    P2 = 2 * CH
    NP = RWKV_DIM // LANES
    pg, pi, pa, ps = (RWKV_PASSES[n] for n in ("gram", "inv", "apply", "state"))

    @pl.when(c == 0)
    def _():
        state_ref[...] = jnp.zeros_like(state_ref)

    lane = lax.broadcasted_iota(jnp.int32, (1, LANES), 1)
    head0 = lane < N
    ri = lax.broadcasted_iota(jnp.int32, (P2, P2), 0)
    ci = lax.broadcasted_iota(jnp.int32, (P2, P2), 1)
    same_head = (ri // CH) == (ci // CH)
    strict = (ri % CH) > (ci % CH)
    incl = (ri % CH) >= (ci % CH)
    eye = (ri == ci).astype(F32)
    ones_bd = same_head.astype(BF16)
    ti = lax.broadcasted_iota(jnp.int32, (CH, CH), 0)
    si = lax.broadcasted_iota(jnp.int32, (CH, CH), 1)
    tril_incl = (ti >= si).astype(BF16)

    def stack(x):
        return jnp.concatenate([jnp.where(head0, x, 0.0), jnp.where(head0, 0.0, x)], axis=0)

    pairs = range(NP)
    sls = [slice(p * LANES, (p + 1) * LANES) for p in pairs]
    r = [r_ref[:, sl] for sl in sls]
    k = [k_ref[:, sl] for sl in sls]
    v = [v_ref[:, sl] for sl in sls]
    kk = [kk_ref[:, sl] for sl in sls]
    lw = [lw_ref[:, sl] for sl in sls]
    cum = [_sum01_left(tril_incl, lw[p]) for p in pairs]
    cum_end = [cum[p][CH - 1:CH, :] for p in pairs]
    b = [kk[p] * a_ref[:, sls[p]] for p in pairs]
    e_neg = [jnp.exp(-cum[p]) for p in pairs]
    e_end = [jnp.exp(cum_end[p] - cum[p]) for p in pairs]
    A_st = [stack(-kk[p] * jnp.exp(cum[p] - lw[p])) for p in pairs]
    R_st = [stack(r[p] * jnp.exp(cum[p])) for p in pairs]
    BK = [jnp.concatenate([stack(b[p] * e_neg[p]), stack(k[p] * e_neg[p])], axis=0) for p in pairs]
    Bend_st = [stack(b[p] * e_end[p]) for p in pairs]
    Kend_st = [stack(k[p] * e_end[p]) for p in pairs]
    V_st = [stack(v[p]) for p in pairs]

    G = [_mm(jnp.concatenate([A_st[p], R_st[p]], axis=0), BK[p], pg, _NT) for p in pairs]
    Lab = [jnp.where(strict, G[p][:P2, :P2], 0.0) for p in pairs]
    Lak = [jnp.where(strict, G[p][:P2, P2:], 0.0) for p in pairs]
    Mrb = [jnp.where(incl, G[p][P2:, :P2], 0.0) for p in pairs]
    Mrk = [jnp.where(incl, G[p][P2:, P2:], 0.0) for p in pairs]

    Tinv = [eye + Lab[p] for p in pairs]
    Lp = Lab
    for _ in range(int(math.log2(CH)) - 1):
        Lp = [_mm(Lp[p], Lp[p], pi) for p in pairs]
        Tinv = [Tinv[p] + _mm(Tinv[p], Lp[p], pi) for p in pairs]

    LakV = [_mm(Lak[p], V_st[p], pa) for p in pairs]
    AU = [_mm(Tinv[p], jnp.concatenate([A_st[p], LakV[p]], axis=1), pa) for p in pairs]
    MM = [_mm(Mrb[p], AU[p], pa) for p in pairs]
    MV = [_mm(Mrk[p], V_st[p], pa) for p in pairs]
    BT = [_mm(Bend_st[p], AU[p], pa, _TN) for p in pairs]
    KV = [_mm(Kend_st[p], V_st[p], pa, _TN) for p in pairs]
    Rhat = [R_st[p] + MM[p][:, :LANES] for p in pairs]
    Mmat = [eye * jnp.exp(cum_end[p]) + BT[p][:, :LANES] for p in pairs]

    RS = [_mm(jnp.concatenate([Rhat[p], Mmat[p]], axis=0), state_ref[p], ps) for p in pairs]
    for p in pairs:
        state_ref[p] = jnp.where(same_head, RS[p][P2:] + BT[p][:, LANES:] + KV[p], 0.0)
    Y_st = [RS[p][:P2] + MM[p][:, LANES:] + MV[p] for p in pairs]
    y = [Y_st[p][:CH] + Y_st[p][CH:] for p in pairs]

    sums = [_sum01_right(jnp.concatenate([y[p], r[p] * k[p] * rk_ref[:, sls[p]]], axis=0), ones_bd)
            for p in pairs]
    yc = [y[p] - sums[p][:CH] * (1.0 / N) for p in pairs]
    var = [_sum01_right(yc[p] * yc[p], ones_bd) * (1.0 / N) for p in pairs]
    for p in pairs:
        yn = yc[p] * lax.rsqrt(var[p] + RWKV_GN_EPS) * lnw_ref[:, sls[p]] + lnb_ref[:, sls[p]]
        o_ref[:, sls[p]] = ((yn + sums[p][CH:] * v[p]) * g_ref[:, sls[p]]).astype(o_ref.dtype)


def rwkv_scan(r, lw, k, v, kk, a, g, r_k, ln_w, ln_b, B, S):
    T, C = r.shape
    CH = RWKV_CHUNK
    nc = S // CH
    blk = pl.BlockSpec((CH, C), lambda b, c: (b * nc + c, 0))
    par = pl.BlockSpec((1, C), lambda b, c: (0, 0))
    return pl.pallas_call(
        _rwkv_scan_kernel,
        grid=(B, nc),
        in_specs=[blk] * 7 + [par] * 3,
        out_specs=blk,
        out_shape=jax.ShapeDtypeStruct((T, C), BF16),
        scratch_shapes=[pltpu.VMEM((C // LANES, LANES, LANES), F32)],
        compiler_params=_cparams("parallel", "arbitrary"),
        name="rwkv_scan",
    )(r, lw, k, v, kk, a, g, r_k.reshape(1, C), ln_w.reshape(1, C), ln_b.reshape(1, C))


def _rope_pairs(x, cos2, sin2):
    lane = lax.broadcasted_iota(jnp.int32, (1, LANES), 1)
    first_half = (lane % MLA_ROPE_DIM) < (MLA_ROPE_DIM // 2)
    partner = jnp.where(first_half, pltpu.roll(x, LANES - MLA_ROPE_DIM // 2, 1),
                        pltpu.roll(x, MLA_ROPE_DIM // 2, 1))
    return x * cos2 + partner * sin2


def _mla_cproj_kernel(h_ref, w_ref, cos_ref, sin_ref, c_ref, kr_ref):
    acc = _dot(h_ref[...], w_ref[...])
    NC = MLA_Q_LORA + MLA_KV_LORA
    c_ref[...] = acc[:, :NC]
    kr_ref[...] = _rope_pairs(acc[:, NC:], cos_ref[...], sin_ref[...]).astype(kr_ref.dtype)


def mla_cproj(h16, w_in, cos2, sin2, S, tm=512):
    T, D = h16.shape
    NC = MLA_Q_LORA + MLA_KV_LORA
    w = jnp.concatenate([w_in, w_in[:, NC:]], axis=1).astype(BF16)
    ns = S // tm
    return pl.pallas_call(
        _mla_cproj_kernel,
        grid=(T // tm,),
        in_specs=[pl.BlockSpec((tm, D), lambda i: (i, 0)),
                  pl.BlockSpec((D, NC + LANES), lambda i: (0, 0)),
                  pl.BlockSpec((tm, LANES), lambda i: (i % ns, 0)),
                  pl.BlockSpec((tm, LANES), lambda i: (i % ns, 0))],
        out_specs=[pl.BlockSpec((tm, NC), lambda i: (i, 0)),
                   pl.BlockSpec((tm, LANES), lambda i: (i, 0))],
        out_shape=[jax.ShapeDtypeStruct((T, NC), F32), jax.ShapeDtypeStruct((T, LANES), BF16)],
        compiler_params=_cparams("parallel"),
        name="mla_cproj",
    )(h16, w, cos2, sin2)


def _rms_up_kernel(rope_from, out_scale, c_ref, g_ref, w_ref, cos_ref, sin_ref, o_ref):
    x = c_ref[...]
    ms = jnp.mean(x * x, axis=-1, keepdims=True)
    cn = (x * lax.rsqrt(ms + RMS_EPS) * g_ref[...]).astype(BF16)
    N = w_ref.shape[1]
    plain = N if rope_from is None else rope_from
    TN = 1024
    for n0 in range(0, plain, TN):
        acc = _dot(cn, w_ref[:, n0:n0 + TN])
        if out_scale != 1.0:
            acc = acc * out_scale
        o_ref[:, n0:n0 + TN] = acc.astype(o_ref.dtype)
    if rope_from is not None:
        cos2, sin2 = cos_ref[...], sin_ref[...]
        acc = _dot(cn, w_ref[:, rope_from:]) * out_scale
        for gidx in range(acc.shape[1] // LANES):
            sl = slice(gidx * LANES, (gidx + 1) * LANES)
            o_ref[:, rope_from + gidx * LANES:rope_from + (gidx + 1) * LANES] = (
                _rope_pairs(acc[:, sl], cos2, sin2).astype(o_ref.dtype))


def rms_up_proj(c, col_block, gain, w, cos2, sin2, S, rope_from, out_scale=1.0, tm=512):
    T = c.shape[0]
    K, N = w.shape
    ns = S // tm
    return pl.pallas_call(
        functools.partial(_rms_up_kernel, rope_from, out_scale),
        grid=(T // tm,),
        in_specs=[pl.BlockSpec((tm, K), lambda i: (i, col_block)),
                  pl.BlockSpec((1, K), lambda i: (0, 0)),
                  pl.BlockSpec((K, N), lambda i: (0, 0)),
                  pl.BlockSpec((tm, LANES), lambda i: (i % ns, 0)),
                  pl.BlockSpec((tm, LANES), lambda i: (i % ns, 0))],
        out_specs=pl.BlockSpec((tm, N), lambda i: (i, 0)),
        out_shape=jax.ShapeDtypeStruct((T, N), BF16),
        compiler_params=_cparams("parallel"),
        name="rms_up_proj",
    )(c, gain.reshape(1, K), w, cos2, sin2)


MLA_GROUP = 4
MLA_SCORE_SCALE = (MLA_NOPE_DIM + MLA_ROPE_DIM) ** -0.5 * math.log2(math.e)


def _mla_attn_kernel(qn_ref, qr_ref, kv_ref, kr_ref, o_ref, vt_ref):
    qi = pl.program_id(2)
    TQ = qn_ref.shape[0]
    S = kv_ref.shape[0]
    G = MLA_GROUP
    lane = lax.broadcasted_iota(jnp.int32, (1, LANES), 1)

    @pl.when(qi == 0)
    def _():
        for j in range(G):
            for n in range(S // TQ):
                rows = slice(n * TQ, (n + 1) * TQ)
                v_t = kv_ref[rows, (2 * j + 1) * LANES:(2 * j + 2) * LANES]
                vt_ref[j, :, rows] = v_t.astype(F32).T.astype(BF16)

    key = lax.broadcasted_iota(jnp.int32, (TQ, TQ), 0)
    qry = lax.broadcasted_iota(jnp.int32, (TQ, TQ), 1)
    qts = []
    for j in range(G):
        own = (lane // MLA_ROPE_DIM) == (j % 2)
        qr = qr_ref[:, (j // 2) * LANES:(j // 2 + 1) * LANES]
        qc = jnp.concatenate([qn_ref[:, j * LANES:(j + 1) * LANES],
                              jnp.where(own, qr, jnp.zeros_like(qr))], axis=1)
        qts.append(qc.astype(F32).T.astype(BF16))

    def scores(j, off):
        kc = jnp.concatenate([kv_ref[pl.ds(off, TQ), 2 * j * LANES:(2 * j + 1) * LANES],
                              kr_ref[pl.ds(off, TQ), :]], axis=1)
        return _dot(kc, qts[j])

    def softmax_step(t, m_run, l_run, diag):
        if diag:
            t = jnp.where(key <= qry, t, NEG_INF)
        m_new = jnp.maximum(m_run, jnp.max(_col_groups(t, jnp.max), axis=0, keepdims=True))
        alpha = jnp.exp2(m_run - m_new)
        pr = jnp.exp2(t - m_new)
        l_new = alpha * l_run + _col_groups(pr, jnp.sum)
        return m_new, l_new, alpha, pr.astype(BF16)

    def weighted_values(j, off, acc, alpha, pr):
        return alpha * acc + _dot(vt_ref[j, :, pl.ds(off, TQ)], pr)

    def body(n, carry):
        off = pl.multiple_of(n * TQ, TQ)
        out = []
        for j in range(G):
            m_run, l_run, acc = carry[j]
            m_new, l_new, alpha, pr = softmax_step(scores(j, off), m_run, l_run, False)
            out.append((m_new, l_new, weighted_values(j, off, acc, alpha, pr)))
        return tuple(out)

    carry = tuple((jnp.full((1, TQ), NEG_INF, F32), jnp.zeros((8, TQ), F32),
                   jnp.zeros((LANES, TQ), F32)) for j in range(G))
    carry = lax.fori_loop(0, qi, body, carry)
    off_d = pl.multiple_of(qi * TQ, TQ)
    for j in range(G):
        m_run, l_run, acc = carry[j]
        m_new, l_new, alpha, pr = softmax_step(scores(j, off_d), m_run, l_run, True)
        acc = weighted_values(j, off_d, acc, alpha, pr)
        out_t = acc / jnp.sum(l_new, axis=0, keepdims=True)
        o_ref[:, j * LANES:(j + 1) * LANES] = out_t.T.astype(o_ref.dtype)


def mla_attention(q, kv, kr2, B, S, tq=512):
    T = B * S
    H, G = MLA_HEADS, MLA_GROUP
    nq = S // tq
    n_nope = H // G
    return pl.pallas_call(
        _mla_attn_kernel,
        grid=(B, H // G, nq),
        in_specs=[pl.BlockSpec((tq, G * LANES), lambda b, g, i: (b * nq + i, g)),
                  pl.BlockSpec((tq, G // 2 * LANES), lambda b, g, i: (b * nq + i, 2 * n_nope + g)),
                  pl.BlockSpec((S, 2 * G * LANES), lambda b, g, i: (b, g)),
                  pl.BlockSpec((S, LANES), lambda b, g, i: (b, 0))],
        out_specs=pl.BlockSpec((tq, G * LANES), lambda b, g, i: (b * nq + i, g)),
        out_shape=jax.ShapeDtypeStruct((T, H * MLA_V_DIM), BF16),
        scratch_shapes=[pltpu.VMEM((G, LANES, S), BF16)],
        compiler_params=_cparams("parallel", "parallel", "arbitrary"),
        name="mla_attention",
    )(q, q, kv, kr2)


def _rope_tables(S):
    half = MLA_ROPE_DIM // 2
    inv = ROPE_THETA ** (-jnp.arange(0, MLA_ROPE_DIM, 2, dtype=F32) / MLA_ROPE_DIM)
    ang = jnp.arange(S, dtype=F32)[:, None] * inv[None, :]
    cos, sin = jnp.cos(ang), jnp.sin(ang)
    reps = LANES // MLA_ROPE_DIM
    cos2 = jnp.tile(jnp.concatenate([cos, cos], axis=1), (1, reps))
    sin2 = jnp.tile(jnp.concatenate([-sin, sin], axis=1), (1, reps))
    assert cos2.shape == (S, LANES) and half * 2 * reps == LANES
    return cos2, sin2


def kernel(x, ev_w_in, ev_w_out, rw_mu, rw_w0, rw_w2, rw_a0, rw_a2, rw_g2, rw_k_k, rw_k_a, rw_r_k,
           rw_ln_w, rw_ln_b, rw_v0, rw_v1, rw_v2, od_w_in, od_q_norm, od_kv_norm, od_w_uq, od_w_ukv,
           od_w_out, ln_mix_g, ln_mix_b, ln_ffn_g, ln_ffn_b, moe_w_r, moe_b_r, moe_w1, moe_b1,
           moe_w2, moe_b2):
    B, S, D = x.shape
    T = B * S
    h32 = x.reshape(T, D)
    h16 = h32.astype(BF16)
    cos2, sin2 = _rope_tables(S)
    n_main = 3 * MOBA_DIM + 3 * RWKV_DIM
    lora_pad = -(-RWKV_LORA // LANES) * LANES
    qd = MLA_NOPE_DIM + MLA_ROPE_DIM
    perm = np.concatenate([
        (np.arange(MLA_HEADS)[:, None] * qd + np.arange(MLA_NOPE_DIM)[None, :]).reshape(-1),
        (np.arange(MLA_HEADS)[:, None] * qd + MLA_NOPE_DIM + np.arange(MLA_ROPE_DIM)[None, :]).reshape(-1)])
    v_first = None
    for layer in range(DEPTH):
        j = layer // 2
        if layer % 2 == 0:
            w_in = ev_w_in[j]
            w_main = w_in[:, :n_main].astype(BF16)
            w_lora = jnp.zeros((D, lora_pad), BF16).at[:, :RWKV_LORA].set(w_in[:, n_main:].astype(BF16))
            proj = matmul(h16, w_main, F32)
            plora = matmul(h16, w_lora, F32)
            a_out = moba_attention(proj, B, S)
            v_lora = None if j == 0 else (rw_v0[j - 1], rw_v1[j - 1], rw_v2[j - 1])
            r, lw, k, v, kk, a, g = rwkv_prep(proj, plora, S, rw_mu[j], rw_w0[j], rw_w2[j], rw_a0[j],
                                             rw_a2[j], rw_g2[j], rw_k_k[j], rw_k_a[j], v_first, v_lora)
            if j == 0:
                v_first = v
            b_out = rwkv_scan(r, lw, k, v, kk, a, g, rw_r_k[j], rw_ln_w[j], rw_ln_b[j], B, S)
            w_out = ev_w_out[j].astype(BF16)
            h32, h16 = proj_residual_ln([a_out, b_out], [w_out[:MOBA_DIM], w_out[MOBA_DIM:]], h32,
                                        ln_mix_g[layer], ln_mix_b[layer])
        else:
            c, kr2 = mla_cproj(h16, od_w_in[j], cos2, sin2, S)
            q = rms_up_proj(c, 0, od_q_norm[j], od_w_uq[j][:, perm].astype(BF16), cos2, sin2, S,
                            rope_from=MLA_HEADS * MLA_NOPE_DIM, out_scale=MLA_SCORE_SCALE)
            kv = rms_up_proj(c, 1, od_kv_norm[j], od_w_ukv[j].astype(BF16), cos2, sin2, S,
                             rope_from=None)
            o = mla_attention(q, kv, kr2, B, S)
            h32, h16 = proj_residual_ln([o], [od_w_out[j].astype(BF16)], h32,
                                        ln_mix_g[layer], ln_mix_b[layer])
        h32, h16 = moe_layer(h32, h16, moe_w_r[layer], moe_b_r[layer], moe_w1, moe_b1[layer], moe_w2,
                             moe_b2[layer], ln_ffn_g[layer], ln_ffn_b[layer], layer)
    return h32.reshape(B, S, D)
```

```python
import functools
import math

import jax
import jax.numpy as jnp
import numpy as np
from jax import lax
from jax.experimental import pallas as pl
from jax.experimental.pallas import tpu as pltpu

F32 = jnp.float32
BF16 = jnp.bfloat16
HIGHEST = lax.Precision.HIGHEST

DEPTH = 4
MOBA_HEADS = 16
MOBA_HEAD_DIM = 64
MOBA_DIM = MOBA_HEADS * MOBA_HEAD_DIM
MOBA_BLOCK = 256
MOBA_TOPK = 3
RWKV_HEADS = 16
RWKV_HEAD_DIM = 64
RWKV_DIM = RWKV_HEADS * RWKV_HEAD_DIM
RWKV_DECAY_LORA = 64
RWKV_A_LORA = 64
RWKV_G_LORA = 160
RWKV_LORA = RWKV_DECAY_LORA + RWKV_A_LORA + RWKV_G_LORA
RWKV_GN_EPS = 64e-5
RWKV_CHUNK = 64
MLA_HEADS = 16
MLA_Q_LORA = 512
MLA_KV_LORA = 512
MLA_NOPE_DIM = 128
MLA_ROPE_DIM = 64
MLA_V_DIM = 128
ROPE_THETA = 10000.0
N_EXPERTS = 32
TOP_K = 4
D_EXPERT = 1024
SWIGLU_ALPHA = 1.702
SWIGLU_LIMIT = 7.0
MOE_ROWS = 256
DEEPNORM_ALPHA = (2 * DEPTH) ** 0.25
LN_EPS = 1e-5
RMS_EPS = 1e-6
NEG_INF = -1e30

LANES = 128
VMEM_LIMIT = 56 * 1024 * 1024


def _cparams(*sem):
    return pltpu.CompilerParams(dimension_semantics=sem, vmem_limit_bytes=VMEM_LIMIT)


def _dot(a, b, precision=None):
    return jnp.dot(a, b, preferred_element_type=F32, precision=precision)


def _mm_kernel(a_ref, w_ref, o_ref):
    o_ref[...] = _dot(a_ref[...], w_ref[...]).astype(o_ref.dtype)


def matmul(a, w, out_dtype, tm=1024, tn=512):
    M, K = a.shape
    N = w.shape[1]
    tm, tn = min(tm, M), min(tn, N)
    assert M % tm == 0 and N % tn == 0
    return pl.pallas_call(
        _mm_kernel,
        grid=(M // tm, N // tn),
        in_specs=[pl.BlockSpec((tm, K), lambda i, j: (i, 0)),
                  pl.BlockSpec((K, tn), lambda i, j: (0, j))],
        out_specs=pl.BlockSpec((tm, tn), lambda i, j: (i, j)),
        out_shape=jax.ShapeDtypeStruct((M, N), out_dtype),
        compiler_params=_cparams("parallel", "parallel"),
        name="matmul",
    )(a, w)


def _layer_norm_rows(z, g, b):
    mu = jnp.mean(z, axis=-1, keepdims=True)
    zc = z - mu
    var = jnp.mean(zc * zc, axis=-1, keepdims=True)
    return zc * lax.rsqrt(var + LN_EPS) * g + b


def _proj_ln_kernel(n_in, *refs):
    a_refs = refs[:n_in]
    w_refs = refs[n_in:2 * n_in]
    h_ref, g_ref, b_ref, o32_ref, o16_ref = refs[2 * n_in:]
    acc = DEEPNORM_ALPHA * h_ref[...]
    for a_ref, w_ref in zip(a_refs, w_refs):
        acc = acc + _dot(a_ref[...], w_ref[...])
    y = _layer_norm_rows(acc, g_ref[...], b_ref[...])
    o32_ref[...] = y
    o16_ref[...] = y.astype(BF16)


def proj_residual_ln(a_list, w_list, h, g, b, tm=256):
    M, D = h.shape
    n_in = len(a_list)
    in_specs = ([pl.BlockSpec((tm, a.shape[1]), lambda i: (i, 0)) for a in a_list]
                + [pl.BlockSpec(w.shape, lambda i: (0, 0)) for w in w_list]
                + [pl.BlockSpec((tm, D), lambda i: (i, 0)),
                   pl.BlockSpec((1, D), lambda i: (0, 0)),
                   pl.BlockSpec((1, D), lambda i: (0, 0))])
    return pl.pallas_call(
        functools.partial(_proj_ln_kernel, n_in),
        grid=(M // tm,),
        in_specs=in_specs,
        out_specs=[pl.BlockSpec((tm, D), lambda i: (i, 0)),
                   pl.BlockSpec((tm, D), lambda i: (i, 0))],
        out_shape=[jax.ShapeDtypeStruct((M, D), F32), jax.ShapeDtypeStruct((M, D), BF16)],
        compiler_params=_cparams("parallel"),
        name="proj_residual_ln",
    )(*a_list, *w_list, h, g.reshape(1, D), b.reshape(1, D))


def _combine_ln_kernel(y_ref, gate_ref, h_ref, g_ref, b_ref, *rest):
    o32_ref, o16_ref = rest[-2:]
    acc = DEEPNORM_ALPHA * h_ref[...]
    gates = gate_ref[...]
    for k in range(TOP_K):
        acc = acc + y_ref[k].astype(F32) * gates[:, k:k + 1]
    y = _layer_norm_rows(acc, g_ref[...], b_ref[...])
    o32_ref[...] = y
    o16_ref[...] = y.astype(BF16)


def combine_residual_ln(y4, gates, h, g, b, part, n_parts, prev=None, tm=256):
    M, D = h.shape
    Mp = M // n_parts
    off = part * (Mp // tm)
    in_specs = [pl.BlockSpec((TOP_K, tm, D), lambda i: (0, i, 0)),
                pl.BlockSpec((tm, TOP_K), lambda i: (i + off, 0)),
                pl.BlockSpec((tm, D), lambda i: (i + off, 0)),
                pl.BlockSpec((1, D), lambda i: (0, 0)),
                pl.BlockSpec((1, D), lambda i: (0, 0))]
    args = [y4, gates, h, g.reshape(1, D), b.reshape(1, D)]
    aliases = {}
    if prev is not None:
        in_specs += [pl.BlockSpec(memory_space=pl.ANY)] * 2
        aliases = {len(args): 0, len(args) + 1: 1}
        args += list(prev)
    return pl.pallas_call(
        _combine_ln_kernel,
        grid=(Mp // tm,),
        in_specs=in_specs,
        out_specs=[pl.BlockSpec((tm, D), lambda i: (i + off, 0)),
                   pl.BlockSpec((tm, D), lambda i: (i + off, 0))],
        out_shape=[jax.ShapeDtypeStruct((M, D), F32), jax.ShapeDtypeStruct((M, D), BF16)],
        input_output_aliases=aliases,
        compiler_params=_cparams("parallel"),
        name="combine_residual_ln",
    )(*args)


def _router_kernel(h_ref, w_ref, b_ref, gate_ref, idx_ref, cnt_ref, base_ref):
    i = pl.program_id(0)
    tm = h_ref.shape[0]

    @pl.when(i == 0)
    def _():
        base_ref[...] = jnp.zeros_like(base_ref)

    lane = lax.broadcasted_iota(jnp.int32, (tm, LANES), 1)
    logits = _mm(h_ref[...], w_ref[...], 3) + b_ref[...]
    val = jnp.where(lane < N_EXPERTS, logits, -jnp.inf)
    tops, idxs = [], []
    member = jnp.zeros((tm, LANES), F32)
    lane_f = lane.astype(F32)
    for _ in range(TOP_K):
        mx = jnp.max(val, axis=-1, keepdims=True)
        ix = jnp.min(jnp.where(val == mx, lane_f, float(LANES)), axis=-1, keepdims=True).astype(jnp.int32)
        hit = lane == ix
        member = member + jnp.where(hit, 1.0, 0.0)
        val = jnp.where(hit, -jnp.inf, val)
        tops.append(mx)
        idxs.append(ix)
    ex = [jnp.exp(t - tops[0]) for t in tops]
    den = ex[0] + ex[1] + ex[2] + ex[3]
    ti = lax.broadcasted_iota(jnp.int32, (tm, tm), 0)
    si = lax.broadcasted_iota(jnp.int32, (tm, tm), 1)
    csum = base_ref[...] + _dot((ti >= si).astype(BF16), member.astype(BF16))
    gate_out = jnp.zeros((tm, LANES), F32)
    idx_out = jnp.zeros((tm, LANES), jnp.int32)
    for k in range(TOP_K):
        rank = jnp.sum(jnp.where(lane == idxs[k], csum - 1.0, 0.0), axis=-1, keepdims=True)
        gate_out = jnp.where(lane == k, ex[k] / den, gate_out)
        idx_out = jnp.where(lane == k, idxs[k], idx_out)
        idx_out = jnp.where(lane == TOP_K + k, rank.astype(jnp.int32), idx_out)
    gate_ref[...] = gate_out
    idx_ref[...] = idx_out
    base_ref[...] = csum[tm - 1:tm, :]
    cnt_ref[...] = csum[tm - 1:tm, :].astype(jnp.int32)


def route_tokens(h, w_r, b_r, tm=512):
    M, D = h.shape
    tm = min(tm, M)
    w = jnp.zeros((D, LANES), F32).at[:, :N_EXPERTS].set(w_r)
    b = jnp.zeros((1, LANES), F32).at[0, :N_EXPERTS].set(b_r)
    gate, idx, cnt = pl.pallas_call(
        _router_kernel,
        grid=(M // tm,),
        in_specs=[pl.BlockSpec((tm, D), lambda i: (i, 0)),
                  pl.BlockSpec((D, LANES), lambda i: (0, 0)),
                  pl.BlockSpec((1, LANES), lambda i: (0, 0))],
        out_specs=[pl.BlockSpec((tm, LANES), lambda i: (i, 0)),
                   pl.BlockSpec((tm, LANES), lambda i: (i, 0)),
                   pl.BlockSpec((1, LANES), lambda i: (0, 0))],
        out_shape=[jax.ShapeDtypeStruct((M, LANES), F32), jax.ShapeDtypeStruct((M, LANES), jnp.int32),
                   jax.ShapeDtypeStruct((1, LANES), jnp.int32)],
        scratch_shapes=[pltpu.VMEM((1, LANES), F32)],
        compiler_params=_cparams("arbitrary"),
        name="route_tokens",
    )(h, w, b)
    return gate[:, :TOP_K], idx[:, :TOP_K], idx[:, TOP_K:2 * TOP_K], cnt[0, :N_EXPERTS]


MOE_W_CHUNKS = 4
MOE_COMBINE_PARTS = 2


def _expert_kernel(cb_ref, ce_ref, le_ref, lc_ref, fl_ref, x_ref, w1c_ref, w2c_ref, b1_ref, b2_ref,
                   o_ref, w1a_ref, w2a_ref, w1b_ref, w2b_ref):
    s = pl.program_id(0)
    fl = fl_ref[s]
    comp = (fl & 1) == 1
    load = (fl & 2) == 2
    par = (fl & 4) == 4
    c = lc_ref[s]
    C1 = w1c_ref.shape[0]
    C2 = w2c_ref.shape[0]

    def cast_chunk(w1_dst, w2_dst):
        w1_dst[pl.ds(pl.multiple_of(c * C1, C1), C1), :] = w1c_ref[...].astype(BF16)
        w2_dst[pl.ds(pl.multiple_of(c * C2, C2), C2), :] = w2c_ref[...].astype(BF16)

    def ffn(w1_src, w2_src):
        hgu = _dot(x_ref[...], w1_src[...]) + b1_ref[0]
        gate = jnp.minimum(hgu[:, :D_EXPERT], SWIGLU_LIMIT)
        up = jnp.clip(hgu[:, D_EXPERT:], -SWIGLU_LIMIT, SWIGLU_LIMIT)
        act = gate * jax.nn.sigmoid(SWIGLU_ALPHA * gate) * (up + 1.0)
        o_ref[...] = (_dot(act.astype(BF16), w2_src[...]) + b2_ref[0]).astype(o_ref.dtype)

    bufs = ((w1a_ref, w2a_ref), (w1b_ref, w2b_ref))
    for p in (0, 1):
        use, fill = bufs[p], bufs[1 - p]
        in_phase = par == (p == 1)

        @pl.when(comp & in_phase)
        def _():
            cast_chunk(*fill)
            ffn(*use)

        @pl.when(load & jnp.logical_not(comp) & in_phase)
        def _():
            cast_chunk(*fill)

    @pl.when((fl & 8) == 8)
    def _():
        o_ref[...] = jnp.zeros_like(o_ref)


def _expert_schedule(nblk, n_blocks):
    E, NC = N_EXPERTS, MOE_W_CHUNKS
    ph_ids = jnp.arange(E + 1, dtype=jnp.int32)
    prev_n = jnp.concatenate([jnp.zeros((1,), jnp.int32), nblk.astype(jnp.int32)])
    plen = jnp.where(ph_ids == 0, NC, jnp.where(ph_ids == E, prev_n, jnp.maximum(prev_n, NC)))
    pend = jnp.cumsum(plen)
    pstart = pend - plen
    n_steps = n_blocks + E * NC
    s = jnp.arange(n_steps, dtype=jnp.int32)
    ph = jnp.minimum(jnp.sum(pend[None, :] <= s[:, None], axis=1), E).astype(jnp.int32)
    t = s - pstart[ph]
    comp = (ph >= 1) & (t < prev_n[ph])
    load = (ph < E) & (t < NC)
    n_used = jnp.sum(nblk).astype(jnp.int32)
    tail_blk = n_used + (s - pend[E])
    fill = (s >= pend[E]) & (tail_blk < n_blocks)
    comp_blk = jnp.where(s >= pend[E], jnp.minimum(tail_blk, n_blocks - 1),
                         jnp.maximum(jnp.cumsum(comp.astype(jnp.int32)) - 1, 0))
    comp_e = jnp.clip(ph - 1, 0, E - 1)
    load_e = jnp.minimum(ph, E - 1)
    load_c = jnp.where(ph >= E, NC - 1, jnp.minimum(t, NC - 1))
    flags = (comp.astype(jnp.int32) + 2 * load.astype(jnp.int32)
             + 4 * ((ph - 1) % 2 == 1).astype(jnp.int32) + 8 * fill.astype(jnp.int32))
    return comp_blk.astype(jnp.int32), comp_e.astype(jnp.int32), load_e, load_c.astype(jnp.int32), flags


def expert_ffn(xbuf, nblk, w1, b1, w2, b2, layer):
    rows, D = xbuf.shape
    R = MOE_ROWS
    NC = MOE_W_CHUNKS
    n_blocks = rows // R
    H2 = 2 * D_EXPERT
    C1, C2 = D // NC, D_EXPERT // NC
    sched = _expert_schedule(nblk, n_blocks)
    n_steps = sched[0].shape[0]
    grid_spec = pltpu.PrefetchScalarGridSpec(
        num_scalar_prefetch=5,
        grid=(n_steps,),
        in_specs=[pl.BlockSpec((R, D), lambda s, cb, ce, le, lc, fl: (cb[s], 0)),
                  pl.BlockSpec((None, None, C1, H2), lambda s, cb, ce, le, lc, fl: (layer, le[s], lc[s], 0)),
                  pl.BlockSpec((None, None, C2, D), lambda s, cb, ce, le, lc, fl: (layer, le[s], lc[s], 0)),
                  pl.BlockSpec((1, 1, H2), lambda s, cb, ce, le, lc, fl: (ce[s], 0, 0)),
                  pl.BlockSpec((1, 1, D), lambda s, cb, ce, le, lc, fl: (ce[s], 0, 0))],
        out_specs=pl.BlockSpec((R, D), lambda s, cb, ce, le, lc, fl: (cb[s], 0)),
        scratch_shapes=[pltpu.VMEM((D, H2), BF16), pltpu.VMEM((D_EXPERT, D), BF16),
                        pltpu.VMEM((D, H2), BF16), pltpu.VMEM((D_EXPERT, D), BF16)],
    )
    return pl.pallas_call(
        _expert_kernel,
        grid_spec=grid_spec,
        out_shape=jax.ShapeDtypeStruct((rows, D), BF16),
        compiler_params=_cparams("arbitrary"),
        name="expert_ffn",
    )(*sched, xbuf, w1, w2, b1.reshape(N_EXPERTS, 1, -1), b2.reshape(N_EXPERTS, 1, -1))


def moe_layer(h32, h16, w_r, b_r, w1, b1, w2, b2, ln_g, ln_b, layer):
    T, D = h32.shape
    R = MOE_ROWS
    M = T * TOP_K
    gates, top_idx, rank, counts = route_tokens(h32, w_r, b_r)
    padded = (counts + R - 1) // R * R
    pad_end = jnp.cumsum(padded)
    pad_start = pad_end - padded
    pos = rank + pad_start[top_idx]
    n_blocks = (M + N_EXPERTS * (R - 1) + R - 1) // R
    rows = n_blocks * R
    src_tok = (jnp.arange(rows, dtype=jnp.int32) % T).at[pos.reshape(M)].set(
        jnp.arange(M, dtype=jnp.int32) // TOP_K, unique_indices=True)
    xbuf = h16[src_tok]
    ybuf = expert_ffn(xbuf, padded // R, w1, b1, w2, b2, layer)
    out = None
    Tp = T // MOE_COMBINE_PARTS
    for part in range(MOE_COMBINE_PARTS):
        y4 = ybuf[pos[part * Tp:(part + 1) * Tp].T]
        out = combine_residual_ln(y4, gates, h32, ln_g, ln_b, part, MOE_COMBINE_PARTS, prev=out)
    return out


MOBA_GROUP = 4
MOBA_GATE_ROWS = 8


def _col_groups(x, op):
    rows, cols = x.shape
    return op(x.reshape(rows // 8, 8, cols), axis=0)


def _moba_kernel(q_ref, k_ref, v_ref, o_ref, kmean_ref, vt_ref, bias_ref):
    pg = pl.program_id(1)
    cur = pl.program_id(2)
    BLK = MOBA_BLOCK
    GP = MOBA_GROUP
    NR = MOBA_GATE_ROWS
    nb = k_ref.shape[0] // BLK
    assert nb <= NR
    lane = lax.broadcasted_iota(jnp.int32, (1, LANES), 1)
    LOG2E = math.log2(math.e)
    c2 = MOBA_HEAD_DIM ** -0.5 * LOG2E

    @pl.when(cur == 0)
    def _():
        kmean_ref[...] = jnp.zeros_like(kmean_ref)
        for g in range(GP):
            for n in range(nb):
                rows = slice(n * BLK, (n + 1) * BLK)
                kmean_ref[g, n:n + 1, :] = jnp.mean(k_ref[rows, g * LANES:(g + 1) * LANES], axis=0,
                                                    keepdims=True)
                vt_ref[g, :, rows] = v_ref[rows, g * LANES:(g + 1) * LANES].T.astype(BF16)

    Q2 = 2 * BLK
    key = lax.broadcasted_iota(jnp.int32, (BLK, Q2), 0)
    qry = lax.broadcasted_iota(jnp.int32, (BLK, Q2), 1) % BLK
    krow = lax.broadcasted_iota(jnp.int32, (BLK, LANES), 0)
    klane = lax.broadcasted_iota(jnp.int32, (BLK, LANES), 1)
    k_extra = jnp.where(klane < 2, krow, 0).astype(F32).astype(BF16)
    srow = lax.broadcasted_iota(jnp.int32, (LANES, BLK), 0)
    blk_row = lax.broadcasted_iota(jnp.int32, (NR, BLK), 0)

    qts = []
    for g in range(GP):
        q = q_ref[:, g * LANES:(g + 1) * LANES]
        halves, biases = [], []
        for hh in range(2):
            own = (lane // MOBA_HEAD_DIM) == hh
            qm_t = jnp.where(own, q, 0.0).T
            head_idx = 2 * (pg * GP + g) + hh + 1
            sl2 = jnp.exp(jnp.full((1, 1), head_idx, jnp.int32).astype(F32)
                          * (-8.0 * math.log(2.0) / MOBA_HEADS)) * LOG2E
            s_hi = sl2.astype(BF16).astype(F32)
            s_lo = (sl2 - s_hi).astype(BF16).astype(F32)
            q_extra = jnp.where(srow == 0, s_hi, jnp.where(srow == 1, s_lo, 0.0))
            halves.append(jnp.concatenate([qm_t * c2, q_extra], axis=0).astype(BF16))
            gate = _dot(kmean_ref[g], qm_t, precision=HIGHEST)
            rank = jnp.zeros((NR, BLK), jnp.int32)
            for m in range(nb):
                gm = gate[m:m + 1, :]
                ahead = (gm > gate) | ((gm == gate) & (m < blk_row))
                rank = rank + jnp.where(ahead, 1, 0) * (m < cur).astype(jnp.int32)
            sel = (rank < MOBA_TOPK) & (blk_row < cur)
            biases.append(jnp.where(sel, -sl2 * ((cur - blk_row) * BLK).astype(F32), NEG_INF))
        qts.append(jnp.concatenate(halves, axis=1))
        bias_ref[g] = jnp.concatenate(biases, axis=1)

    def scores(g, off):
        kc = jnp.concatenate([k_ref[pl.ds(off, BLK), g * LANES:(g + 1) * LANES].astype(BF16), k_extra],
                             axis=1)
        return _dot(kc, qts[g])

    def softmax_step(t, m_run, l_run, bias):
        if bias is None:
            t = jnp.where(key <= qry, t, NEG_INF)
            m_new = jnp.maximum(m_run, jnp.max(_col_groups(t, jnp.max), axis=0, keepdims=True))
            shift = m_new
        else:
            m_new = jnp.maximum(m_run, jnp.max(_col_groups(t, jnp.max), axis=0, keepdims=True) + bias)
            shift = m_new - bias
        alpha = jnp.exp2(m_run - m_new)
        pr = jnp.exp2(t - shift)
        l_new = alpha * l_run + _col_groups(pr, jnp.sum)
        return m_new, l_new, alpha, pr.astype(BF16)

    def block(g, n, carry, bias):
        m_run, l_run, acc = carry
        off = pl.multiple_of(n * BLK, BLK)
        m_new, l_new, alpha, pr = softmax_step(scores(g, off), m_run, l_run, bias)
        return m_new, l_new, alpha * acc + _dot(vt_ref[g, :, pl.ds(off, BLK)], pr)

    init = (jnp.full((1, Q2), NEG_INF, F32), jnp.zeros((8, Q2), F32), jnp.zeros((LANES, Q2), F32))
    carry = tuple(block(g, cur, init, None) for g in range(GP))

    def body(n, carry):
        return tuple(block(g, n, carry[g], bias_ref[g, pl.ds(n, 1), :]) for g in range(GP))

    carry = lax.fori_loop(0, cur, body, carry)
    row128 = lax.broadcasted_iota(jnp.int32, (LANES, 1), 0)
    for g in range(GP):
        m_run, l_run, acc = carry[g]
        out = acc / jnp.sum(l_run, axis=0, keepdims=True)
        out_t = jnp.where(row128 < MOBA_HEAD_DIM, out[:, :BLK], out[:, BLK:])
        o_ref[:, g * LANES:(g + 1) * LANES] = out_t.T.astype(o_ref.dtype)


def moba_attention(proj, B, S):
    T = B * S
    BLK = MOBA_BLOCK
    GP = MOBA_GROUP
    W = GP * LANES
    n_grp = MOBA_DIM // W
    nq = S // BLK
    return pl.pallas_call(
        _moba_kernel,
        grid=(B, n_grp, nq),
        in_specs=[pl.BlockSpec((BLK, W), lambda b, p, c: (b * nq + c, p)),
                  pl.BlockSpec((S, W), lambda b, p, c: (b, n_grp + p)),
                  pl.BlockSpec((S, W), lambda b, p, c: (b, 2 * n_grp + p))],
        out_specs=pl.BlockSpec((BLK, W), lambda b, p, c: (b * nq + c, p)),
        out_shape=jax.ShapeDtypeStruct((T, MOBA_DIM), BF16),
        scratch_shapes=[pltpu.VMEM((GP, MOBA_GATE_ROWS, LANES), F32),
                        pltpu.VMEM((GP, LANES, S), BF16),
                        pltpu.VMEM((GP, MOBA_GATE_ROWS, 2 * BLK), F32)],
        compiler_params=_cparams("parallel", "parallel", "arbitrary"),
        name="moba_attention",
    )(proj, proj, proj)


def _rwkv_prep_kernel(has_vres, S, *refs):
    if has_vres:
        (pm_ref, pl_ref, pm_prev_ref, pl_prev_ref, mu_m_ref, mu_l_ref, w0_ref, w2_ref, a0_ref, a2_ref,
         g2_ref, kk_ref, ka_ref, vfirst_ref, v0_ref, v1_ref, v2_ref,
         r_o, lw_o, k_o, v_o, kkn_o, a_o, g_o) = refs
    else:
        (pm_ref, pl_ref, pm_prev_ref, pl_prev_ref, mu_m_ref, mu_l_ref, w0_ref, w2_ref, a0_ref, a2_ref,
         g2_ref, kk_ref, ka_ref,
         r_o, lw_o, k_o, v_o, kkn_o, a_o, g_o) = refs
    i = pl.program_id(0)
    tm = pm_ref.shape[0]
    C = RWKV_DIM
    row = lax.broadcasted_iota(jnp.int32, (tm, 1), 0)
    seq_start = (i * tm) % S == 0

    def shifted(cur_ref, prev_ref, mu_ref):
        x = cur_ref[...]
        prev_row = jnp.where(seq_start, 0.0, prev_ref[7:8, :])
        xs = jnp.where(row == 0, prev_row, pltpu.roll(x, 1, 0))
        return x + (xs - x) * mu_ref[...]

    pm = shifted(pm_ref, pm_prev_ref, mu_m_ref)
    plo = shifted(pl_ref, pl_prev_ref, mu_l_ref)
    r = pm[:, :C]
    k = pm[:, C:2 * C]
    v = pm[:, 2 * C:]
    wd = plo[:, :RWKV_DECAY_LORA]
    ad = plo[:, RWKV_DECAY_LORA:RWKV_DECAY_LORA + RWKV_A_LORA]
    gd = plo[:, RWKV_DECAY_LORA + RWKV_A_LORA:RWKV_LORA]
    w = -jax.nn.softplus(-(w0_ref[...] + _dot(jnp.tanh(wd).astype(BF16), w2_ref[...]))) - 0.5
    a = jax.nn.sigmoid(a0_ref[...] + _dot(ad.astype(BF16), a2_ref[...]))
    g = _dot(jax.nn.sigmoid(gd).astype(BF16), g2_ref[...])
    if has_vres:
        lo = _dot(_dot(v.astype(BF16), v1_ref[...]).astype(BF16), v2_ref[...])
        v = v + (vfirst_ref[...] - v) * jax.nn.sigmoid(v0_ref[...] + lo)
    kk = k * kk_ref[...]
    hid_r = lax.broadcasted_iota(jnp.int32, (LANES, LANES), 0) // RWKV_HEAD_DIM
    hid_c = lax.broadcasted_iota(jnp.int32, (LANES, LANES), 1) // RWKV_HEAD_DIM
    ones_bd = (hid_r == hid_c).astype(F32)
    for pp in range(C // LANES):
        sl = slice(pp * LANES, (pp + 1) * LANES)
        kkp = kk[:, sl]
        ss = _dot(kkp * kkp, ones_bd, precision=HIGHEST)
        kkn_o[:, sl] = kkp * lax.rsqrt(jnp.maximum(ss, 1e-24))
    r_o[...] = r
    lw_o[...] = -jnp.exp(w)
    k_o[...] = k * (1.0 + (a - 1.0) * ka_ref[...])
    v_o[...] = v
    a_o[...] = a
    g_o[...] = g


def rwkv_prep(proj, plora, S, mu, w0, w2, a0, a2, g2, k_k, k_a, v_first, v_lora, tm=256):
    T = proj.shape[0]
    C = RWKV_DIM
    LP = plora.shape[1]
    has_vres = v_lora is not None
    mu_m = mu[:3 * C].reshape(1, 3 * C)
    mu_l = jnp.zeros((1, LP), F32).at[0, :RWKV_LORA].set(mu[3 * C:])
    row = lambda z: z.reshape(1, -1)
    full = lambda z: pl.BlockSpec(z.shape, lambda i: (0,) * z.ndim)
    args = [proj, plora, proj, plora, mu_m, mu_l, row(w0), w2.astype(BF16), row(a0), a2.astype(BF16),
            g2.astype(BF16), row(k_k), row(k_a)]
    in_specs = [pl.BlockSpec((tm, 3 * C), lambda i: (i, 1)),
                pl.BlockSpec((tm, LP), lambda i: (i, 0)),
                pl.BlockSpec((8, 3 * C), lambda i: (jnp.maximum(i * (tm // 8) - 1, 0), 1)),
                pl.BlockSpec((8, LP), lambda i: (jnp.maximum(i * (tm // 8) - 1, 0), 0))]
    in_specs += [full(z) for z in args[4:]]
    if has_vres:
        v0, v1, v2 = v_lora
        extra = [v_first, row(v0), v1.astype(BF16), v2.astype(BF16)]
        args += extra
        in_specs += [pl.BlockSpec((tm, C), lambda i: (i, 0))] + [full(z) for z in extra[1:]]
    out_spec = pl.BlockSpec((tm, C), lambda i: (i, 0))
    return pl.pallas_call(
        functools.partial(_rwkv_prep_kernel, has_vres, S),
        grid=(T // tm,),
        in_specs=in_specs,
        out_specs=[out_spec] * 7,
        out_shape=[jax.ShapeDtypeStruct((T, C), F32)] * 7,
        compiler_params=_cparams("parallel"),
        name="rwkv_prep",
    )(*args)


def _split3_bf16(x):
    x0 = x.astype(BF16)
    r1 = x - x0.astype(F32)
    x1 = r1.astype(BF16)
    x2 = (r1 - x1.astype(F32)).astype(BF16)
    return x0, x1, x2


def _sum01_left(m01, x):
    x0, x1, x2 = _split3_bf16(x)
    return _dot(m01, x0) + (_dot(m01, x1) + _dot(m01, x2))


def _sum01_right(x, m01):
    x0, x1, x2 = _split3_bf16(x)
    return _dot(x0, m01) + (_dot(x1, m01) + _dot(x2, m01))


_NN = (((1,), (0,)), ((), ()))
_NT = (((1,), (1,)), ((), ()))
_TN = (((0,), (0,)), ((), ()))


def _mm(a, b, passes, dims=_NN):
    dg = lambda x, y: lax.dot_general(x, y, dims, preferred_element_type=F32)
    if passes == 1:
        return dg(a.astype(BF16), b.astype(BF16))
    a_hi = a.astype(BF16)
    a_lo = (a - a_hi.astype(F32)).astype(BF16)
    b_hi = b.astype(BF16)
    b_lo = (b - b_hi.astype(F32)).astype(BF16)
    return dg(a_hi, b_hi) + (dg(a_hi, b_lo) + dg(a_lo, b_hi))


RWKV_PASSES = dict(gram=1, inv=1, apply=1, state=1)


def _rwkv_scan_kernel(r_ref, lw_ref, k_ref, v_ref, kk_ref, a_ref, g_ref, rk_ref, lnw_ref, lnb_ref,
                      o_ref, state_ref):
    c = pl.program_id(1)
    CH = RWKV_CHUNK
    N = RWKV_HEAD_DIM
    P2 = 2 * CH
    NP = RWKV_DIM // LANES
    pg, pi, pa, ps = (RWKV_PASSES[n] for n in ("gram", "inv", "apply", "state"))

    @pl.when(c == 0)
    def _():
        state_ref[...] = jnp.zeros_like(state_ref)

    lane = lax.broadcasted_iota(jnp.int32, (1, LANES), 1)
    head0 = lane < N
    ri = lax.broadcasted_iota(jnp.int32, (P2, P2), 0)
    ci = lax.broadcasted_iota(jnp.int32, (P2, P2), 1)
    same_head = (ri // CH) == (ci // CH)
    strict = (ri % CH) > (ci % CH)
    incl = (ri % CH) >= (ci % CH)
    eye = (ri == ci).astype(F32)
    ones_bd = same_head.astype(BF16)
    ti = lax.broadcasted_iota(jnp.int32, (CH, CH), 0)
    si = lax.broadcasted_iota(jnp.int32, (CH, CH), 1)
    tril_incl = (ti >= si).astype(BF16)

    def stack(x):
        return jnp.concatenate([jnp.where(head0, x, 0.0), jnp.where(head0, 0.0, x)], axis=0)

    pairs = range(NP)
    sls = [slice(p * LANES, (p + 1) * LANES) for p in pairs]
    r = [r_ref[:, sl] for sl in sls]
    k = [k_ref[:, sl] for sl in sls]
    v = [v_ref[:, sl] for sl in sls]
    kk = [kk_ref[:, sl] for sl in sls]
    lw = [lw_ref[:, sl] for sl in sls]
    cum = [_sum01_left(tril_incl, lw[p]) for p in pairs]
    cum_end = [cum[p][CH - 1:CH, :] for p in pairs]
    b = [kk[p] * a_ref[:, sls[p]] for p in pairs]
    e_neg = [jnp.exp(-cum[p]) for p in pairs]
    e_end = [jnp.exp(cum_end[p] - cum[p]) for p in pairs]
    A_st = [stack(-kk[p] * jnp.exp(cum[p] - lw[p])) for p in pairs]
    R_st = [stack(r[p] * jnp.exp(cum[p])) for p in pairs]
    BK = [jnp.concatenate([stack(b[p] * e_neg[p]), stack(k[p] * e_neg[p])], axis=0) for p in pairs]
    Bend_st = [stack(b[p] * e_end[p]) for p in pairs]
    Kend_st = [stack(k[p] * e_end[p]) for p in pairs]
    V_st = [stack(v[p]) for p in pairs]

    G = [_mm(jnp.concatenate([A_st[p], R_st[p]], axis=0), BK[p], pg, _NT) for p in pairs]
    Lab = [jnp.where(strict, G[p][:P2, :P2], 0.0) for p in pairs]
    Lak = [jnp.where(strict, G[p][:P2, P2:], 0.0) for p in pairs]
    Mrb = [jnp.where(incl, G[p][P2:, :P2], 0.0) for p in pairs]
    Mrk = [jnp.where(incl, G[p][P2:, P2:], 0.0) for p in pairs]

    Tinv = [eye + Lab[p] for p in pairs]
    Lp = Lab
    for _ in range(int(math.log2(CH)) - 1):
        Lp = [_mm(Lp[p], Lp[p], pi) for p in pairs]
        Tinv = [Tinv[p] + _mm(Tinv[p], Lp[p], pi) for p in pairs]

    LakV = [_mm(Lak[p], V_st[p], pa) for p in pairs]
    AU = [_mm(Tinv[p], jnp.concatenate([A_st[p], LakV[p]], axis=1), pa) for p in pairs]
    MM = [_mm(Mrb[p], AU[p], pa) for p in pairs]
    MV = [_mm(Mrk[p], V_st[p], pa) for p in pairs]
    BT = [_mm(Bend_st[p], AU[p], pa, _TN) for p in pairs]
    KV = [_mm(Kend_st[p], V_st[p], pa, _TN) for p in pairs]
    Rhat = [R_st[p] + MM[p][:, :LANES] for p in pairs]
    Mmat = [eye * jnp.exp(cum_end[p]) + BT[p][:, :LANES] for p in pairs]

    RS = [_mm(jnp.concatenate([Rhat[p], Mmat[p]], axis=0), state_ref[p], ps) for p in pairs]
    for p in pairs:
        state_ref[p] = jnp.where(same_head, RS[p][P2:] + BT[p][:, LANES:] + KV[p], 0.0)
    Y_st = [RS[p][:P2] + MM[p][:, LANES:] + MV[p] for p in pairs]
    y = [Y_st[p][:CH] + Y_st[p][CH:] for p in pairs]

    sums = [_sum01_right(jnp.concatenate([y[p], r[p] * k[p] * rk_ref[:, sls[p]]], axis=0), ones_bd)
            for p in pairs]
    yc = [y[p] - sums[p][:CH] * (1.0 / N) for p in pairs]
    var = [_sum01_right(yc[p] * yc[p], ones_bd) * (1.0 / N) for p in pairs]
    for p in pairs:
        yn = yc[p] * lax.rsqrt(var[p] + RWKV_GN_EPS) * lnw_ref[:, sls[p]] + lnb_ref[:, sls[p]]
        o_ref[:, sls[p]] = ((yn + sums[p][CH:] * v[p]) * g_ref[:, sls[p]]).astype(o_ref.dtype)


def rwkv_scan(r, lw, k, v, kk, a, g, r_k, ln_w, ln_b, B, S):
    T, C = r.shape
    CH = RWKV_CHUNK
    nc = S // CH
    blk = pl.BlockSpec((CH, C), lambda b, c: (b * nc + c, 0))
    par = pl.BlockSpec((1, C), lambda b, c: (0, 0))
    return pl.pallas_call(
        _rwkv_scan_kernel,
        grid=(B, nc),
        in_specs=[blk] * 7 + [par] * 3,
        out_specs=blk,
        out_shape=jax.ShapeDtypeStruct((T, C), BF16),
        scratch_shapes=[pltpu.VMEM((C // LANES, LANES, LANES), F32)],
        compiler_params=_cparams("parallel", "arbitrary"),
        name="rwkv_scan",
    )(r, lw, k, v, kk, a, g, r_k.reshape(1, C), ln_w.reshape(1, C), ln_b.reshape(1, C))


def _rope_pairs(x, cos2, sin2):
    lane = lax.broadcasted_iota(jnp.int32, (1, LANES), 1)
    first_half = (lane % MLA_ROPE_DIM) < (MLA_ROPE_DIM // 2)
    partner = jnp.where(first_half, pltpu.roll(x, LANES - MLA_ROPE_DIM // 2, 1),
                        pltpu.roll(x, MLA_ROPE_DIM // 2, 1))
    return x * cos2 + partner * sin2


def _mla_cproj_kernel(h_ref, w_ref, cos_ref, sin_ref, c_ref, kr_ref):
    acc = _dot(h_ref[...], w_ref[...])
    NC = MLA_Q_LORA + MLA_KV_LORA
    c_ref[...] = acc[:, :NC]
    kr_ref[...] = _rope_pairs(acc[:, NC:], cos_ref[...], sin_ref[...]).astype(kr_ref.dtype)


def mla_cproj(h16, w_in, cos2, sin2, S, tm=512):
    T, D = h16.shape
    NC = MLA_Q_LORA + MLA_KV_LORA
    w = jnp.concatenate([w_in, w_in[:, NC:]], axis=1).astype(BF16)
    ns = S // tm
    return pl.pallas_call(
        _mla_cproj_kernel,
        grid=(T // tm,),
        in_specs=[pl.BlockSpec((tm, D), lambda i: (i, 0)),
                  pl.BlockSpec((D, NC + LANES), lambda i: (0, 0)),
                  pl.BlockSpec((tm, LANES), lambda i: (i % ns, 0)),
                  pl.BlockSpec((tm, LANES), lambda i: (i % ns, 0))],
        out_specs=[pl.BlockSpec((tm, NC), lambda i: (i, 0)),
                   pl.BlockSpec((tm, LANES), lambda i: (i, 0))],
        out_shape=[jax.ShapeDtypeStruct((T, NC), F32), jax.ShapeDtypeStruct((T, LANES), BF16)],
        compiler_params=_cparams("parallel"),
        name="mla_cproj",
    )(h16, w, cos2, sin2)


def _rms_up_kernel(rope_from, out_scale, c_ref, g_ref, w_ref, cos_ref, sin_ref, o_ref):
    x = c_ref[...]
    ms = jnp.mean(x * x, axis=-1, keepdims=True)
    cn = (x * lax.rsqrt(ms + RMS_EPS) * g_ref[...]).astype(BF16)
    N = w_ref.shape[1]
    plain = N if rope_from is None else rope_from
    TN = 1024
    for n0 in range(0, plain, TN):
        acc = _dot(cn, w_ref[:, n0:n0 + TN])
        if out_scale != 1.0:
            acc = acc * out_scale
        o_ref[:, n0:n0 + TN] = acc.astype(o_ref.dtype)
    if rope_from is not None:
        cos2, sin2 = cos_ref[...], sin_ref[...]
        acc = _dot(cn, w_ref[:, rope_from:]) * out_scale
        for gidx in range(acc.shape[1] // LANES):
            sl = slice(gidx * LANES, (gidx + 1) * LANES)
            o_ref[:, rope_from + gidx * LANES:rope_from + (gidx + 1) * LANES] = (
                _rope_pairs(acc[:, sl], cos2, sin2).astype(o_ref.dtype))


def rms_up_proj(c, col_block, gain, w, cos2, sin2, S, rope_from, out_scale=1.0, tm=512):
    T = c.shape[0]
    K, N = w.shape
    ns = S // tm
    return pl.pallas_call(
        functools.partial(_rms_up_kernel, rope_from, out_scale),
        grid=(T // tm,),
        in_specs=[pl.BlockSpec((tm, K), lambda i: (i, col_block)),
                  pl.BlockSpec((1, K), lambda i: (0, 0)),
                  pl.BlockSpec((K, N), lambda i: (0, 0)),
                  pl.BlockSpec((tm, LANES), lambda i: (i % ns, 0)),
                  pl.BlockSpec((tm, LANES), lambda i: (i % ns, 0))],
        out_specs=pl.BlockSpec((tm, N), lambda i: (i, 0)),
        out_shape=jax.ShapeDtypeStruct((T, N), BF16),
        compiler_params=_cparams("parallel"),
        name="rms_up_proj",
    )(c, gain.reshape(1, K), w, cos2, sin2)


MLA_GROUP = 4
MLA_SCORE_SCALE = (MLA_NOPE_DIM + MLA_ROPE_DIM) ** -0.5 * math.log2(math.e)


def _mla_attn_kernel(qn_ref, qr_ref, kv_ref, kr_ref, o_ref, vt_ref):
    qi = pl.program_id(2)
    TQ = qn_ref.shape[0]
    S = kv_ref.shape[0]
    G = MLA_GROUP
    lane = lax.broadcasted_iota(jnp.int32, (1, LANES), 1)

    @pl.when(qi == 0)
    def _():
        for j in range(G):
            for n in range(S // TQ):
                rows = slice(n * TQ, (n + 1) * TQ)
                v_t = kv_ref[rows, (2 * j + 1) * LANES:(2 * j + 2) * LANES]
                vt_ref[j, :, rows] = v_t.astype(F32).T.astype(BF16)

    key = lax.broadcasted_iota(jnp.int32, (TQ, TQ), 0)
    qry = lax.broadcasted_iota(jnp.int32, (TQ, TQ), 1)
    qts = []
    for j in range(G):
        own = (lane // MLA_ROPE_DIM) == (j % 2)
        qr = qr_ref[:, (j // 2) * LANES:(j // 2 + 1) * LANES]
        qc = jnp.concatenate([qn_ref[:, j * LANES:(j + 1) * LANES],
                              jnp.where(own, qr, jnp.zeros_like(qr))], axis=1)
        qts.append(qc.astype(F32).T.astype(BF16))

    def scores(j, off):
        kc = jnp.concatenate([kv_ref[pl.ds(off, TQ), 2 * j * LANES:(2 * j + 1) * LANES],
                              kr_ref[pl.ds(off, TQ), :]], axis=1)
        return _dot(kc, qts[j])

    def softmax_step(t, m_run, l_run, diag):
        if diag:
            t = jnp.where(key <= qry, t, NEG_INF)
        m_new = jnp.maximum(m_run, jnp.max(_col_groups(t, jnp.max), axis=0, keepdims=True))
        alpha = jnp.exp2(m_run - m_new)
        pr = jnp.exp2(t - m_new)
        l_new = alpha * l_run + _col_groups(pr, jnp.sum)
        return m_new, l_new, alpha, pr.astype(BF16)

    def weighted_values(j, off, acc, alpha, pr):
        return alpha * acc + _dot(vt_ref[j, :, pl.ds(off, TQ)], pr)

    def body(n, carry):
        off = pl.multiple_of(n * TQ, TQ)
        out = []
        for j in range(G):
            m_run, l_run, acc = carry[j]
            m_new, l_new, alpha, pr = softmax_step(scores(j, off), m_run, l_run, False)
            out.append((m_new, l_new, weighted_values(j, off, acc, alpha, pr)))
        return tuple(out)

    carry = tuple((jnp.full((1, TQ), NEG_INF, F32), jnp.zeros((8, TQ), F32),
                   jnp.zeros((LANES, TQ), F32)) for j in range(G))
    carry = lax.fori_loop(0, qi, body, carry)
    off_d = pl.multiple_of(qi * TQ, TQ)
    for j in range(G):
        m_run, l_run, acc = carry[j]
        m_new, l_new, alpha, pr = softmax_step(scores(j, off_d), m_run, l_run, True)
        acc = weighted_values(j, off_d, acc, alpha, pr)
        out_t = acc / jnp.sum(l_new, axis=0, keepdims=True)
        o_ref[:, j * LANES:(j + 1) * LANES] = out_t.T.astype(o_ref.dtype)


def mla_attention(q, kv, kr2, B, S, tq=512):
    T = B * S
    H, G = MLA_HEADS, MLA_GROUP
    nq = S // tq
    n_nope = H // G
    return pl.pallas_call(
        _mla_attn_kernel,
        grid=(B, H // G, nq),
        in_specs=[pl.BlockSpec((tq, G * LANES), lambda b, g, i: (b * nq + i, g)),
                  pl.BlockSpec((tq, G // 2 * LANES), lambda b, g, i: (b * nq + i, 2 * n_nope + g)),
                  pl.BlockSpec((S, 2 * G * LANES), lambda b, g, i: (b, g)),
                  pl.BlockSpec((S, LANES), lambda b, g, i: (b, 0))],
        out_specs=pl.BlockSpec((tq, G * LANES), lambda b, g, i: (b * nq + i, g)),
        out_shape=jax.ShapeDtypeStruct((T, H * MLA_V_DIM), BF16),
        scratch_shapes=[pltpu.VMEM((G, LANES, S), BF16)],
        compiler_params=_cparams("parallel", "parallel", "arbitrary"),
        name="mla_attention",
    )(q, q, kv, kr2)


def _rope_tables(S):
    half = MLA_ROPE_DIM // 2
    inv = ROPE_THETA ** (-jnp.arange(0, MLA_ROPE_DIM, 2, dtype=F32) / MLA_ROPE_DIM)
    ang = jnp.arange(S, dtype=F32)[:, None] * inv[None, :]
    cos, sin = jnp.cos(ang), jnp.sin(ang)
    reps = LANES // MLA_ROPE_DIM
    cos2 = jnp.tile(jnp.concatenate([cos, cos], axis=1), (1, reps))
    sin2 = jnp.tile(jnp.concatenate([-sin, sin], axis=1), (1, reps))
    assert cos2.shape == (S, LANES) and half * 2 * reps == LANES
    return cos2, sin2


def kernel(x, ev_w_in, ev_w_out, rw_mu, rw_w0, rw_w2, rw_a0, rw_a2, rw_g2, rw_k_k, rw_k_a, rw_r_k,
           rw_ln_w, rw_ln_b, rw_v0, rw_v1, rw_v2, od_w_in, od_q_norm, od_kv_norm, od_w_uq, od_w_ukv,
           od_w_out, ln_mix_g, ln_mix_b, ln_ffn_g, ln_ffn_b, moe_w_r, moe_b_r, moe_w1, moe_b1,
           moe_w2, moe_b2):
    B, S, D = x.shape
    T = B * S
    h32 = x.reshape(T, D)
    h16 = h32.astype(BF16)
    cos2, sin2 = _rope_tables(S)
    n_main = 3 * MOBA_DIM + 3 * RWKV_DIM
    lora_pad = -(-RWKV_LORA // LANES) * LANES
    qd = MLA_NOPE_DIM + MLA_ROPE_DIM
    perm = np.concatenate([
        (np.arange(MLA_HEADS)[:, None] * qd + np.arange(MLA_NOPE_DIM)[None, :]).reshape(-1),
        (np.arange(MLA_HEADS)[:, None] * qd + MLA_NOPE_DIM + np.arange(MLA_ROPE_DIM)[None, :]).reshape(-1)])
    v_first = None
    for layer in range(DEPTH):
        j = layer // 2
        if layer % 2 == 0:
            w_in = ev_w_in[j]
            w_main = w_in[:, :n_main].astype(BF16)
            w_lora = jnp.zeros((D, lora_pad), BF16).at[:, :RWKV_LORA].set(w_in[:, n_main:].astype(BF16))
            proj = matmul(h16, w_main, F32)
            plora = matmul(h16, w_lora, F32)
            a_out = moba_attention(proj, B, S)
            v_lora = None if j == 0 else (rw_v0[j - 1], rw_v1[j - 1], rw_v2[j - 1])
            r, lw, k, v, kk, a, g = rwkv_prep(proj, plora, S, rw_mu[j], rw_w0[j], rw_w2[j], rw_a0[j],
                                             rw_a2[j], rw_g2[j], rw_k_k[j], rw_k_a[j], v_first, v_lora)
            if j == 0:
                v_first = v
            b_out = rwkv_scan(r, lw, k, v, kk, a, g, rw_r_k[j], rw_ln_w[j], rw_ln_b[j], B, S)
            w_out = ev_w_out[j].astype(BF16)
            h32, h16 = proj_residual_ln([a_out, b_out], [w_out[:MOBA_DIM], w_out[MOBA_DIM:]], h32,
                                        ln_mix_g[layer], ln_mix_b[layer])
        else:
            c, kr2 = mla_cproj(h16, od_w_in[j], cos2, sin2, S)
            q = rms_up_proj(c, 0, od_q_norm[j], od_w_uq[j][:, perm].astype(BF16), cos2, sin2, S,
                            rope_from=MLA_HEADS * MLA_NOPE_DIM, out_scale=MLA_SCORE_SCALE)
            kv = rms_up_proj(c, 1, od_kv_norm[j], od_w_ukv[j].astype(BF16), cos2, sin2, S,
                             rope_from=None)
            o = mla_attention(q, kv, kr2, B, S)
            h32, h16 = proj_residual_ln([o], [od_w_out[j].astype(BF16)], h32,
                                        ln_mix_g[layer], ln_mix_b[layer])
        h32, h16 = moe_layer(h32, h16, moe_w_r[layer], moe_b_r[layer], moe_w1, moe_b1[layer], moe_w2,
                             moe_b2[layer], ln_ffn_g[layer], ln_ffn_b[layer], layer)
    return h32.reshape(B, S, D)
```

```python
import functools
import math

import jax
import jax.numpy as jnp
import numpy as np
from jax import lax
from jax.experimental import pallas as pl
from jax.experimental.pallas import tpu as pltpu

F32 = jnp.float32
BF16 = jnp.bfloat16
HIGHEST = lax.Precision.HIGHEST

DEPTH = 4
MOBA_HEADS = 16
MOBA_HEAD_DIM = 64
MOBA_DIM = MOBA_HEADS * MOBA_HEAD_DIM
MOBA_BLOCK = 256
MOBA_TOPK = 3
RWKV_HEADS = 16
RWKV_HEAD_DIM = 64
RWKV_DIM = RWKV_HEADS * RWKV_HEAD_DIM
RWKV_DECAY_LORA = 64
RWKV_A_LORA = 64
RWKV_G_LORA = 160
RWKV_LORA = RWKV_DECAY_LORA + RWKV_A_LORA + RWKV_G_LORA
RWKV_GN_EPS = 64e-5
RWKV_CHUNK = 64
MLA_HEADS = 16
MLA_Q_LORA = 512
MLA_KV_LORA = 512
MLA_NOPE_DIM = 128
MLA_ROPE_DIM = 64
MLA_V_DIM = 128
ROPE_THETA = 10000.0
N_EXPERTS = 32
TOP_K = 4
D_EXPERT = 1024
SWIGLU_ALPHA = 1.702
SWIGLU_LIMIT = 7.0
MOE_ROWS = 256
DEEPNORM_ALPHA = (2 * DEPTH) ** 0.25
LN_EPS = 1e-5
RMS_EPS = 1e-6
NEG_INF = -1e30

LANES = 128
VMEM_LIMIT = 56 * 1024 * 1024


def _cparams(*sem):
    return pltpu.CompilerParams(dimension_semantics=sem, vmem_limit_bytes=VMEM_LIMIT)


def _dot(a, b, precision=None):
    return jnp.dot(a, b, preferred_element_type=F32, precision=precision)


def _mm_kernel(a_ref, w_ref, o_ref):
    o_ref[...] = _dot(a_ref[...], w_ref[...]).astype(o_ref.dtype)


def matmul(a, w, out_dtype, tm=1024, tn=512):
    M, K = a.shape
    N = w.shape[1]
    tm, tn = min(tm, M), min(tn, N)
    assert M % tm == 0 and N % tn == 0
    return pl.pallas_call(
        _mm_kernel,
        grid=(M // tm, N // tn),
        in_specs=[pl.BlockSpec((tm, K), lambda i, j: (i, 0)),
                  pl.BlockSpec((K, tn), lambda i, j: (0, j))],
        out_specs=pl.BlockSpec((tm, tn), lambda i, j: (i, j)),
        out_shape=jax.ShapeDtypeStruct((M, N), out_dtype),
        compiler_params=_cparams("parallel", "parallel"),
        name="matmul",
    )(a, w)


def _layer_norm_rows(z, g, b):
    mu = jnp.mean(z, axis=-1, keepdims=True)
    zc = z - mu
    var = jnp.mean(zc * zc, axis=-1, keepdims=True)
    return zc * lax.rsqrt(var + LN_EPS) * g + b


def _proj_ln_kernel(n_in, *refs):
    a_refs = refs[:n_in]
    w_refs = refs[n_in:2 * n_in]
    h_ref, g_ref, b_ref, o32_ref, o16_ref = refs[2 * n_in:]
    acc = DEEPNORM_ALPHA * h_ref[...]
    for a_ref, w_ref in zip(a_refs, w_refs):
        acc = acc + _dot(a_ref[...], w_ref[...])
    y = _layer_norm_rows(acc, g_ref[...], b_ref[...])
    o32_ref[...] = y
    o16_ref[...] = y.astype(BF16)


def proj_residual_ln(a_list, w_list, h, g, b, tm=256):
    M, D = h.shape
    n_in = len(a_list)
    in_specs = ([pl.BlockSpec((tm, a.shape[1]), lambda i: (i, 0)) for a in a_list]
                + [pl.BlockSpec(w.shape, lambda i: (0, 0)) for w in w_list]
                + [pl.BlockSpec((tm, D), lambda i: (i, 0)),
                   pl.BlockSpec((1, D), lambda i: (0, 0)),
                   pl.BlockSpec((1, D), lambda i: (0, 0))])
    return pl.pallas_call(
        functools.partial(_proj_ln_kernel, n_in),
        grid=(M // tm,),
        in_specs=in_specs,
        out_specs=[pl.BlockSpec((tm, D), lambda i: (i, 0)),
                   pl.BlockSpec((tm, D), lambda i: (i, 0))],
        out_shape=[jax.ShapeDtypeStruct((M, D), F32), jax.ShapeDtypeStruct((M, D), BF16)],
        compiler_params=_cparams("parallel"),
        name="proj_residual_ln",
    )(*a_list, *w_list, h, g.reshape(1, D), b.reshape(1, D))


def _combine_ln_kernel(y_ref, gate_ref, h_ref, g_ref, b_ref, o32_ref, o16_ref):
    acc = DEEPNORM_ALPHA * h_ref[...]
    gates = gate_ref[...]
    for k in range(TOP_K):
        acc = acc + y_ref[k].astype(F32) * gates[:, k:k + 1]
    y = _layer_norm_rows(acc, g_ref[...], b_ref[...])
    o32_ref[...] = y
    o16_ref[...] = y.astype(BF16)


def combine_residual_ln(y4, gates, h, g, b, tm=256):
    M, D = h.shape
    return pl.pallas_call(
        _combine_ln_kernel,
        grid=(M // tm,),
        in_specs=[pl.BlockSpec((TOP_K, tm, D), lambda i: (0, i, 0)),
                  pl.BlockSpec((tm, TOP_K), lambda i: (i, 0)),
                  pl.BlockSpec((tm, D), lambda i: (i, 0)),
                  pl.BlockSpec((1, D), lambda i: (0, 0)),
                  pl.BlockSpec((1, D), lambda i: (0, 0))],
        out_specs=[pl.BlockSpec((tm, D), lambda i: (i, 0)),
                   pl.BlockSpec((tm, D), lambda i: (i, 0))],
        out_shape=[jax.ShapeDtypeStruct((M, D), F32), jax.ShapeDtypeStruct((M, D), BF16)],
        compiler_params=_cparams("parallel"),
        name="combine_residual_ln",
    )(y4, gates, h, g.reshape(1, D), b.reshape(1, D))


def _router_kernel(h_ref, w_ref, b_ref, gate_ref, idx_ref, cnt_ref, base_ref):
    i = pl.program_id(0)
    tm = h_ref.shape[0]

    @pl.when(i == 0)
    def _():
        base_ref[...] = jnp.zeros_like(base_ref)

    lane = lax.broadcasted_iota(jnp.int32, (tm, LANES), 1)
    logits = _mm(h_ref[...], w_ref[...], 3) + b_ref[...]
    val = jnp.where(lane < N_EXPERTS, logits, -jnp.inf)
    tops, idxs = [], []
    member = jnp.zeros((tm, LANES), F32)
    lane_f = lane.astype(F32)
    for _ in range(TOP_K):
        mx = jnp.max(val, axis=-1, keepdims=True)
        ix = jnp.min(jnp.where(val == mx, lane_f, float(LANES)), axis=-1, keepdims=True).astype(jnp.int32)
        hit = lane == ix
        member = member + jnp.where(hit, 1.0, 0.0)
        val = jnp.where(hit, -jnp.inf, val)
        tops.append(mx)
        idxs.append(ix)
    ex = [jnp.exp(t - tops[0]) for t in tops]
    den = ex[0] + ex[1] + ex[2] + ex[3]
    ti = lax.broadcasted_iota(jnp.int32, (tm, tm), 0)
    si = lax.broadcasted_iota(jnp.int32, (tm, tm), 1)
    csum = base_ref[...] + _dot((ti >= si).astype(BF16), member.astype(BF16))
    gate_out = jnp.zeros((tm, LANES), F32)
    idx_out = jnp.zeros((tm, LANES), jnp.int32)
    for k in range(TOP_K):
        rank = jnp.sum(jnp.where(lane == idxs[k], csum - 1.0, 0.0), axis=-1, keepdims=True)
        gate_out = jnp.where(lane == k, ex[k] / den, gate_out)
        idx_out = jnp.where(lane == k, idxs[k], idx_out)
        idx_out = jnp.where(lane == TOP_K + k, rank.astype(jnp.int32), idx_out)
    gate_ref[...] = gate_out
    idx_ref[...] = idx_out
    base_ref[...] = csum[tm - 1:tm, :]
    cnt_ref[...] = csum[tm - 1:tm, :].astype(jnp.int32)


def route_tokens(h, w_r, b_r, tm=512):
    M, D = h.shape
    tm = min(tm, M)
    w = jnp.zeros((D, LANES), F32).at[:, :N_EXPERTS].set(w_r)
    b = jnp.zeros((1, LANES), F32).at[0, :N_EXPERTS].set(b_r)
    gate, idx, cnt = pl.pallas_call(
        _router_kernel,
        grid=(M // tm,),
        in_specs=[pl.BlockSpec((tm, D), lambda i: (i, 0)),
                  pl.BlockSpec((D, LANES), lambda i: (0, 0)),
                  pl.BlockSpec((1, LANES), lambda i: (0, 0))],
        out_specs=[pl.BlockSpec((tm, LANES), lambda i: (i, 0)),
                   pl.BlockSpec((tm, LANES), lambda i: (i, 0)),
                   pl.BlockSpec((1, LANES), lambda i: (0, 0))],
        out_shape=[jax.ShapeDtypeStruct((M, LANES), F32), jax.ShapeDtypeStruct((M, LANES), jnp.int32),
                   jax.ShapeDtypeStruct((1, LANES), jnp.int32)],
        scratch_shapes=[pltpu.VMEM((1, LANES), F32)],
        compiler_params=_cparams("arbitrary"),
        name="route_tokens",
    )(h, w, b)
    return gate[:, :TOP_K], idx[:, :TOP_K], idx[:, TOP_K:2 * TOP_K], cnt[0, :N_EXPERTS]


MOE_W_CHUNKS = 4


def _expert_kernel(cb_ref, ce_ref, le_ref, lc_ref, fl_ref, x_ref, w1c_ref, w2c_ref, b1_ref, b2_ref,
                   o_ref, w1a_ref, w2a_ref, w1b_ref, w2b_ref):
    s = pl.program_id(0)
    fl = fl_ref[s]
    comp = (fl & 1) == 1
    load = (fl & 2) == 2
    par = (fl & 4) == 4
    c = lc_ref[s]
    C1 = w1c_ref.shape[0]
    C2 = w2c_ref.shape[0]

    def cast_chunk(w1_dst, w2_dst):
        w1_dst[pl.ds(pl.multiple_of(c * C1, C1), C1), :] = w1c_ref[...].astype(BF16)
        w2_dst[pl.ds(pl.multiple_of(c * C2, C2), C2), :] = w2c_ref[...].astype(BF16)

    def ffn(w1_src, w2_src):
        hgu = _dot(x_ref[...], w1_src[...]) + b1_ref[0]
        gate = jnp.minimum(hgu[:, :D_EXPERT], SWIGLU_LIMIT)
        up = jnp.clip(hgu[:, D_EXPERT:], -SWIGLU_LIMIT, SWIGLU_LIMIT)
        act = gate * jax.nn.sigmoid(SWIGLU_ALPHA * gate) * (up + 1.0)
        o_ref[...] = (_dot(act.astype(BF16), w2_src[...]) + b2_ref[0]).astype(o_ref.dtype)

    bufs = ((w1a_ref, w2a_ref), (w1b_ref, w2b_ref))
    for p in (0, 1):
        use, fill = bufs[p], bufs[1 - p]
        in_phase = par == (p == 1)

        @pl.when(comp & in_phase)
        def _():
            cast_chunk(*fill)
            ffn(*use)

        @pl.when(load & jnp.logical_not(comp) & in_phase)
        def _():
            cast_chunk(*fill)

    @pl.when((fl & 8) == 8)
    def _():
        o_ref[...] = jnp.zeros_like(o_ref)


def _expert_schedule(nblk, n_blocks):
    E, NC = N_EXPERTS, MOE_W_CHUNKS
    ph_ids = jnp.arange(E + 1, dtype=jnp.int32)
    prev_n = jnp.concatenate([jnp.zeros((1,), jnp.int32), nblk.astype(jnp.int32)])
    plen = jnp.where(ph_ids == 0, NC, jnp.where(ph_ids == E, prev_n, jnp.maximum(prev_n, NC)))
    pend = jnp.cumsum(plen)
    pstart = pend - plen
    n_steps = n_blocks + E * NC
    s = jnp.arange(n_steps, dtype=jnp.int32)
    ph = jnp.minimum(jnp.sum(pend[None, :] <= s[:, None], axis=1), E).astype(jnp.int32)
    t = s - pstart[ph]
    comp = (ph >= 1) & (t < prev_n[ph])
    load = (ph < E) & (t < NC)
    n_used = jnp.sum(nblk).astype(jnp.int32)
    tail_blk = n_used + (s - pend[E])
    fill = (s >= pend[E]) & (tail_blk < n_blocks)
    comp_blk = jnp.where(s >= pend[E], jnp.minimum(tail_blk, n_blocks - 1),
                         jnp.maximum(jnp.cumsum(comp.astype(jnp.int32)) - 1, 0))
    comp_e = jnp.clip(ph - 1, 0, E - 1)
    load_e = jnp.minimum(ph, E - 1)
    load_c = jnp.where(ph >= E, NC - 1, jnp.minimum(t, NC - 1))
    flags = (comp.astype(jnp.int32) + 2 * load.astype(jnp.int32)
             + 4 * ((ph - 1) % 2 == 1).astype(jnp.int32) + 8 * fill.astype(jnp.int32))
    return comp_blk.astype(jnp.int32), comp_e.astype(jnp.int32), load_e, load_c.astype(jnp.int32), flags


def expert_ffn(xbuf, nblk, w1, b1, w2, b2, layer):
    rows, D = xbuf.shape
    R = MOE_ROWS
    NC = MOE_W_CHUNKS
    n_blocks = rows // R
    H2 = 2 * D_EXPERT
    C1, C2 = D // NC, D_EXPERT // NC
    sched = _expert_schedule(nblk, n_blocks)
    n_steps = sched[0].shape[0]
    grid_spec = pltpu.PrefetchScalarGridSpec(
        num_scalar_prefetch=5,
        grid=(n_steps,),
        in_specs=[pl.BlockSpec((R, D), lambda s, cb, ce, le, lc, fl: (cb[s], 0)),
                  pl.BlockSpec((None, None, C1, H2), lambda s, cb, ce, le, lc, fl: (layer, le[s], lc[s], 0)),
                  pl.BlockSpec((None, None, C2, D), lambda s, cb, ce, le, lc, fl: (layer, le[s], lc[s], 0)),
                  pl.BlockSpec((1, 1, H2), lambda s, cb, ce, le, lc, fl: (ce[s], 0, 0)),
                  pl.BlockSpec((1, 1, D), lambda s, cb, ce, le, lc, fl: (ce[s], 0, 0))],
        out_specs=pl.BlockSpec((R, D), lambda s, cb, ce, le, lc, fl: (cb[s], 0)),
        scratch_shapes=[pltpu.VMEM((D, H2), BF16), pltpu.VMEM((D_EXPERT, D), BF16),
                        pltpu.VMEM((D, H2), BF16), pltpu.VMEM((D_EXPERT, D), BF16)],
    )
    return pl.pallas_call(
        _expert_kernel,
        grid_spec=grid_spec,
        out_shape=jax.ShapeDtypeStruct((rows, D), BF16),
        compiler_params=_cparams("arbitrary"),
        name="expert_ffn",
    )(*sched, xbuf, w1, w2, b1.reshape(N_EXPERTS, 1, -1), b2.reshape(N_EXPERTS, 1, -1))


def moe_layer(h32, h16, w_r, b_r, w1, b1, w2, b2, ln_g, ln_b, layer):
    T, D = h32.shape
    R = MOE_ROWS
    M = T * TOP_K
    gates, top_idx, rank, counts = route_tokens(h32, w_r, b_r)
    padded = (counts + R - 1) // R * R
    pad_end = jnp.cumsum(padded)
    pad_start = pad_end - padded
    pos = rank + pad_start[top_idx]
    n_blocks = (M + N_EXPERTS * (R - 1) + R - 1) // R
    rows = n_blocks * R
    _, tok_sorted = lax.sort_key_val(pos.reshape(M), jnp.arange(M, dtype=jnp.int32) // TOP_K)
    row = jnp.arange(rows, dtype=jnp.int32)
    blk_e = jnp.minimum(jnp.sum(pad_end[None, :] <= (row[::R])[:, None], axis=1), N_EXPERTS - 1)
    first = jnp.cumsum(counts) - counts
    row_e = jnp.repeat(blk_e, R)
    in_e = row - pad_start[row_e]
    src_tok = jnp.where(in_e < counts[row_e], tok_sorted[jnp.minimum(first[row_e] + in_e, M - 1)], row % T)
    xbuf = h16[src_tok]
    ybuf = expert_ffn(xbuf, padded // R, w1, b1, w2, b2, layer)
    y4 = ybuf[pos.T]
    return combine_residual_ln(y4, gates, h32, ln_g, ln_b)


MOBA_GROUP = 4
MOBA_GATE_ROWS = 8


def _col_groups(x, op):
    rows, cols = x.shape
    return op(x.reshape(rows // 8, 8, cols), axis=0)


def _moba_kernel(q_ref, k_ref, v_ref, o_ref, kmean_ref, vt_ref, bias_ref):
    pg = pl.program_id(1)
    cur = pl.program_id(2)
    BLK = MOBA_BLOCK
    GP = MOBA_GROUP
    NR = MOBA_GATE_ROWS
    nb = k_ref.shape[0] // BLK
    assert nb <= NR
    lane = lax.broadcasted_iota(jnp.int32, (1, LANES), 1)
    LOG2E = math.log2(math.e)
    c2 = MOBA_HEAD_DIM ** -0.5 * LOG2E

    @pl.when(cur == 0)
    def _():
        kmean_ref[...] = jnp.zeros_like(kmean_ref)
        for g in range(GP):
            for n in range(nb):
                rows = slice(n * BLK, (n + 1) * BLK)
                kmean_ref[g, n:n + 1, :] = jnp.mean(k_ref[rows, g * LANES:(g + 1) * LANES], axis=0,
                                                    keepdims=True)
                vt_ref[g, :, rows] = v_ref[rows, g * LANES:(g + 1) * LANES].T.astype(BF16)

    Q2 = 2 * BLK
    key = lax.broadcasted_iota(jnp.int32, (BLK, Q2), 0)
    qry = lax.broadcasted_iota(jnp.int32, (BLK, Q2), 1) % BLK
    krow = lax.broadcasted_iota(jnp.int32, (BLK, LANES), 0)
    klane = lax.broadcasted_iota(jnp.int32, (BLK, LANES), 1)
    k_extra = jnp.where(klane < 2, krow, 0).astype(F32).astype(BF16)
    srow = lax.broadcasted_iota(jnp.int32, (LANES, BLK), 0)
    blk_row = lax.broadcasted_iota(jnp.int32, (NR, BLK), 0)

    qts = []
    for g in range(GP):
        q = q_ref[:, g * LANES:(g + 1) * LANES]
        halves, biases = [], []
        for hh in range(2):
            own = (lane // MOBA_HEAD_DIM) == hh
            qm_t = jnp.where(own, q, 0.0).T
            head_idx = 2 * (pg * GP + g) + hh + 1
            sl2 = jnp.exp(jnp.full((1, 1), head_idx, jnp.int32).astype(F32)
                          * (-8.0 * math.log(2.0) / MOBA_HEADS)) * LOG2E
            s_hi = sl2.astype(BF16).astype(F32)
            s_lo = (sl2 - s_hi).astype(BF16).astype(F32)
            q_extra = jnp.where(srow == 0, s_hi, jnp.where(srow == 1, s_lo, 0.0))
            halves.append(jnp.concatenate([qm_t * c2, q_extra], axis=0).astype(BF16))
            gate = _dot(kmean_ref[g], qm_t, precision=HIGHEST)
            rank = jnp.zeros((NR, BLK), jnp.int32)
            for m in range(nb):
                gm = gate[m:m + 1, :]
                ahead = (gm > gate) | ((gm == gate) & (m < blk_row))
                rank = rank + jnp.where(ahead, 1, 0) * (m < cur).astype(jnp.int32)
            sel = (rank < MOBA_TOPK) & (blk_row < cur)
            biases.append(jnp.where(sel, -sl2 * ((cur - blk_row) * BLK).astype(F32), NEG_INF))
        qts.append(jnp.concatenate(halves, axis=1))
        bias_ref[g] = jnp.concatenate(biases, axis=1)

    def scores(g, off):
        kc = jnp.concatenate([k_ref[pl.ds(off, BLK), g * LANES:(g + 1) * LANES].astype(BF16), k_extra],
                             axis=1)
        return _dot(kc, qts[g])

    def softmax_step(t, m_run, l_run, bias):
        if bias is None:
            t = jnp.where(key <= qry, t, NEG_INF)
            m_new = jnp.maximum(m_run, jnp.max(_col_groups(t, jnp.max), axis=0, keepdims=True))
            shift = m_new
        else:
            m_new = jnp.maximum(m_run, jnp.max(_col_groups(t, jnp.max), axis=0, keepdims=True) + bias)
            shift = m_new - bias
        alpha = jnp.exp2(m_run - m_new)
        pr = jnp.exp2(t - shift)
        l_new = alpha * l_run + _col_groups(pr, jnp.sum)
        return m_new, l_new, alpha, pr.astype(BF16)

    def block(g, n, carry, bias):
        m_run, l_run, acc = carry
        off = pl.multiple_of(n * BLK, BLK)
        m_new, l_new, alpha, pr = softmax_step(scores(g, off), m_run, l_run, bias)
        return m_new, l_new, alpha * acc + _dot(vt_ref[g, :, pl.ds(off, BLK)], pr)

    init = (jnp.full((1, Q2), NEG_INF, F32), jnp.zeros((8, Q2), F32), jnp.zeros((LANES, Q2), F32))
    carry = tuple(block(g, cur, init, None) for g in range(GP))

    def body(n, carry):
        return tuple(block(g, n, carry[g], bias_ref[g, pl.ds(n, 1), :]) for g in range(GP))

    carry = lax.fori_loop(0, cur, body, carry)
    row128 = lax.broadcasted_iota(jnp.int32, (LANES, 1), 0)
    for g in range(GP):
        m_run, l_run, acc = carry[g]
        out = acc / jnp.sum(l_run, axis=0, keepdims=True)
        out_t = jnp.where(row128 < MOBA_HEAD_DIM, out[:, :BLK], out[:, BLK:])
        o_ref[:, g * LANES:(g + 1) * LANES] = out_t.T.astype(o_ref.dtype)


def moba_attention(proj, B, S):
    T = B * S
    BLK = MOBA_BLOCK
    GP = MOBA_GROUP
    W = GP * LANES
    n_grp = MOBA_DIM // W
    nq = S // BLK
    return pl.pallas_call(
        _moba_kernel,
        grid=(B, n_grp, nq),
        in_specs=[pl.BlockSpec((BLK, W), lambda b, p, c: (b * nq + c, p)),
                  pl.BlockSpec((S, W), lambda b, p, c: (b, n_grp + p)),
                  pl.BlockSpec((S, W), lambda b, p, c: (b, 2 * n_grp + p))],
        out_specs=pl.BlockSpec((BLK, W), lambda b, p, c: (b * nq + c, p)),
        out_shape=jax.ShapeDtypeStruct((T, MOBA_DIM), BF16),
        scratch_shapes=[pltpu.VMEM((GP, MOBA_GATE_ROWS, LANES), F32),
                        pltpu.VMEM((GP, LANES, S), BF16),
                        pltpu.VMEM((GP, MOBA_GATE_ROWS, 2 * BLK), F32)],
        compiler_params=_cparams("parallel", "parallel", "arbitrary"),
        name="moba_attention",
    )(proj, proj, proj)


def _rwkv_prep_kernel(has_vres, S, *refs):
    if has_vres:
        (pm_ref, pl_ref, pm_prev_ref, pl_prev_ref, mu_m_ref, mu_l_ref, w0_ref, w2_ref, a0_ref, a2_ref,
         g2_ref, kk_ref, ka_ref, vfirst_ref, v0_ref, v1_ref, v2_ref,
         r_o, lw_o, k_o, v_o, kkn_o, a_o, g_o) = refs
    else:
        (pm_ref, pl_ref, pm_prev_ref, pl_prev_ref, mu_m_ref, mu_l_ref, w0_ref, w2_ref, a0_ref, a2_ref,
         g2_ref, kk_ref, ka_ref,
         r_o, lw_o, k_o, v_o, kkn_o, a_o, g_o) = refs
    i = pl.program_id(0)
    tm = pm_ref.shape[0]
    C = RWKV_DIM
    row = lax.broadcasted_iota(jnp.int32, (tm, 1), 0)
    seq_start = (i * tm) % S == 0

    def shifted(cur_ref, prev_ref, mu_ref):
        x = cur_ref[...]
        prev_row = jnp.where(seq_start, 0.0, prev_ref[7:8, :])
        xs = jnp.where(row == 0, prev_row, pltpu.roll(x, 1, 0))
        return x + (xs - x) * mu_ref[...]

    pm = shifted(pm_ref, pm_prev_ref, mu_m_ref)
    plo = shifted(pl_ref, pl_prev_ref, mu_l_ref)
    r = pm[:, :C]
    k = pm[:, C:2 * C]
    v = pm[:, 2 * C:]
    wd = plo[:, :RWKV_DECAY_LORA]
    ad = plo[:, RWKV_DECAY_LORA:RWKV_DECAY_LORA + RWKV_A_LORA]
    gd = plo[:, RWKV_DECAY_LORA + RWKV_A_LORA:RWKV_LORA]
    w = -jax.nn.softplus(-(w0_ref[...] + _dot(jnp.tanh(wd).astype(BF16), w2_ref[...]))) - 0.5
    a = jax.nn.sigmoid(a0_ref[...] + _dot(ad.astype(BF16), a2_ref[...]))
    g = _dot(jax.nn.sigmoid(gd).astype(BF16), g2_ref[...])
    if has_vres:
        lo = _dot(_dot(v.astype(BF16), v1_ref[...]).astype(BF16), v2_ref[...])
        v = v + (vfirst_ref[...] - v) * jax.nn.sigmoid(v0_ref[...] + lo)
    kk = k * kk_ref[...]
    hid_r = lax.broadcasted_iota(jnp.int32, (LANES, LANES), 0) // RWKV_HEAD_DIM
    hid_c = lax.broadcasted_iota(jnp.int32, (LANES, LANES), 1) // RWKV_HEAD_DIM
    ones_bd = (hid_r == hid_c).astype(F32)
    for pp in range(C // LANES):
        sl = slice(pp * LANES, (pp + 1) * LANES)
        kkp = kk[:, sl]
        ss = _dot(kkp * kkp, ones_bd, precision=HIGHEST)
        kkn_o[:, sl] = kkp * lax.rsqrt(jnp.maximum(ss, 1e-24))
    r_o[...] = r
    lw_o[...] = -jnp.exp(w)
    k_o[...] = k * (1.0 + (a - 1.0) * ka_ref[...])
    v_o[...] = v
    a_o[...] = a
    g_o[...] = g


def rwkv_prep(proj, plora, S, mu, w0, w2, a0, a2, g2, k_k, k_a, v_first, v_lora, tm=256):
    T = proj.shape[0]
    C = RWKV_DIM
    LP = plora.shape[1]
    has_vres = v_lora is not None
    mu_m = mu[:3 * C].reshape(1, 3 * C)
    mu_l = jnp.zeros((1, LP), F32).at[0, :RWKV_LORA].set(mu[3 * C:])
    row = lambda z: z.reshape(1, -1)
    full = lambda z: pl.BlockSpec(z.shape, lambda i: (0,) * z.ndim)
    args = [proj, plora, proj, plora, mu_m, mu_l, row(w0), w2.astype(BF16), row(a0), a2.astype(BF16),
            g2.astype(BF16), row(k_k), row(k_a)]
    in_specs = [pl.BlockSpec((tm, 3 * C), lambda i: (i, 1)),
                pl.BlockSpec((tm, LP), lambda i: (i, 0)),
                pl.BlockSpec((8, 3 * C), lambda i: (jnp.maximum(i * (tm // 8) - 1, 0), 1)),
                pl.BlockSpec((8, LP), lambda i: (jnp.maximum(i * (tm // 8) - 1, 0), 0))]
    in_specs += [full(z) for z in args[4:]]
    if has_vres:
        v0, v1, v2 = v_lora
        extra = [v_first, row(v0), v1.astype(BF16), v2.astype(BF16)]
        args += extra
        in_specs += [pl.BlockSpec((tm, C), lambda i: (i, 0))] + [full(z) for z in extra[1:]]
    out_spec = pl.BlockSpec((tm, C), lambda i: (i, 0))
    return pl.pallas_call(
        functools.partial(_rwkv_prep_kernel, has_vres, S),
        grid=(T // tm,),
        in_specs=in_specs,
        out_specs=[out_spec] * 7,
        out_shape=[jax.ShapeDtypeStruct((T, C), F32)] * 7,
        compiler_params=_cparams("parallel"),
        name="rwkv_prep",
    )(*args)


def _split3_bf16(x):
    x0 = x.astype(BF16)
    r1 = x - x0.astype(F32)
    x1 = r1.astype(BF16)
    x2 = (r1 - x1.astype(F32)).astype(BF16)
    return x0, x1, x2


def _sum01_left(m01, x):
    x0, x1, x2 = _split3_bf16(x)
    return _dot(m01, x0) + (_dot(m01, x1) + _dot(m01, x2))


def _sum01_right(x, m01):
    x0, x1, x2 = _split3_bf16(x)
    return _dot(x0, m01) + (_dot(x1, m01) + _dot(x2, m01))


_NN = (((1,), (0,)), ((), ()))
_NT = (((1,), (1,)), ((), ()))
_TN = (((0,), (0,)), ((), ()))


def _mm(a, b, passes, dims=_NN):
    dg = lambda x, y: lax.dot_general(x, y, dims, preferred_element_type=F32)
    if passes == 1:
        return dg(a.astype(BF16), b.astype(BF16))
    a_hi = a.astype(BF16)
    a_lo = (a - a_hi.astype(F32)).astype(BF16)
    b_hi = b.astype(BF16)
    b_lo = (b - b_hi.astype(F32)).astype(BF16)
    return dg(a_hi, b_hi) + (dg(a_hi, b_lo) + dg(a_lo, b_hi))


RWKV_PASSES = dict(gram=1, inv=1, apply=1, state=1)


def _rwkv_scan_kernel(r_ref, lw_ref, k_ref, v_ref, kk_ref, a_ref, g_ref, rk_ref, lnw_ref, lnb_ref,
                      o_ref, state_ref):
    c = pl.program_id(1)
    CH = RWKV_CHUNK
    N = RWKV_HEAD_DIM
    P2 = 2 * CH
    NP = RWKV_DIM // LANES
    pg, pi, pa, ps = (RWKV_PASSES[n] for n in ("gram", "inv", "apply", "state"))

    @pl.when(c == 0)
    def _():
        state_ref[...] = jnp.zeros_like(state_ref)

    lane = lax.broadcasted_iota(jnp.int32, (1, LANES), 1)
    head0 = lane < N
    ri = lax.broadcasted_iota(jnp.int32, (P2, P2), 0)
    ci = lax.broadcasted_iota(jnp.int32, (P2, P2), 1)
    same_head = (ri // CH) == (ci // CH)
    strict = (ri % CH) > (ci % CH)
    incl = (ri % CH) >= (ci % CH)
    eye = (ri == ci).astype(F32)
    ones_bd = same_head.astype(BF16)
    ti = lax.broadcasted_iota(jnp.int32, (CH, CH), 0)
    si = lax.broadcasted_iota(jnp.int32, (CH, CH), 1)
    tril_incl = (ti >= si).astype(BF16)

    def stack(x):
        return jnp.concatenate([jnp.where(head0, x, 0.0), jnp.where(head0, 0.0, x)], axis=0)

    pairs = range(NP)
    sls = [slice(p * LANES, (p + 1) * LANES) for p in pairs]
    r = [r_ref[:, sl] for sl in sls]
    k = [k_ref[:, sl] for sl in sls]
    v = [v_ref[:, sl] for sl in sls]
    kk = [kk_ref[:, sl] for sl in sls]
    lw = [lw_ref[:, sl] for sl in sls]
    cum = [_sum01_left(tril_incl, lw[p]) for p in pairs]
    cum_end = [cum[p][CH - 1:CH, :] for p in pairs]
    b = [kk[p] * a_ref[:, sls[p]] for p in pairs]
    e_neg = [jnp.exp(-cum[p]) for p in pairs]
    e_end = [jnp.exp(cum_end[p] - cum[p]) for p in pairs]
    A_st = [stack(-kk[p] * jnp.exp(cum[p] - lw[p])) for p in pairs]
    R_st = [stack(r[p] * jnp.exp(cum[p])) for p in pairs]
    BK = [jnp.concatenate([stack(b[p] * e_neg[p]), stack(k[p] * e_neg[p])], axis=0) for p in pairs]
    Bend_st = [stack(b[p] * e_end[p]) for p in pairs]
    Kend_st = [stack(k[p] * e_end[p]) for p in pairs]
    V_st = [stack(v[p]) for p in pairs]

    G = [_mm(jnp.concatenate([A_st[p], R_st[p]], axis=0), BK[p], pg, _NT) for p in pairs]
    Lab = [jnp.where(strict, G[p][:P2, :P2], 0.0) for p in pairs]
    Lak = [jnp.where(strict, G[p][:P2, P2:], 0.0) for p in pairs]
    Mrb = [jnp.where(incl, G[p][P2:, :P2], 0.0) for p in pairs]
    Mrk = [jnp.where(incl, G[p][P2:, P2:], 0.0) for p in pairs]

    Tinv = [eye + Lab[p] for p in pairs]
    Lp = Lab
    for _ in range(int(math.log2(CH)) - 1):
        Lp = [_mm(Lp[p], Lp[p], pi) for p in pairs]
        Tinv = [Tinv[p] + _mm(Tinv[p], Lp[p], pi) for p in pairs]

    LakV = [_mm(Lak[p], V_st[p], pa) for p in pairs]
    AU = [_mm(Tinv[p], jnp.concatenate([A_st[p], LakV[p]], axis=1), pa) for p in pairs]
    MM = [_mm(Mrb[p], AU[p], pa) for p in pairs]
    MV = [_mm(Mrk[p], V_st[p], pa) for p in pairs]
    BT = [_mm(Bend_st[p], AU[p], pa, _TN) for p in pairs]
    KV = [_mm(Kend_st[p], V_st[p], pa, _TN) for p in pairs]
    Rhat = [R_st[p] + MM[p][:, :LANES] for p in pairs]
    Mmat = [eye * jnp.exp(cum_end[p]) + BT[p][:, :LANES] for p in pairs]

    RS = [_mm(jnp.concatenate([Rhat[p], Mmat[p]], axis=0), state_ref[p], ps) for p in pairs]
    for p in pairs:
        state_ref[p] = jnp.where(same_head, RS[p][P2:] + BT[p][:, LANES:] + KV[p], 0.0)
    Y_st = [RS[p][:P2] + MM[p][:, LANES:] + MV[p] for p in pairs]
    y = [Y_st[p][:CH] + Y_st[p][CH:] for p in pairs]

    sums = [_sum01_right(jnp.concatenate([y[p], r[p] * k[p] * rk_ref[:, sls[p]]], axis=0), ones_bd)
            for p in pairs]
    yc = [y[p] - sums[p][:CH] * (1.0 / N) for p in pairs]
    var = [_sum01_right(yc[p] * yc[p], ones_bd) * (1.0 / N) for p in pairs]
    for p in pairs:
        yn = yc[p] * lax.rsqrt(var[p] + RWKV_GN_EPS) * lnw_ref[:, sls[p]] + lnb_ref[:, sls[p]]
        o_ref[:, sls[p]] = ((yn + sums[p][CH:] * v[p]) * g_ref[:, sls[p]]).astype(o_ref.dtype)


def rwkv_scan(r, lw, k, v, kk, a, g, r_k, ln_w, ln_b, B, S):
    T, C = r.shape
    CH = RWKV_CHUNK
    nc = S // CH
    blk = pl.BlockSpec((CH, C), lambda b, c: (b * nc + c, 0))
    par = pl.BlockSpec((1, C), lambda b, c: (0, 0))
    return pl.pallas_call(
        _rwkv_scan_kernel,
        grid=(B, nc),
        in_specs=[blk] * 7 + [par] * 3,
        out_specs=blk,
        out_shape=jax.ShapeDtypeStruct((T, C), BF16),
        scratch_shapes=[pltpu.VMEM((C // LANES, LANES, LANES), F32)],
        compiler_params=_cparams("parallel", "arbitrary"),
        name="rwkv_scan",
    )(r, lw, k, v, kk, a, g, r_k.reshape(1, C), ln_w.reshape(1, C), ln_b.reshape(1, C))


def _rope_pairs(x, cos2, sin2):
    lane = lax.broadcasted_iota(jnp.int32, (1, LANES), 1)
    first_half = (lane % MLA_ROPE_DIM) < (MLA_ROPE_DIM // 2)
    partner = jnp.where(first_half, pltpu.roll(x, LANES - MLA_ROPE_DIM // 2, 1),
                        pltpu.roll(x, MLA_ROPE_DIM // 2, 1))
    return x * cos2 + partner * sin2


def _mla_cproj_kernel(h_ref, w_ref, cos_ref, sin_ref, c_ref, kr_ref):
    acc = _dot(h_ref[...], w_ref[...])
    NC = MLA_Q_LORA + MLA_KV_LORA
    c_ref[...] = acc[:, :NC]
    kr_ref[...] = _rope_pairs(acc[:, NC:], cos_ref[...], sin_ref[...]).astype(kr_ref.dtype)


def mla_cproj(h16, w_in, cos2, sin2, S, tm=512):
    T, D = h16.shape
    NC = MLA_Q_LORA + MLA_KV_LORA
    w = jnp.concatenate([w_in, w_in[:, NC:]], axis=1).astype(BF16)
    ns = S // tm
    return pl.pallas_call(
        _mla_cproj_kernel,
        grid=(T // tm,),
        in_specs=[pl.BlockSpec((tm, D), lambda i: (i, 0)),
                  pl.BlockSpec((D, NC + LANES), lambda i: (0, 0)),
                  pl.BlockSpec((tm, LANES), lambda i: (i % ns, 0)),
                  pl.BlockSpec((tm, LANES), lambda i: (i % ns, 0))],
        out_specs=[pl.BlockSpec((tm, NC), lambda i: (i, 0)),
                   pl.BlockSpec((tm, LANES), lambda i: (i, 0))],
        out_shape=[jax.ShapeDtypeStruct((T, NC), F32), jax.ShapeDtypeStruct((T, LANES), BF16)],
        compiler_params=_cparams("parallel"),
        name="mla_cproj",
    )(h16, w, cos2, sin2)


def _rms_up_kernel(rope_from, out_scale, c_ref, g_ref, w_ref, cos_ref, sin_ref, o_ref):
    x = c_ref[...]
    ms = jnp.mean(x * x, axis=-1, keepdims=True)
    cn = (x * lax.rsqrt(ms + RMS_EPS) * g_ref[...]).astype(BF16)
    N = w_ref.shape[1]
    plain = N if rope_from is None else rope_from
    TN = 1024
    for n0 in range(0, plain, TN):
        acc = _dot(cn, w_ref[:, n0:n0 + TN])
        if out_scale != 1.0:
            acc = acc * out_scale
        o_ref[:, n0:n0 + TN] = acc.astype(o_ref.dtype)
    if rope_from is not None:
        cos2, sin2 = cos_ref[...], sin_ref[...]
        acc = _dot(cn, w_ref[:, rope_from:]) * out_scale
        for gidx in range(acc.shape[1] // LANES):
            sl = slice(gidx * LANES, (gidx + 1) * LANES)
            o_ref[:, rope_from + gidx * LANES:rope_from + (gidx + 1) * LANES] = (
                _rope_pairs(acc[:, sl], cos2, sin2).astype(o_ref.dtype))


def rms_up_proj(c, col_block, gain, w, cos2, sin2, S, rope_from, out_scale=1.0, tm=512):
    T = c.shape[0]
    K, N = w.shape
    ns = S // tm
    return pl.pallas_call(
        functools.partial(_rms_up_kernel, rope_from, out_scale),
        grid=(T // tm,),
        in_specs=[pl.BlockSpec((tm, K), lambda i: (i, col_block)),
                  pl.BlockSpec((1, K), lambda i: (0, 0)),
                  pl.BlockSpec((K, N), lambda i: (0, 0)),
                  pl.BlockSpec((tm, LANES), lambda i: (i % ns, 0)),
                  pl.BlockSpec((tm, LANES), lambda i: (i % ns, 0))],
        out_specs=pl.BlockSpec((tm, N), lambda i: (i, 0)),
        out_shape=jax.ShapeDtypeStruct((T, N), BF16),
        compiler_params=_cparams("parallel"),
        name="rms_up_proj",
    )(c, gain.reshape(1, K), w, cos2, sin2)


MLA_GROUP = 4
MLA_SCORE_SCALE = (MLA_NOPE_DIM + MLA_ROPE_DIM) ** -0.5 * math.log2(math.e)


def _mla_attn_kernel(qn_ref, qr_ref, kv_ref, kr_ref, o_ref, vt_ref):
    qi = pl.program_id(2)
    TQ = qn_ref.shape[0]
    S = kv_ref.shape[0]
    G = MLA_GROUP
    lane = lax.broadcasted_iota(jnp.int32, (1, LANES), 1)

    @pl.when(qi == 0)
    def _():
        for j in range(G):
            for n in range(S // TQ):
                rows = slice(n * TQ, (n + 1) * TQ)
                v_t = kv_ref[rows, (2 * j + 1) * LANES:(2 * j + 2) * LANES]
                vt_ref[j, :, rows] = v_t.astype(F32).T.astype(BF16)

    key = lax.broadcasted_iota(jnp.int32, (TQ, TQ), 0)
    qry = lax.broadcasted_iota(jnp.int32, (TQ, TQ), 1)
    qts = []
    for j in range(G):
        own = (lane // MLA_ROPE_DIM) == (j % 2)
        qr = qr_ref[:, (j // 2) * LANES:(j // 2 + 1) * LANES]
        qc = jnp.concatenate([qn_ref[:, j * LANES:(j + 1) * LANES],
                              jnp.where(own, qr, jnp.zeros_like(qr))], axis=1)
        qts.append(qc.astype(F32).T.astype(BF16))

    def scores(j, off):
        kc = jnp.concatenate([kv_ref[pl.ds(off, TQ), 2 * j * LANES:(2 * j + 1) * LANES],
                              kr_ref[pl.ds(off, TQ), :]], axis=1)
        return _dot(kc, qts[j])

    def softmax_step(t, m_run, l_run, diag):
        if diag:
            t = jnp.where(key <= qry, t, NEG_INF)
        m_new = jnp.maximum(m_run, jnp.max(_col_groups(t, jnp.max), axis=0, keepdims=True))
        alpha = jnp.exp2(m_run - m_new)
        pr = jnp.exp2(t - m_new)
        l_new = alpha * l_run + _col_groups(pr, jnp.sum)
        return m_new, l_new, alpha, pr.astype(BF16)

    def weighted_values(j, off, acc, alpha, pr):
        return alpha * acc + _dot(vt_ref[j, :, pl.ds(off, TQ)], pr)

    def body(n, carry):
        off = pl.multiple_of(n * TQ, TQ)
        out = []
        for j in range(G):
            m_run, l_run, acc = carry[j]
            m_new, l_new, alpha, pr = softmax_step(scores(j, off), m_run, l_run, False)
            out.append((m_new, l_new, weighted_values(j, off, acc, alpha, pr)))
        return tuple(out)

    carry = tuple((jnp.full((1, TQ), NEG_INF, F32), jnp.zeros((8, TQ), F32),
                   jnp.zeros((LANES, TQ), F32)) for j in range(G))
    carry = lax.fori_loop(0, qi, body, carry)
    off_d = pl.multiple_of(qi * TQ, TQ)
    for j in range(G):
        m_run, l_run, acc = carry[j]
        m_new, l_new, alpha, pr = softmax_step(scores(j, off_d), m_run, l_run, True)
        acc = weighted_values(j, off_d, acc, alpha, pr)
        out_t = acc / jnp.sum(l_new, axis=0, keepdims=True)
        o_ref[:, j * LANES:(j + 1) * LANES] = out_t.T.astype(o_ref.dtype)


def mla_attention(q, kv, kr2, B, S, tq=512):
    T = B * S
    H, G = MLA_HEADS, MLA_GROUP
    nq = S // tq
    n_nope = H // G
    return pl.pallas_call(
        _mla_attn_kernel,
        grid=(B, H // G, nq),
        in_specs=[pl.BlockSpec((tq, G * LANES), lambda b, g, i: (b * nq + i, g)),
                  pl.BlockSpec((tq, G // 2 * LANES), lambda b, g, i: (b * nq + i, 2 * n_nope + g)),
                  pl.BlockSpec((S, 2 * G * LANES), lambda b, g, i: (b, g)),
                  pl.BlockSpec((S, LANES), lambda b, g, i: (b, 0))],
        out_specs=pl.BlockSpec((tq, G * LANES), lambda b, g, i: (b * nq + i, g)),
        out_shape=jax.ShapeDtypeStruct((T, H * MLA_V_DIM), BF16),
        scratch_shapes=[pltpu.VMEM((G, LANES, S), BF16)],
        compiler_params=_cparams("parallel", "parallel", "arbitrary"),
        name="mla_attention",
    )(q, q, kv, kr2)


def _rope_tables(S):
    half = MLA_ROPE_DIM // 2
    inv = ROPE_THETA ** (-jnp.arange(0, MLA_ROPE_DIM, 2, dtype=F32) / MLA_ROPE_DIM)
    ang = jnp.arange(S, dtype=F32)[:, None] * inv[None, :]
    cos, sin = jnp.cos(ang), jnp.sin(ang)
    reps = LANES // MLA_ROPE_DIM
    cos2 = jnp.tile(jnp.concatenate([cos, cos], axis=1), (1, reps))
    sin2 = jnp.tile(jnp.concatenate([-sin, sin], axis=1), (1, reps))
    assert cos2.shape == (S, LANES) and half * 2 * reps == LANES
    return cos2, sin2


def kernel(x, ev_w_in, ev_w_out, rw_mu, rw_w0, rw_w2, rw_a0, rw_a2, rw_g2, rw_k_k, rw_k_a, rw_r_k,
           rw_ln_w, rw_ln_b, rw_v0, rw_v1, rw_v2, od_w_in, od_q_norm, od_kv_norm, od_w_uq, od_w_ukv,
           od_w_out, ln_mix_g, ln_mix_b, ln_ffn_g, ln_ffn_b, moe_w_r, moe_b_r, moe_w1, moe_b1,
           moe_w2, moe_b2):
    B, S, D = x.shape
    T = B * S
    h32 = x.reshape(T, D)
    h16 = h32.astype(BF16)
    cos2, sin2 = _rope_tables(S)
    n_main = 3 * MOBA_DIM + 3 * RWKV_DIM
    lora_pad = -(-RWKV_LORA // LANES) * LANES
    qd = MLA_NOPE_DIM + MLA_ROPE_DIM
    perm = np.concatenate([
        (np.arange(MLA_HEADS)[:, None] * qd + np.arange(MLA_NOPE_DIM)[None, :]).reshape(-1),
        (np.arange(MLA_HEADS)[:, None] * qd + MLA_NOPE_DIM + np.arange(MLA_ROPE_DIM)[None, :]).reshape(-1)])
    v_first = None
    for layer in range(DEPTH):
        j = layer // 2
        if layer % 2 == 0:
            w_in = ev_w_in[j]
            w_main = w_in[:, :n_main].astype(BF16)
            w_lora = jnp.zeros((D, lora_pad), BF16).at[:, :RWKV_LORA].set(w_in[:, n_main:].astype(BF16))
            proj = matmul(h16, w_main, F32)
            plora = matmul(h16, w_lora, F32)
            a_out = moba_attention(proj, B, S)
            v_lora = None if j == 0 else (rw_v0[j - 1], rw_v1[j - 1], rw_v2[j - 1])
            r, lw, k, v, kk, a, g = rwkv_prep(proj, plora, S, rw_mu[j], rw_w0[j], rw_w2[j], rw_a0[j],
                                             rw_a2[j], rw_g2[j], rw_k_k[j], rw_k_a[j], v_first, v_lora)
            if j == 0:
                v_first = v
            b_out = rwkv_scan(r, lw, k, v, kk, a, g, rw_r_k[j], rw_ln_w[j], rw_ln_b[j], B, S)
            w_out = ev_w_out[j].astype(BF16)
            h32, h16 = proj_residual_ln([a_out, b_out], [w_out[:MOBA_DIM], w_out[MOBA_DIM:]], h32,
                                        ln_mix_g[layer], ln_mix_b[layer])
        else:
            c, kr2 = mla_cproj(h16, od_w_in[j], cos2, sin2, S)
            q = rms_up_proj(c, 0, od_q_norm[j], od_w_uq[j][:, perm].astype(BF16), cos2, sin2, S,
                            rope_from=MLA_HEADS * MLA_NOPE_DIM, out_scale=MLA_SCORE_SCALE)
            kv = rms_up_proj(c, 1, od_kv_norm[j], od_w_ukv[j].astype(BF16), cos2, sin2, S,
                             rope_from=None)
            o = mla_attention(q, kv, kr2, B, S)
            h32, h16 = proj_residual_ln([o], [od_w_out[j].astype(BF16)], h32,
                                        ln_mix_g[layer], ln_mix_b[layer])
        h32, h16 = moe_layer(h32, h16, moe_w_r[layer], moe_b_r[layer], moe_w1, moe_b1[layer], moe_w2,
                             moe_b2[layer], ln_ffn_g[layer], ln_ffn_b[layer], layer)
    return h32.reshape(B, S, D)
```

```python
import functools
import math

import jax
import jax.numpy as jnp
import numpy as np
from jax import lax
from jax.experimental import pallas as pl
from jax.experimental.pallas import tpu as pltpu

F32 = jnp.float32
BF16 = jnp.bfloat16
HIGHEST = lax.Precision.HIGHEST

DEPTH = 4
MOBA_HEADS = 16
MOBA_HEAD_DIM = 64
MOBA_DIM = MOBA_HEADS * MOBA_HEAD_DIM
MOBA_BLOCK = 256
MOBA_TOPK = 3
RWKV_HEADS = 16
RWKV_HEAD_DIM = 64
RWKV_DIM = RWKV_HEADS * RWKV_HEAD_DIM
RWKV_DECAY_LORA = 64
RWKV_A_LORA = 64
RWKV_G_LORA = 160
RWKV_LORA = RWKV_DECAY_LORA + RWKV_A_LORA + RWKV_G_LORA
RWKV_GN_EPS = 64e-5
RWKV_CHUNK = 64
MLA_HEADS = 16
MLA_Q_LORA = 512
MLA_KV_LORA = 512
MLA_NOPE_DIM = 128
MLA_ROPE_DIM = 64
MLA_V_DIM = 128
ROPE_THETA = 10000.0
N_EXPERTS = 32
TOP_K = 4
D_EXPERT = 1024
SWIGLU_ALPHA = 1.702
SWIGLU_LIMIT = 7.0
MOE_ROWS = 256
DEEPNORM_ALPHA = (2 * DEPTH) ** 0.25
LN_EPS = 1e-5
RMS_EPS = 1e-6
NEG_INF = -1e30

LANES = 128
VMEM_LIMIT = 56 * 1024 * 1024


def _cparams(*sem):
    return pltpu.CompilerParams(dimension_semantics=sem, vmem_limit_bytes=VMEM_LIMIT)


def _dot(a, b, precision=None):
    return jnp.dot(a, b, preferred_element_type=F32, precision=precision)


def _mm_kernel(a_ref, w_ref, o_ref):
    o_ref[...] = _dot(a_ref[...], w_ref[...]).astype(o_ref.dtype)


def matmul(a, w, out_dtype, tm=1024, tn=512):
    M, K = a.shape
    N = w.shape[1]
    tm, tn = min(tm, M), min(tn, N)
    assert M % tm == 0 and N % tn == 0
    return pl.pallas_call(
        _mm_kernel,
        grid=(M // tm, N // tn),
        in_specs=[pl.BlockSpec((tm, K), lambda i, j: (i, 0)),
                  pl.BlockSpec((K, tn), lambda i, j: (0, j))],
        out_specs=pl.BlockSpec((tm, tn), lambda i, j: (i, j)),
        out_shape=jax.ShapeDtypeStruct((M, N), out_dtype),
        compiler_params=_cparams("parallel", "parallel"),
        name="matmul",
    )(a, w)


def _layer_norm_rows(z, g, b):
    mu = jnp.mean(z, axis=-1, keepdims=True)
    zc = z - mu
    var = jnp.mean(zc * zc, axis=-1, keepdims=True)
    return zc * lax.rsqrt(var + LN_EPS) * g + b


def _proj_ln_kernel(n_in, *refs):
    a_refs = refs[:n_in]
    w_refs = refs[n_in:2 * n_in]
    h_ref, g_ref, b_ref, o32_ref, o16_ref = refs[2 * n_in:]
    acc = DEEPNORM_ALPHA * h_ref[...]
    for a_ref, w_ref in zip(a_refs, w_refs):
        acc = acc + _dot(a_ref[...], w_ref[...])
    y = _layer_norm_rows(acc, g_ref[...], b_ref[...])
    o32_ref[...] = y
    o16_ref[...] = y.astype(BF16)


def proj_residual_ln(a_list, w_list, h, g, b, tm=256):
    M, D = h.shape
    n_in = len(a_list)
    in_specs = ([pl.BlockSpec((tm, a.shape[1]), lambda i: (i, 0)) for a in a_list]
                + [pl.BlockSpec(w.shape, lambda i: (0, 0)) for w in w_list]
                + [pl.BlockSpec((tm, D), lambda i: (i, 0)),
                   pl.BlockSpec((1, D), lambda i: (0, 0)),
                   pl.BlockSpec((1, D), lambda i: (0, 0))])
    return pl.pallas_call(
        functools.partial(_proj_ln_kernel, n_in),
        grid=(M // tm,),
        in_specs=in_specs,
        out_specs=[pl.BlockSpec((tm, D), lambda i: (i, 0)),
                   pl.BlockSpec((tm, D), lambda i: (i, 0))],
        out_shape=[jax.ShapeDtypeStruct((M, D), F32), jax.ShapeDtypeStruct((M, D), BF16)],
        compiler_params=_cparams("parallel"),
        name="proj_residual_ln",
    )(*a_list, *w_list, h, g.reshape(1, D), b.reshape(1, D))


def _combine_ln_kernel(y_ref, gate_ref, h_ref, g_ref, b_ref, o32_ref, o16_ref):
    acc = DEEPNORM_ALPHA * h_ref[...]
    gates = gate_ref[...]
    for k in range(TOP_K):
        acc = acc + y_ref[k].astype(F32) * gates[:, k:k + 1]
    y = _layer_norm_rows(acc, g_ref[...], b_ref[...])
    o32_ref[...] = y
    o16_ref[...] = y.astype(BF16)


def combine_residual_ln(y4, gates, h, g, b, tm=256):
    M, D = h.shape
    return pl.pallas_call(
        _combine_ln_kernel,
        grid=(M // tm,),
        in_specs=[pl.BlockSpec((TOP_K, tm, D), lambda i: (0, i, 0)),
                  pl.BlockSpec((tm, TOP_K), lambda i: (i, 0)),
                  pl.BlockSpec((tm, D), lambda i: (i, 0)),
                  pl.BlockSpec((1, D), lambda i: (0, 0)),
                  pl.BlockSpec((1, D), lambda i: (0, 0))],
        out_specs=[pl.BlockSpec((tm, D), lambda i: (i, 0)),
                   pl.BlockSpec((tm, D), lambda i: (i, 0))],
        out_shape=[jax.ShapeDtypeStruct((M, D), F32), jax.ShapeDtypeStruct((M, D), BF16)],
        compiler_params=_cparams("parallel"),
        name="combine_residual_ln",
    )(y4, gates, h, g.reshape(1, D), b.reshape(1, D))


def _router_kernel(h_ref, w_ref, b_ref, gate_ref, idx_ref, cnt_ref, base_ref):
    i = pl.program_id(0)
    tm = h_ref.shape[0]

    @pl.when(i == 0)
    def _():
        base_ref[...] = jnp.zeros_like(base_ref)

    lane = lax.broadcasted_iota(jnp.int32, (tm, LANES), 1)
    logits = _mm(h_ref[...], w_ref[...], 3) + b_ref[...]
    val = jnp.where(lane < N_EXPERTS, logits, -jnp.inf)
    tops, idxs = [], []
    member = jnp.zeros((tm, LANES), F32)
    lane_f = lane.astype(F32)
    for _ in range(TOP_K):
        mx = jnp.max(val, axis=-1, keepdims=True)
        ix = jnp.min(jnp.where(val == mx, lane_f, float(LANES)), axis=-1, keepdims=True).astype(jnp.int32)
        hit = lane == ix
        member = member + jnp.where(hit, 1.0, 0.0)
        val = jnp.where(hit, -jnp.inf, val)
        tops.append(mx)
        idxs.append(ix)
    ex = [jnp.exp(t - tops[0]) for t in tops]
    den = ex[0] + ex[1] + ex[2] + ex[3]
    ti = lax.broadcasted_iota(jnp.int32, (tm, tm), 0)
    si = lax.broadcasted_iota(jnp.int32, (tm, tm), 1)
    csum = base_ref[...] + _dot((ti >= si).astype(BF16), member.astype(BF16))
    gate_out = jnp.zeros((tm, LANES), F32)
    idx_out = jnp.zeros((tm, LANES), jnp.int32)
    for k in range(TOP_K):
        rank = jnp.sum(jnp.where(lane == idxs[k], csum - 1.0, 0.0), axis=-1, keepdims=True)
        gate_out = jnp.where(lane == k, ex[k] / den, gate_out)
        idx_out = jnp.where(lane == k, idxs[k], idx_out)
        idx_out = jnp.where(lane == TOP_K + k, rank.astype(jnp.int32), idx_out)
    gate_ref[...] = gate_out
    idx_ref[...] = idx_out
    base_ref[...] = csum[tm - 1:tm, :]
    cnt_ref[...] = csum[tm - 1:tm, :].astype(jnp.int32)


def route_tokens(h, w_r, b_r, tm=512):
    M, D = h.shape
    tm = min(tm, M)
    w = jnp.zeros((D, LANES), F32).at[:, :N_EXPERTS].set(w_r)
    b = jnp.zeros((1, LANES), F32).at[0, :N_EXPERTS].set(b_r)
    gate, idx, cnt = pl.pallas_call(
        _router_kernel,
        grid=(M // tm,),
        in_specs=[pl.BlockSpec((tm, D), lambda i: (i, 0)),
                  pl.BlockSpec((D, LANES), lambda i: (0, 0)),
                  pl.BlockSpec((1, LANES), lambda i: (0, 0))],
        out_specs=[pl.BlockSpec((tm, LANES), lambda i: (i, 0)),
                   pl.BlockSpec((tm, LANES), lambda i: (i, 0)),
                   pl.BlockSpec((1, LANES), lambda i: (0, 0))],
        out_shape=[jax.ShapeDtypeStruct((M, LANES), F32), jax.ShapeDtypeStruct((M, LANES), jnp.int32),
                   jax.ShapeDtypeStruct((1, LANES), jnp.int32)],
        scratch_shapes=[pltpu.VMEM((1, LANES), F32)],
        compiler_params=_cparams("arbitrary"),
        name="route_tokens",
    )(h, w, b)
    return gate[:, :TOP_K], idx[:, :TOP_K], idx[:, TOP_K:2 * TOP_K], cnt[0, :N_EXPERTS]


MOE_W_CHUNKS = 4


def _expert_kernel(cb_ref, ce_ref, le_ref, lc_ref, fl_ref, x_ref, w1c_ref, w2c_ref, b1_ref, b2_ref,
                   o_ref, w1a_ref, w2a_ref, w1b_ref, w2b_ref):
    s = pl.program_id(0)
    fl = fl_ref[s]
    comp = (fl & 1) == 1
    load = (fl & 2) == 2
    par = (fl & 4) == 4
    c = lc_ref[s]
    C1 = w1c_ref.shape[0]
    C2 = w2c_ref.shape[0]

    def cast_chunk(w1_dst, w2_dst):
        w1_dst[pl.ds(pl.multiple_of(c * C1, C1), C1), :] = w1c_ref[...].astype(BF16)
        w2_dst[pl.ds(pl.multiple_of(c * C2, C2), C2), :] = w2c_ref[...].astype(BF16)

    def ffn(w1_src, w2_src):
        hgu = _dot(x_ref[...], w1_src[...]) + b1_ref[0]
        gate = jnp.minimum(hgu[:, :D_EXPERT], SWIGLU_LIMIT)
        up = jnp.clip(hgu[:, D_EXPERT:], -SWIGLU_LIMIT, SWIGLU_LIMIT)
        act = gate * jax.nn.sigmoid(SWIGLU_ALPHA * gate) * (up + 1.0)
        o_ref[...] = (_dot(act.astype(BF16), w2_src[...]) + b2_ref[0]).astype(o_ref.dtype)

    bufs = ((w1a_ref, w2a_ref), (w1b_ref, w2b_ref))
    for p in (0, 1):
        use, fill = bufs[p], bufs[1 - p]
        in_phase = par == (p == 1)

        @pl.when(comp & in_phase)
        def _():
            cast_chunk(*fill)
            ffn(*use)

        @pl.when(load & jnp.logical_not(comp) & in_phase)
        def _():
            cast_chunk(*fill)

    @pl.when((fl & 8) == 8)
    def _():
        o_ref[...] = jnp.zeros_like(o_ref)


def _prefix_sum(x):
    n = x.shape[0]
    tri = jnp.arange(n)[None, :] <= jnp.arange(n)[:, None]
    return jnp.sum(jnp.where(tri, x[None, :], 0), axis=1).astype(jnp.int32)


def _expert_schedule(nblk, n_blocks):
    E, NC = N_EXPERTS, MOE_W_CHUNKS
    ph_ids = jnp.arange(E + 1, dtype=jnp.int32)
    prev_n = jnp.concatenate([jnp.zeros((1,), jnp.int32), nblk.astype(jnp.int32)])
    plen = jnp.where(ph_ids == 0, NC, jnp.where(ph_ids == E, prev_n, jnp.maximum(prev_n, NC)))
    pend = _prefix_sum(plen)
    pstart = pend - plen
    n_steps = n_blocks + E * NC
    s = jnp.arange(n_steps, dtype=jnp.int32)
    ph = jnp.minimum(jnp.sum(pend[None, :] <= s[:, None], axis=1), E).astype(jnp.int32)
    t = s - pstart[ph]
    comp = (ph >= 1) & (t < prev_n[ph])
    load = (ph < E) & (t < NC)
    n_used = jnp.sum(nblk).astype(jnp.int32)
    tail_blk = n_used + (s - pend[E])
    fill = (s >= pend[E]) & (tail_blk < n_blocks)
    comp_blk = jnp.where(s >= pend[E], jnp.minimum(tail_blk, n_blocks - 1),
                         jnp.maximum(_prefix_sum(comp.astype(jnp.int32)) - 1, 0))
    comp_e = jnp.clip(ph - 1, 0, E - 1)
    load_e = jnp.minimum(ph, E - 1)
    load_c = jnp.where(ph >= E, NC - 1, jnp.minimum(t, NC - 1))
    flags = (comp.astype(jnp.int32) + 2 * load.astype(jnp.int32)
             + 4 * ((ph - 1) % 2 == 1).astype(jnp.int32) + 8 * fill.astype(jnp.int32))
    return comp_blk.astype(jnp.int32), comp_e.astype(jnp.int32), load_e, load_c.astype(jnp.int32), flags


def expert_ffn(xbuf, nblk, w1, b1, w2, b2, layer):
    rows, D = xbuf.shape
    R = MOE_ROWS
    NC = MOE_W_CHUNKS
    n_blocks = rows // R
    H2 = 2 * D_EXPERT
    C1, C2 = D // NC, D_EXPERT // NC
    sched = _expert_schedule(nblk, n_blocks)
    n_steps = sched[0].shape[0]
    grid_spec = pltpu.PrefetchScalarGridSpec(
        num_scalar_prefetch=5,
        grid=(n_steps,),
        in_specs=[pl.BlockSpec((R, D), lambda s, cb, ce, le, lc, fl: (cb[s], 0)),
                  pl.BlockSpec((None, None, C1, H2), lambda s, cb, ce, le, lc, fl: (layer, le[s], lc[s], 0)),
                  pl.BlockSpec((None, None, C2, D), lambda s, cb, ce, le, lc, fl: (layer, le[s], lc[s], 0)),
                  pl.BlockSpec((1, 1, H2), lambda s, cb, ce, le, lc, fl: (ce[s], 0, 0)),
                  pl.BlockSpec((1, 1, D), lambda s, cb, ce, le, lc, fl: (ce[s], 0, 0))],
        out_specs=pl.BlockSpec((R, D), lambda s, cb, ce, le, lc, fl: (cb[s], 0)),
        scratch_shapes=[pltpu.VMEM((D, H2), BF16), pltpu.VMEM((D_EXPERT, D), BF16),
                        pltpu.VMEM((D, H2), BF16), pltpu.VMEM((D_EXPERT, D), BF16)],
    )
    return pl.pallas_call(
        _expert_kernel,
        grid_spec=grid_spec,
        out_shape=jax.ShapeDtypeStruct((rows, D), BF16),
        compiler_params=_cparams("arbitrary"),
        name="expert_ffn",
    )(*sched, xbuf, w1, w2, b1.reshape(N_EXPERTS, 1, -1), b2.reshape(N_EXPERTS, 1, -1))


def moe_layer(h32, h16, w_r, b_r, w1, b1, w2, b2, ln_g, ln_b, layer):
    T, D = h32.shape
    R = MOE_ROWS
    M = T * TOP_K
    gates, top_idx, rank, counts = route_tokens(h32, w_r, b_r)
    padded = (counts + R - 1) // R * R
    pad_end = _prefix_sum(padded)
    pad_start = pad_end - padded
    pos = rank + pad_start[top_idx]
    n_blocks = (M + N_EXPERTS * (R - 1) + R - 1) // R
    rows = n_blocks * R
    _, tok_sorted = lax.sort_key_val(pos.reshape(M), jnp.arange(M, dtype=jnp.int32) // TOP_K)
    blk_row0 = jnp.arange(n_blocks, dtype=jnp.int32) * R
    blk_e = jnp.minimum(jnp.sum(pad_end[None, :] <= blk_row0[:, None], axis=1), N_EXPERTS - 1)
    first = _prefix_sum(counts) - counts
    row = blk_row0[:, None] + jnp.arange(R, dtype=jnp.int32)[None, :]
    in_e = row - pad_start[blk_e][:, None]
    src_tok = jnp.where(in_e < counts[blk_e][:, None],
                        tok_sorted[jnp.minimum(first[blk_e][:, None] + in_e, M - 1)],
                        row % T).reshape(rows)
    xbuf = h16[src_tok]
    ybuf = expert_ffn(xbuf, padded // R, w1, b1, w2, b2, layer)
    y4 = ybuf[pos.T]
    return combine_residual_ln(y4, gates, h32, ln_g, ln_b)


MOBA_GROUP = 4
MOBA_GATE_ROWS = 8


def _col_groups(x, op):
    rows, cols = x.shape
    return op(x.reshape(rows // 8, 8, cols), axis=0)


def _moba_kernel(q_ref, k_ref, v_ref, o_ref, kmean_ref, vt_ref, bias_ref):
    pg = pl.program_id(1)
    cur = pl.program_id(2)
    BLK = MOBA_BLOCK
    GP = MOBA_GROUP
    NR = MOBA_GATE_ROWS
    nb = k_ref.shape[0] // BLK
    assert nb <= NR
    lane = lax.broadcasted_iota(jnp.int32, (1, LANES), 1)
    LOG2E = math.log2(math.e)
    c2 = MOBA_HEAD_DIM ** -0.5 * LOG2E

    @pl.when(cur == 0)
    def _():
        kmean_ref[...] = jnp.zeros_like(kmean_ref)
        for g in range(GP):
            for n in range(nb):
                rows = slice(n * BLK, (n + 1) * BLK)
                kmean_ref[g, n:n + 1, :] = jnp.mean(k_ref[rows, g * LANES:(g + 1) * LANES], axis=0,
                                                    keepdims=True)
                vt_ref[g, :, rows] = v_ref[rows, g * LANES:(g + 1) * LANES].T.astype(BF16)

    Q2 = 2 * BLK
    key = lax.broadcasted_iota(jnp.int32, (BLK, Q2), 0)
    qry = lax.broadcasted_iota(jnp.int32, (BLK, Q2), 1) % BLK
    krow = lax.broadcasted_iota(jnp.int32, (BLK, LANES), 0)
    klane = lax.broadcasted_iota(jnp.int32, (BLK, LANES), 1)
    k_extra = jnp.where(klane < 2, krow, 0).astype(F32).astype(BF16)
    srow = lax.broadcasted_iota(jnp.int32, (LANES, BLK), 0)
    blk_row = lax.broadcasted_iota(jnp.int32, (NR, BLK), 0)

    qts = []
    for g in range(GP):
        q = q_ref[:, g * LANES:(g + 1) * LANES]
        halves, biases = [], []
        for hh in range(2):
            own = (lane // MOBA_HEAD_DIM) == hh
            qm_t = jnp.where(own, q, 0.0).T
            head_idx = 2 * (pg * GP + g) + hh + 1
            sl2 = jnp.exp(jnp.full((1, 1), head_idx, jnp.int32).astype(F32)
                          * (-8.0 * math.log(2.0) / MOBA_HEADS)) * LOG2E
            s_hi = sl2.astype(BF16).astype(F32)
            s_lo = (sl2 - s_hi).astype(BF16).astype(F32)
            q_extra = jnp.where(srow == 0, s_hi, jnp.where(srow == 1, s_lo, 0.0))
            halves.append(jnp.concatenate([qm_t * c2, q_extra], axis=0).astype(BF16))
            gate = _dot(kmean_ref[g], qm_t, precision=HIGHEST)
            rank = jnp.zeros((NR, BLK), jnp.int32)
            for m in range(nb):
                gm = gate[m:m + 1, :]
                ahead = (gm > gate) | ((gm == gate) & (m < blk_row))
                rank = rank + jnp.where(ahead, 1, 0) * (m < cur).astype(jnp.int32)
            sel = (rank < MOBA_TOPK) & (blk_row < cur)
            biases.append(jnp.where(sel, -sl2 * ((cur - blk_row) * BLK).astype(F32), NEG_INF))
        qts.append(jnp.concatenate(halves, axis=1))
        bias_ref[g] = jnp.concatenate(biases, axis=1)

    def scores(g, off):
        kc = jnp.concatenate([k_ref[pl.ds(off, BLK), g * LANES:(g + 1) * LANES].astype(BF16), k_extra],
                             axis=1)
        return _dot(kc, qts[g])

    def softmax_step(t, m_run, l_run, bias):
        if bias is None:
            t = jnp.where(key <= qry, t, NEG_INF)
            m_new = jnp.maximum(m_run, jnp.max(_col_groups(t, jnp.max), axis=0, keepdims=True))
            shift = m_new
        else:
            m_new = jnp.maximum(m_run, jnp.max(_col_groups(t, jnp.max), axis=0, keepdims=True) + bias)
            shift = m_new - bias
        alpha = jnp.exp2(m_run - m_new)
        pr = jnp.exp2(t - shift)
        l_new = alpha * l_run + _col_groups(pr, jnp.sum)
        return m_new, l_new, alpha, pr.astype(BF16)

    def block(g, n, carry, bias):
        m_run, l_run, acc = carry
        off = pl.multiple_of(n * BLK, BLK)
        m_new, l_new, alpha, pr = softmax_step(scores(g, off), m_run, l_run, bias)
        return m_new, l_new, alpha * acc + _dot(vt_ref[g, :, pl.ds(off, BLK)], pr)

    init = (jnp.full((1, Q2), NEG_INF, F32), jnp.zeros((8, Q2), F32), jnp.zeros((LANES, Q2), F32))
    carry = tuple(block(g, cur, init, None) for g in range(GP))

    def body(n, carry):
        return tuple(block(g, n, carry[g], bias_ref[g, pl.ds(n, 1), :]) for g in range(GP))

    carry = lax.fori_loop(0, cur, body, carry)
    row128 = lax.broadcasted_iota(jnp.int32, (LANES, 1), 0)
    for g in range(GP):
        m_run, l_run, acc = carry[g]
        out = acc / jnp.sum(l_run, axis=0, keepdims=True)
        out_t = jnp.where(row128 < MOBA_HEAD_DIM, out[:, :BLK], out[:, BLK:])
        o_ref[:, g * LANES:(g + 1) * LANES] = out_t.T.astype(o_ref.dtype)


def moba_attention(proj, B, S):
    T = B * S
    BLK = MOBA_BLOCK
    GP = MOBA_GROUP
    W = GP * LANES
    n_grp = MOBA_DIM // W
    nq = S // BLK
    return pl.pallas_call(
        _moba_kernel,
        grid=(B, n_grp, nq),
        in_specs=[pl.BlockSpec((BLK, W), lambda b, p, c: (b * nq + c, p)),
                  pl.BlockSpec((S, W), lambda b, p, c: (b, n_grp + p)),
                  pl.BlockSpec((S, W), lambda b, p, c: (b, 2 * n_grp + p))],
        out_specs=pl.BlockSpec((BLK, W), lambda b, p, c: (b * nq + c, p)),
        out_shape=jax.ShapeDtypeStruct((T, MOBA_DIM), BF16),
        scratch_shapes=[pltpu.VMEM((GP, MOBA_GATE_ROWS, LANES), F32),
                        pltpu.VMEM((GP, LANES, S), BF16),
                        pltpu.VMEM((GP, MOBA_GATE_ROWS, 2 * BLK), F32)],
        compiler_params=_cparams("parallel", "parallel", "arbitrary"),
        name="moba_attention",
    )(proj, proj, proj)


def _rwkv_prep_kernel(has_vres, S, *refs):
    if has_vres:
        (pm_ref, pl_ref, pm_prev_ref, pl_prev_ref, mu_m_ref, mu_l_ref, w0_ref, w2_ref, a0_ref, a2_ref,
         g2_ref, kk_ref, ka_ref, vfirst_ref, v0_ref, v1_ref, v2_ref,
         r_o, lw_o, k_o, v_o, kkn_o, a_o, g_o) = refs
    else:
        (pm_ref, pl_ref, pm_prev_ref, pl_prev_ref, mu_m_ref, mu_l_ref, w0_ref, w2_ref, a0_ref, a2_ref,
         g2_ref, kk_ref, ka_ref,
         r_o, lw_o, k_o, v_o, kkn_o, a_o, g_o) = refs
    i = pl.program_id(0)
    tm = pm_ref.shape[0]
    C = RWKV_DIM
    row = lax.broadcasted_iota(jnp.int32, (tm, 1), 0)
    seq_start = (i * tm) % S == 0

    def shifted(cur_ref, prev_ref, mu_ref):
        x = cur_ref[...]
        prev_row = jnp.where(seq_start, 0.0, prev_ref[7:8, :])
        xs = jnp.where(row == 0, prev_row, pltpu.roll(x, 1, 0))
        return x + (xs - x) * mu_ref[...]

    pm = shifted(pm_ref, pm_prev_ref, mu_m_ref)
    plo = shifted(pl_ref, pl_prev_ref, mu_l_ref)
    r = pm[:, :C]
    k = pm[:, C:2 * C]
    v = pm[:, 2 * C:]
    wd = plo[:, :RWKV_DECAY_LORA]
    ad = plo[:, RWKV_DECAY_LORA:RWKV_DECAY_LORA + RWKV_A_LORA]
    gd = plo[:, RWKV_DECAY_LORA + RWKV_A_LORA:RWKV_LORA]
    w = -jax.nn.softplus(-(w0_ref[...] + _dot(jnp.tanh(wd).astype(BF16), w2_ref[...]))) - 0.5
    a = jax.nn.sigmoid(a0_ref[...] + _dot(ad.astype(BF16), a2_ref[...]))
    g = _dot(jax.nn.sigmoid(gd).astype(BF16), g2_ref[...])
    if has_vres:
        lo = _dot(_dot(v.astype(BF16), v1_ref[...]).astype(BF16), v2_ref[...])
        v = v + (vfirst_ref[...] - v) * jax.nn.sigmoid(v0_ref[...] + lo)
    kk = k * kk_ref[...]
    hid_r = lax.broadcasted_iota(jnp.int32, (LANES, LANES), 0) // RWKV_HEAD_DIM
    hid_c = lax.broadcasted_iota(jnp.int32, (LANES, LANES), 1) // RWKV_HEAD_DIM
    ones_bd = (hid_r == hid_c).astype(F32)
    for pp in range(C // LANES):
        sl = slice(pp * LANES, (pp + 1) * LANES)
        kkp = kk[:, sl]
        ss = _dot(kkp * kkp, ones_bd, precision=HIGHEST)
        kkn_o[:, sl] = kkp * lax.rsqrt(jnp.maximum(ss, 1e-24))
    r_o[...] = r
    lw_o[...] = -jnp.exp(w)
    k_o[...] = k * (1.0 + (a - 1.0) * ka_ref[...])
    v_o[...] = v
    a_o[...] = a
    g_o[...] = g


def rwkv_prep(proj, plora, S, mu, w0, w2, a0, a2, g2, k_k, k_a, v_first, v_lora, tm=256):
    T = proj.shape[0]
    C = RWKV_DIM
    LP = plora.shape[1]
    has_vres = v_lora is not None
    mu_m = mu[:3 * C].reshape(1, 3 * C)
    mu_l = jnp.zeros((1, LP), F32).at[0, :RWKV_LORA].set(mu[3 * C:])
    row = lambda z: z.reshape(1, -1)
    full = lambda z: pl.BlockSpec(z.shape, lambda i: (0,) * z.ndim)
    args = [proj, plora, proj, plora, mu_m, mu_l, row(w0), w2.astype(BF16), row(a0), a2.astype(BF16),
            g2.astype(BF16), row(k_k), row(k_a)]
    in_specs = [pl.BlockSpec((tm, 3 * C), lambda i: (i, 1)),
                pl.BlockSpec((tm, LP), lambda i: (i, 0)),
                pl.BlockSpec((8, 3 * C), lambda i: (jnp.maximum(i * (tm // 8) - 1, 0), 1)),
                pl.BlockSpec((8, LP), lambda i: (jnp.maximum(i * (tm // 8) - 1, 0), 0))]
    in_specs += [full(z) for z in args[4:]]
    if has_vres:
        v0, v1, v2 = v_lora
        extra = [v_first, row(v0), v1.astype(BF16), v2.astype(BF16)]
        args += extra
        in_specs += [pl.BlockSpec((tm, C), lambda i: (i, 0))] + [full(z) for z in extra[1:]]
    out_spec = pl.BlockSpec((tm, C), lambda i: (i, 0))
    return pl.pallas_call(
        functools.partial(_rwkv_prep_kernel, has_vres, S),
        grid=(T // tm,),
        in_specs=in_specs,
        out_specs=[out_spec] * 7,
        out_shape=[jax.ShapeDtypeStruct((T, C), F32)] * 7,
        compiler_params=_cparams("parallel"),
        name="rwkv_prep",
    )(*args)


def _split3_bf16(x):
    x0 = x.astype(BF16)
    r1 = x - x0.astype(F32)
    x1 = r1.astype(BF16)
    x2 = (r1 - x1.astype(F32)).astype(BF16)
    return x0, x1, x2


def _sum01_left(m01, x):
    x0, x1, x2 = _split3_bf16(x)
    return _dot(m01, x0) + (_dot(m01, x1) + _dot(m01, x2))


def _sum01_right(x, m01):
    x0, x1, x2 = _split3_bf16(x)
    return _dot(x0, m01) + (_dot(x1, m01) + _dot(x2, m01))


_NN = (((1,), (0,)), ((), ()))
_NT = (((1,), (1,)), ((), ()))
_TN = (((0,), (0,)), ((), ()))


def _mm(a, b, passes, dims=_NN):
    dg = lambda x, y: lax.dot_general(x, y, dims, preferred_element_type=F32)
    if passes == 1:
        return dg(a.astype(BF16), b.astype(BF16))
    a_hi = a.astype(BF16)
    a_lo = (a - a_hi.astype(F32)).astype(BF16)
    b_hi = b.astype(BF16)
    b_lo = (b - b_hi.astype(F32)).astype(BF16)
    return dg(a_hi, b_hi) + (dg(a_hi, b_lo) + dg(a_lo, b_hi))


RWKV_PASSES = dict(gram=1, inv=1, apply=1, state=1)


def _rwkv_scan_kernel(r_ref, lw_ref, k_ref, v_ref, kk_ref, a_ref, g_ref, rk_ref, lnw_ref, lnb_ref,
                      o_ref, state_ref):
    c = pl.program_id(1)
    CH = RWKV_CHUNK
    N = RWKV_HEAD_DIM
    P2 = 2 * CH
    NP = RWKV_DIM // LANES
    pg, pi, pa, ps = (RWKV_PASSES[n] for n in ("gram", "inv", "apply", "state"))

    @pl.when(c == 0)
    def _():
        state_ref[...] = jnp.zeros_like(state_ref)

    lane = lax.broadcasted_iota(jnp.int32, (1, LANES), 1)
    head0 = lane < N
    ri = lax.broadcasted_iota(jnp.int32, (P2, P2), 0)
    ci = lax.broadcasted_iota(jnp.int32, (P2, P2), 1)
    same_head = (ri // CH) == (ci // CH)
    strict = (ri % CH) > (ci % CH)
    incl = (ri % CH) >= (ci % CH)
    eye = (ri == ci).astype(F32)
    ones_bd = same_head.astype(BF16)
    ti = lax.broadcasted_iota(jnp.int32, (CH, CH), 0)
    si = lax.broadcasted_iota(jnp.int32, (CH, CH), 1)
    tril_incl = (ti >= si).astype(BF16)

    def stack(x):
        return jnp.concatenate([jnp.where(head0, x, 0.0), jnp.where(head0, 0.0, x)], axis=0)

    pairs = range(NP)
    sls = [slice(p * LANES, (p + 1) * LANES) for p in pairs]
    r = [r_ref[:, sl] for sl in sls]
    k = [k_ref[:, sl] for sl in sls]
    v = [v_ref[:, sl] for sl in sls]
    kk = [kk_ref[:, sl] for sl in sls]
    lw = [lw_ref[:, sl] for sl in sls]
    cum = [_sum01_left(tril_incl, lw[p]) for p in pairs]
    cum_end = [cum[p][CH - 1:CH, :] for p in pairs]
    b = [kk[p] * a_ref[:, sls[p]] for p in pairs]
    e_neg = [jnp.exp(-cum[p]) for p in pairs]
    e_end = [jnp.exp(cum_end[p] - cum[p]) for p in pairs]
    A_st = [stack(-kk[p] * jnp.exp(cum[p] - lw[p])) for p in pairs]
    R_st = [stack(r[p] * jnp.exp(cum[p])) for p in pairs]
    BK = [jnp.concatenate([stack(b[p] * e_neg[p]), stack(k[p] * e_neg[p])], axis=0) for p in pairs]
    Bend_st = [stack(b[p] * e_end[p]) for p in pairs]
    Kend_st = [stack(k[p] * e_end[p]) for p in pairs]
    V_st = [stack(v[p]) for p in pairs]

    G = [_mm(jnp.concatenate([A_st[p], R_st[p]], axis=0), BK[p], pg, _NT) for p in pairs]
    Lab = [jnp.where(strict, G[p][:P2, :P2], 0.0) for p in pairs]
    Lak = [jnp.where(strict, G[p][:P2, P2:], 0.0) for p in pairs]
    Mrb = [jnp.where(incl, G[p][P2:, :P2], 0.0) for p in pairs]
    Mrk = [jnp.where(incl, G[p][P2:, P2:], 0.0) for p in pairs]

    Tinv = [eye + Lab[p] for p in pairs]
    Lp = Lab
    for _ in range(int(math.log2(CH)) - 1):
        Lp = [_mm(Lp[p], Lp[p], pi) for p in pairs]
        Tinv = [Tinv[p] + _mm(Tinv[p], Lp[p], pi) for p in pairs]

    LakV = [_mm(Lak[p], V_st[p], pa) for p in pairs]
    AU = [_mm(Tinv[p], jnp.concatenate([A_st[p], LakV[p]], axis=1), pa) for p in pairs]
    MM = [_mm(Mrb[p], AU[p], pa) for p in pairs]
    MV = [_mm(Mrk[p], V_st[p], pa) for p in pairs]
    BT = [_mm(Bend_st[p], AU[p], pa, _TN) for p in pairs]
    KV = [_mm(Kend_st[p], V_st[p], pa, _TN) for p in pairs]
    Rhat = [R_st[p] + MM[p][:, :LANES] for p in pairs]
    Mmat = [eye * jnp.exp(cum_end[p]) + BT[p][:, :LANES] for p in pairs]

    RS = [_mm(jnp.concatenate([Rhat[p], Mmat[p]], axis=0), state_ref[p], ps) for p in pairs]
    for p in pairs:
        state_ref[p] = jnp.where(same_head, RS[p][P2:] + BT[p][:, LANES:] + KV[p], 0.0)
    Y_st = [RS[p][:P2] + MM[p][:, LANES:] + MV[p] for p in pairs]
    y = [Y_st[p][:CH] + Y_st[p][CH:] for p in pairs]

    sums = [_sum01_right(jnp.concatenate([y[p], r[p] * k[p] * rk_ref[:, sls[p]]], axis=0), ones_bd)
            for p in pairs]
    yc = [y[p] - sums[p][:CH] * (1.0 / N) for p in pairs]
    var = [_sum01_right(yc[p] * yc[p], ones_bd) * (1.0 / N) for p in pairs]
    for p in pairs:
        yn = yc[p] * lax.rsqrt(var[p] + RWKV_GN_EPS) * lnw_ref[:, sls[p]] + lnb_ref[:, sls[p]]
        o_ref[:, sls[p]] = ((yn + sums[p][CH:] * v[p]) * g_ref[:, sls[p]]).astype(o_ref.dtype)


def rwkv_scan(r, lw, k, v, kk, a, g, r_k, ln_w, ln_b, B, S):
    T, C = r.shape
    CH = RWKV_CHUNK
    nc = S // CH
    blk = pl.BlockSpec((CH, C), lambda b, c: (b * nc + c, 0))
    par = pl.BlockSpec((1, C), lambda b, c: (0, 0))
    return pl.pallas_call(
        _rwkv_scan_kernel,
        grid=(B, nc),
        in_specs=[blk] * 7 + [par] * 3,
        out_specs=blk,
        out_shape=jax.ShapeDtypeStruct((T, C), BF16),
        scratch_shapes=[pltpu.VMEM((C // LANES, LANES, LANES), F32)],
        compiler_params=_cparams("parallel", "arbitrary"),
        name="rwkv_scan",
    )(r, lw, k, v, kk, a, g, r_k.reshape(1, C), ln_w.reshape(1, C), ln_b.reshape(1, C))


def _rope_pairs(x, cos2, sin2):
    lane = lax.broadcasted_iota(jnp.int32, (1, LANES), 1)
    first_half = (lane % MLA_ROPE_DIM) < (MLA_ROPE_DIM // 2)
    partner = jnp.where(first_half, pltpu.roll(x, LANES - MLA_ROPE_DIM // 2, 1),
                        pltpu.roll(x, MLA_ROPE_DIM // 2, 1))
    return x * cos2 + partner * sin2


def _mla_cproj_kernel(h_ref, w_ref, cos_ref, sin_ref, c_ref, kr_ref):
    acc = _dot(h_ref[...], w_ref[...])
    NC = MLA_Q_LORA + MLA_KV_LORA
    c_ref[...] = acc[:, :NC]
    kr_ref[...] = _rope_pairs(acc[:, NC:], cos_ref[...], sin_ref[...]).astype(kr_ref.dtype)


def mla_cproj(h16, w_in, cos2, sin2, S, tm=512):
    T, D = h16.shape
    NC = MLA_Q_LORA + MLA_KV_LORA
    w = jnp.concatenate([w_in, w_in[:, NC:]], axis=1).astype(BF16)
    ns = S // tm
    return pl.pallas_call(
        _mla_cproj_kernel,
        grid=(T // tm,),
        in_specs=[pl.BlockSpec((tm, D), lambda i: (i, 0)),
                  pl.BlockSpec((D, NC + LANES), lambda i: (0, 0)),
                  pl.BlockSpec((tm, LANES), lambda i: (i % ns, 0)),
                  pl.BlockSpec((tm, LANES), lambda i: (i % ns, 0))],
        out_specs=[pl.BlockSpec((tm, NC), lambda i: (i, 0)),
                   pl.BlockSpec((tm, LANES), lambda i: (i, 0))],
        out_shape=[jax.ShapeDtypeStruct((T, NC), F32), jax.ShapeDtypeStruct((T, LANES), BF16)],
        compiler_params=_cparams("parallel"),
        name="mla_cproj",
    )(h16, w, cos2, sin2)


def _rms_up_kernel(rope_from, out_scale, c_ref, g_ref, w_ref, cos_ref, sin_ref, o_ref):
    x = c_ref[...]
    ms = jnp.mean(x * x, axis=-1, keepdims=True)
    cn = (x * lax.rsqrt(ms + RMS_EPS) * g_ref[...]).astype(BF16)
    N = w_ref.shape[1]
    plain = N if rope_from is None else rope_from
    TN = 1024
    for n0 in range(0, plain, TN):
        acc = _dot(cn, w_ref[:, n0:n0 + TN])
        if out_scale != 1.0:
            acc = acc * out_scale
        o_ref[:, n0:n0 + TN] = acc.astype(o_ref.dtype)
    if rope_from is not None:
        cos2, sin2 = cos_ref[...], sin_ref[...]
        acc = _dot(cn, w_ref[:, rope_from:]) * out_scale
        for gidx in range(acc.shape[1] // LANES):
            sl = slice(gidx * LANES, (gidx + 1) * LANES)
            o_ref[:, rope_from + gidx * LANES:rope_from + (gidx + 1) * LANES] = (
                _rope_pairs(acc[:, sl], cos2, sin2).astype(o_ref.dtype))


def rms_up_proj(c, col_block, gain, w, cos2, sin2, S, rope_from, out_scale=1.0, tm=512):
    T = c.shape[0]
    K, N = w.shape
    ns = S // tm
    return pl.pallas_call(
        functools.partial(_rms_up_kernel, rope_from, out_scale),
        grid=(T // tm,),
        in_specs=[pl.BlockSpec((tm, K), lambda i: (i, col_block)),
                  pl.BlockSpec((1, K), lambda i: (0, 0)),
                  pl.BlockSpec((K, N), lambda i: (0, 0)),
                  pl.BlockSpec((tm, LANES), lambda i: (i % ns, 0)),
                  pl.BlockSpec((tm, LANES), lambda i: (i % ns, 0))],
        out_specs=pl.BlockSpec((tm, N), lambda i: (i, 0)),
        out_shape=jax.ShapeDtypeStruct((T, N), BF16),
        compiler_params=_cparams("parallel"),
        name="rms_up_proj",
    )(c, gain.reshape(1, K), w, cos2, sin2)


MLA_GROUP = 4
MLA_SCORE_SCALE = (MLA_NOPE_DIM + MLA_ROPE_DIM) ** -0.5 * math.log2(math.e)


def _mla_attn_kernel(qn_ref, qr_ref, kv_ref, kr_ref, o_ref, vt_ref):
    qi = pl.program_id(2)
    TQ = qn_ref.shape[0]
    S = kv_ref.shape[0]
    G = MLA_GROUP
    lane = lax.broadcasted_iota(jnp.int32, (1, LANES), 1)

    @pl.when(qi == 0)
    def _():
        for j in range(G):
            for n in range(S // TQ):
                rows = slice(n * TQ, (n + 1) * TQ)
                v_t = kv_ref[rows, (2 * j + 1) * LANES:(2 * j + 2) * LANES]
                vt_ref[j, :, rows] = v_t.astype(F32).T.astype(BF16)

    key = lax.broadcasted_iota(jnp.int32, (TQ, TQ), 0)
    qry = lax.broadcasted_iota(jnp.int32, (TQ, TQ), 1)
    qts = []
    for j in range(G):
        own = (lane // MLA_ROPE_DIM) == (j % 2)
        qr = qr_ref[:, (j // 2) * LANES:(j // 2 + 1) * LANES]
        qc = jnp.concatenate([qn_ref[:, j * LANES:(j + 1) * LANES],
                              jnp.where(own, qr, jnp.zeros_like(qr))], axis=1)
        qts.append(qc.astype(F32).T.astype(BF16))

    def scores(j, off):
        kc = jnp.concatenate([kv_ref[pl.ds(off, TQ), 2 * j * LANES:(2 * j + 1) * LANES],
                              kr_ref[pl.ds(off, TQ), :]], axis=1)
        return _dot(kc, qts[j])

    def softmax_step(t, m_run, l_run, diag):
        if diag:
            t = jnp.where(key <= qry, t, NEG_INF)
        m_new = jnp.maximum(m_run, jnp.max(_col_groups(t, jnp.max), axis=0, keepdims=True))
        alpha = jnp.exp2(m_run - m_new)
        pr = jnp.exp2(t - m_new)
        l_new = alpha * l_run + _col_groups(pr, jnp.sum)
        return m_new, l_new, alpha, pr.astype(BF16)

    def weighted_values(j, off, acc, alpha, pr):
        return alpha * acc + _dot(vt_ref[j, :, pl.ds(off, TQ)], pr)

    def body(n, carry):
        off = pl.multiple_of(n * TQ, TQ)
        out = []
        for j in range(G):
            m_run, l_run, acc = carry[j]
            m_new, l_new, alpha, pr = softmax_step(scores(j, off), m_run, l_run, False)
            out.append((m_new, l_new, weighted_values(j, off, acc, alpha, pr)))
        return tuple(out)

    carry = tuple((jnp.full((1, TQ), NEG_INF, F32), jnp.zeros((8, TQ), F32),
                   jnp.zeros((LANES, TQ), F32)) for j in range(G))
    carry = lax.fori_loop(0, qi, body, carry)
    off_d = pl.multiple_of(qi * TQ, TQ)
    for j in range(G):
        m_run, l_run, acc = carry[j]
        m_new, l_new, alpha, pr = softmax_step(scores(j, off_d), m_run, l_run, True)
        acc = weighted_values(j, off_d, acc, alpha, pr)
        out_t = acc / jnp.sum(l_new, axis=0, keepdims=True)
        o_ref[:, j * LANES:(j + 1) * LANES] = out_t.T.astype(o_ref.dtype)


def mla_attention(q, kv, kr2, B, S, tq=512):
    T = B * S
    H, G = MLA_HEADS, MLA_GROUP
    nq = S // tq
    n_nope = H // G
    return pl.pallas_call(
        _mla_attn_kernel,
        grid=(B, H // G, nq),
        in_specs=[pl.BlockSpec((tq, G * LANES), lambda b, g, i: (b * nq + i, g)),
                  pl.BlockSpec((tq, G // 2 * LANES), lambda b, g, i: (b * nq + i, 2 * n_nope + g)),
                  pl.BlockSpec((S, 2 * G * LANES), lambda b, g, i: (b, g)),
                  pl.BlockSpec((S, LANES), lambda b, g, i: (b, 0))],
        out_specs=pl.BlockSpec((tq, G * LANES), lambda b, g, i: (b * nq + i, g)),
        out_shape=jax.ShapeDtypeStruct((T, H * MLA_V_DIM), BF16),
        scratch_shapes=[pltpu.VMEM((G, LANES, S), BF16)],
        compiler_params=_cparams("parallel", "parallel", "arbitrary"),
        name="mla_attention",
    )(q, q, kv, kr2)


def _rope_tables(S):
    half = MLA_ROPE_DIM // 2
    inv = ROPE_THETA ** (-jnp.arange(0, MLA_ROPE_DIM, 2, dtype=F32) / MLA_ROPE_DIM)
    ang = jnp.arange(S, dtype=F32)[:, None] * inv[None, :]
    cos, sin = jnp.cos(ang), jnp.sin(ang)
    reps = LANES // MLA_ROPE_DIM
    cos2 = jnp.tile(jnp.concatenate([cos, cos], axis=1), (1, reps))
    sin2 = jnp.tile(jnp.concatenate([-sin, sin], axis=1), (1, reps))
    assert cos2.shape == (S, LANES) and half * 2 * reps == LANES
    return cos2, sin2


def kernel(x, ev_w_in, ev_w_out, rw_mu, rw_w0, rw_w2, rw_a0, rw_a2, rw_g2, rw_k_k, rw_k_a, rw_r_k,
           rw_ln_w, rw_ln_b, rw_v0, rw_v1, rw_v2, od_w_in, od_q_norm, od_kv_norm, od_w_uq, od_w_ukv,
           od_w_out, ln_mix_g, ln_mix_b, ln_ffn_g, ln_ffn_b, moe_w_r, moe_b_r, moe_w1, moe_b1,
           moe_w2, moe_b2):
    B, S, D = x.shape
    T = B * S
    h32 = x.reshape(T, D)
    h16 = h32.astype(BF16)
    cos2, sin2 = _rope_tables(S)
    n_main = 3 * MOBA_DIM + 3 * RWKV_DIM
    lora_pad = -(-RWKV_LORA // LANES) * LANES
    qd = MLA_NOPE_DIM + MLA_ROPE_DIM
    perm = np.concatenate([
        (np.arange(MLA_HEADS)[:, None] * qd + np.arange(MLA_NOPE_DIM)[None, :]).reshape(-1),
        (np.arange(MLA_HEADS)[:, None] * qd + MLA_NOPE_DIM + np.arange(MLA_ROPE_DIM)[None, :]).reshape(-1)])
    v_first = None
    for layer in range(DEPTH):
        j = layer // 2
        if layer % 2 == 0:
            w_in = ev_w_in[j]
            w_main = w_in[:, :n_main].astype(BF16)
            w_lora = jnp.zeros((D, lora_pad), BF16).at[:, :RWKV_LORA].set(w_in[:, n_main:].astype(BF16))
            proj = matmul(h16, w_main, F32)
            plora = matmul(h16, w_lora, F32)
            a_out = moba_attention(proj, B, S)
            v_lora = None if j == 0 else (rw_v0[j - 1], rw_v1[j - 1], rw_v2[j - 1])
            r, lw, k, v, kk, a, g = rwkv_prep(proj, plora, S, rw_mu[j], rw_w0[j], rw_w2[j], rw_a0[j],
                                             rw_a2[j], rw_g2[j], rw_k_k[j], rw_k_a[j], v_first, v_lora)
            if j == 0:
                v_first = v
            b_out = rwkv_scan(r, lw, k, v, kk, a, g, rw_r_k[j], rw_ln_w[j], rw_ln_b[j], B, S)
            w_out = ev_w_out[j].astype(BF16)
            h32, h16 = proj_residual_ln([a_out, b_out], [w_out[:MOBA_DIM], w_out[MOBA_DIM:]], h32,
                                        ln_mix_g[layer], ln_mix_b[layer])
        else:
            c, kr2 = mla_cproj(h16, od_w_in[j], cos2, sin2, S)
            q = rms_up_proj(c, 0, od_q_norm[j], od_w_uq[j][:, perm].astype(BF16), cos2, sin2, S,
                            rope_from=MLA_HEADS * MLA_NOPE_DIM, out_scale=MLA_SCORE_SCALE)
            kv = rms_up_proj(c, 1, od_kv_norm[j], od_w_ukv[j].astype(BF16), cos2, sin2, S,
                             rope_from=None)
            o = mla_attention(q, kv, kr2, B, S)
            h32, h16 = proj_residual_ln([o], [od_w_out[j].astype(BF16)], h32,
                                        ln_mix_g[layer], ln_mix_b[layer])
        h32, h16 = moe_layer(h32, h16, moe_w_r[layer], moe_b_r[layer], moe_w1, moe_b1[layer], moe_w2,
                             moe_b2[layer], ln_ffn_g[layer], ln_ffn_b[layer], layer)
    return h32.reshape(B, S, D)
```

```python
import functools
import math

import jax
import jax.numpy as jnp
import numpy as np
from jax import lax
from jax.experimental import pallas as pl
from jax.experimental.pallas import tpu as pltpu

F32 = jnp.float32
BF16 = jnp.bfloat16
HIGHEST = lax.Precision.HIGHEST

DEPTH = 4
MOBA_HEADS = 16
MOBA_HEAD_DIM = 64
MOBA_DIM = MOBA_HEADS * MOBA_HEAD_DIM
MOBA_BLOCK = 256
MOBA_TOPK = 3
RWKV_HEADS = 16
RWKV_HEAD_DIM = 64
RWKV_DIM = RWKV_HEADS * RWKV_HEAD_DIM
RWKV_DECAY_LORA = 64
RWKV_A_LORA = 64
RWKV_G_LORA = 160
RWKV_LORA = RWKV_DECAY_LORA + RWKV_A_LORA + RWKV_G_LORA
RWKV_GN_EPS = 64e-5
RWKV_CHUNK = 64
MLA_HEADS = 16
MLA_Q_LORA = 512
MLA_KV_LORA = 512
MLA_NOPE_DIM = 128
MLA_ROPE_DIM = 64
MLA_V_DIM = 128
ROPE_THETA = 10000.0
N_EXPERTS = 32
TOP_K = 4
D_EXPERT = 1024
SWIGLU_ALPHA = 1.702
SWIGLU_LIMIT = 7.0
MOE_ROWS = 256
DEEPNORM_ALPHA = (2 * DEPTH) ** 0.25
LN_EPS = 1e-5
RMS_EPS = 1e-6
NEG_INF = -1e30

LANES = 128
VMEM_LIMIT = 56 * 1024 * 1024


def _cparams(*sem):
    return pltpu.CompilerParams(dimension_semantics=sem, vmem_limit_bytes=VMEM_LIMIT)


def _dot(a, b, precision=None):
    return jnp.dot(a, b, preferred_element_type=F32, precision=precision)


def _mm_kernel(a_ref, w_ref, o_ref):
    o_ref[...] = _dot(a_ref[...], w_ref[...]).astype(o_ref.dtype)


def matmul(a, w, out_dtype, tm=1024, tn=512):
    M, K = a.shape
    N = w.shape[1]
    tm, tn = min(tm, M), min(tn, N)
    assert M % tm == 0 and N % tn == 0
    return pl.pallas_call(
        _mm_kernel,
        grid=(M // tm, N // tn),
        in_specs=[pl.BlockSpec((tm, K), lambda i, j: (i, 0)),
                  pl.BlockSpec((K, tn), lambda i, j: (0, j))],
        out_specs=pl.BlockSpec((tm, tn), lambda i, j: (i, j)),
        out_shape=jax.ShapeDtypeStruct((M, N), out_dtype),
        compiler_params=_cparams("parallel", "parallel"),
        name="matmul",
    )(a, w)


def _layer_norm_rows(z, g, b):
    mu = jnp.mean(z, axis=-1, keepdims=True)
    zc = z - mu
    var = jnp.mean(zc * zc, axis=-1, keepdims=True)
    return zc * lax.rsqrt(var + LN_EPS) * g + b


def _proj_ln_kernel(n_in, *refs):
    a_refs = refs[:n_in]
    w_refs = refs[n_in:2 * n_in]
    h_ref, g_ref, b_ref, o32_ref, o16_ref = refs[2 * n_in:]
    acc = DEEPNORM_ALPHA * h_ref[...]
    for a_ref, w_ref in zip(a_refs, w_refs):
        acc = acc + _dot(a_ref[...], w_ref[...])
    y = _layer_norm_rows(acc, g_ref[...], b_ref[...])
    o32_ref[...] = y
    o16_ref[...] = y.astype(BF16)


def proj_residual_ln(a_list, w_list, h, g, b, tm=256):
    M, D = h.shape
    n_in = len(a_list)
    in_specs = ([pl.BlockSpec((tm, a.shape[1]), lambda i: (i, 0)) for a in a_list]
                + [pl.BlockSpec(w.shape, lambda i: (0, 0)) for w in w_list]
                + [pl.BlockSpec((tm, D), lambda i: (i, 0)),
                   pl.BlockSpec((1, D), lambda i: (0, 0)),
                   pl.BlockSpec((1, D), lambda i: (0, 0))])
    return pl.pallas_call(
        functools.partial(_proj_ln_kernel, n_in),
        grid=(M // tm,),
        in_specs=in_specs,
        out_specs=[pl.BlockSpec((tm, D), lambda i: (i, 0)),
                   pl.BlockSpec((tm, D), lambda i: (i, 0))],
        out_shape=[jax.ShapeDtypeStruct((M, D), F32), jax.ShapeDtypeStruct((M, D), BF16)],
        compiler_params=_cparams("parallel"),
        name="proj_residual_ln",
    )(*a_list, *w_list, h, g.reshape(1, D), b.reshape(1, D))


def _combine_ln_kernel(y_ref, gate_ref, h_ref, g_ref, b_ref, o32_ref, o16_ref):
    acc = DEEPNORM_ALPHA * h_ref[...]
    gates = gate_ref[...]
    for k in range(TOP_K):
        acc = acc + y_ref[k].astype(F32) * gates[:, k:k + 1]
    y = _layer_norm_rows(acc, g_ref[...], b_ref[...])
    o32_ref[...] = y
    o16_ref[...] = y.astype(BF16)


def combine_residual_ln(y4, gates, h, g, b, tm=256):
    M, D = h.shape
    return pl.pallas_call(
        _combine_ln_kernel,
        grid=(M // tm,),
        in_specs=[pl.BlockSpec((TOP_K, tm, D), lambda i: (0, i, 0)),
                  pl.BlockSpec((tm, TOP_K), lambda i: (i, 0)),
                  pl.BlockSpec((tm, D), lambda i: (i, 0)),
                  pl.BlockSpec((1, D), lambda i: (0, 0)),
                  pl.BlockSpec((1, D), lambda i: (0, 0))],
        out_specs=[pl.BlockSpec((tm, D), lambda i: (i, 0)),
                   pl.BlockSpec((tm, D), lambda i: (i, 0))],
        out_shape=[jax.ShapeDtypeStruct((M, D), F32), jax.ShapeDtypeStruct((M, D), BF16)],
        compiler_params=_cparams("parallel"),
        name="combine_residual_ln",
    )(y4, gates, h, g.reshape(1, D), b.reshape(1, D))


def _router_kernel(h_ref, w_ref, b_ref, gate_ref, idx_ref, cnt_ref, base_ref):
    i = pl.program_id(0)
    tm = h_ref.shape[0]

    @pl.when(i == 0)
    def _():
        base_ref[...] = jnp.zeros_like(base_ref)

    lane = lax.broadcasted_iota(jnp.int32, (tm, LANES), 1)
    logits = _mm(h_ref[...], w_ref[...], 3) + b_ref[...]
    val = jnp.where(lane < N_EXPERTS, logits, -jnp.inf)
    tops, idxs = [], []
    member = jnp.zeros((tm, LANES), F32)
    lane_f = lane.astype(F32)
    for _ in range(TOP_K):
        mx = jnp.max(val, axis=-1, keepdims=True)
        ix = jnp.min(jnp.where(val == mx, lane_f, float(LANES)), axis=-1, keepdims=True).astype(jnp.int32)
        hit = lane == ix
        member = member + jnp.where(hit, 1.0, 0.0)
        val = jnp.where(hit, -jnp.inf, val)
        tops.append(mx)
        idxs.append(ix)
    ex = [jnp.exp(t - tops[0]) for t in tops]
    den = ex[0] + ex[1] + ex[2] + ex[3]
    ti = lax.broadcasted_iota(jnp.int32, (tm, tm), 0)
    si = lax.broadcasted_iota(jnp.int32, (tm, tm), 1)
    csum = base_ref[...] + _dot((ti >= si).astype(BF16), member.astype(BF16))
    gate_out = jnp.zeros((tm, LANES), F32)
    idx_out = jnp.zeros((tm, LANES), jnp.int32)
    for k in range(TOP_K):
        rank = jnp.sum(jnp.where(lane == idxs[k], csum - 1.0, 0.0), axis=-1, keepdims=True)
        gate_out = jnp.where(lane == k, ex[k] / den, gate_out)
        idx_out = jnp.where(lane == k, idxs[k], idx_out)
        idx_out = jnp.where(lane == TOP_K + k, rank.astype(jnp.int32), idx_out)
    gate_ref[...] = gate_out
    idx_ref[...] = idx_out
    base_ref[...] = csum[tm - 1:tm, :]
    cnt_ref[...] = csum[tm - 1:tm, :].astype(jnp.int32)


def route_tokens(h, w_r, b_r, tm=512):
    M, D = h.shape
    tm = min(tm, M)
    w = jnp.zeros((D, LANES), F32).at[:, :N_EXPERTS].set(w_r)
    b = jnp.zeros((1, LANES), F32).at[0, :N_EXPERTS].set(b_r)
    gate, idx, cnt = pl.pallas_call(
        _router_kernel,
        grid=(M // tm,),
        in_specs=[pl.BlockSpec((tm, D), lambda i: (i, 0)),
                  pl.BlockSpec((D, LANES), lambda i: (0, 0)),
                  pl.BlockSpec((1, LANES), lambda i: (0, 0))],
        out_specs=[pl.BlockSpec((tm, LANES), lambda i: (i, 0)),
                   pl.BlockSpec((tm, LANES), lambda i: (i, 0)),
                   pl.BlockSpec((1, LANES), lambda i: (0, 0))],
        out_shape=[jax.ShapeDtypeStruct((M, LANES), F32), jax.ShapeDtypeStruct((M, LANES), jnp.int32),
                   jax.ShapeDtypeStruct((1, LANES), jnp.int32)],
        scratch_shapes=[pltpu.VMEM((1, LANES), F32)],
        compiler_params=_cparams("arbitrary"),
        name="route_tokens",
    )(h, w, b)
    idx_t = idx[:, :2 * TOP_K].T
    return gate[:, :TOP_K], idx_t[:TOP_K], idx_t[TOP_K:], cnt[0, :N_EXPERTS]


MOE_W_CHUNKS = 4


def _expert_kernel(cb_ref, ce_ref, le_ref, lc_ref, fl_ref, x_ref, w1c_ref, w2c_ref, b1_ref, b2_ref,
                   o_ref, w1a_ref, w2a_ref, w1b_ref, w2b_ref):
    s = pl.program_id(0)
    fl = fl_ref[s]
    comp = (fl & 1) == 1
    load = (fl & 2) == 2
    par = (fl & 4) == 4
    c = lc_ref[s]
    C1 = w1c_ref.shape[0]
    C2 = w2c_ref.shape[0]

    def cast_chunk(w1_dst, w2_dst):
        w1_dst[pl.ds(pl.multiple_of(c * C1, C1), C1), :] = w1c_ref[...].astype(BF16)
        w2_dst[pl.ds(pl.multiple_of(c * C2, C2), C2), :] = w2c_ref[...].astype(BF16)

    def ffn(w1_src, w2_src):
        hgu = _dot(x_ref[...], w1_src[...]) + b1_ref[0]
        gate = jnp.minimum(hgu[:, :D_EXPERT], SWIGLU_LIMIT)
        up = jnp.clip(hgu[:, D_EXPERT:], -SWIGLU_LIMIT, SWIGLU_LIMIT)
        act = gate * jax.nn.sigmoid(SWIGLU_ALPHA * gate) * (up + 1.0)
        o_ref[...] = (_dot(act.astype(BF16), w2_src[...]) + b2_ref[0]).astype(o_ref.dtype)

    bufs = ((w1a_ref, w2a_ref), (w1b_ref, w2b_ref))
    for p in (0, 1):
        use, fill = bufs[p], bufs[1 - p]
        in_phase = par == (p == 1)

        @pl.when(comp & in_phase)
        def _():
            cast_chunk(*fill)
            ffn(*use)

        @pl.when(load & jnp.logical_not(comp) & in_phase)
        def _():
            cast_chunk(*fill)

    @pl.when((fl & 8) == 8)
    def _():
        o_ref[...] = jnp.zeros_like(o_ref)


def _prefix_sum(x):
    n = x.shape[0]
    tri = jnp.arange(n)[None, :] <= jnp.arange(n)[:, None]
    return jnp.sum(jnp.where(tri, x[None, :], 0), axis=1).astype(jnp.int32)


def _expert_schedule(nblk, n_blocks):
    E, NC = N_EXPERTS, MOE_W_CHUNKS
    ph_ids = jnp.arange(E + 1, dtype=jnp.int32)
    prev_n = jnp.concatenate([jnp.zeros((1,), jnp.int32), nblk.astype(jnp.int32)])
    plen = jnp.where(ph_ids == 0, NC, jnp.where(ph_ids == E, prev_n, jnp.maximum(prev_n, NC)))
    pend = _prefix_sum(plen)
    pstart = pend - plen
    n_steps = n_blocks + E * NC
    s = jnp.arange(n_steps, dtype=jnp.int32)
    ph = jnp.minimum(jnp.sum(pend[None, :] <= s[:, None], axis=1), E).astype(jnp.int32)
    t = s - pstart[ph]
    comp = (ph >= 1) & (t < prev_n[ph])
    load = (ph < E) & (t < NC)
    n_used = jnp.sum(nblk).astype(jnp.int32)
    tail_blk = n_used + (s - pend[E])
    fill = (s >= pend[E]) & (tail_blk < n_blocks)
    comp_blk = jnp.where(s >= pend[E], jnp.minimum(tail_blk, n_blocks - 1),
                         jnp.maximum(_prefix_sum(comp.astype(jnp.int32)) - 1, 0))
    comp_e = jnp.clip(ph - 1, 0, E - 1)
    load_e = jnp.minimum(ph, E - 1)
    load_c = jnp.where(ph >= E, NC - 1, jnp.minimum(t, NC - 1))
    flags = (comp.astype(jnp.int32) + 2 * load.astype(jnp.int32)
             + 4 * ((ph - 1) % 2 == 1).astype(jnp.int32) + 8 * fill.astype(jnp.int32))
    return comp_blk.astype(jnp.int32), comp_e.astype(jnp.int32), load_e, load_c.astype(jnp.int32), flags


def expert_ffn(xbuf, nblk, w1, b1, w2, b2, layer):
    rows, D = xbuf.shape
    R = MOE_ROWS
    NC = MOE_W_CHUNKS
    n_blocks = rows // R
    H2 = 2 * D_EXPERT
    C1, C2 = D // NC, D_EXPERT // NC
    sched = _expert_schedule(nblk, n_blocks)
    n_steps = sched[0].shape[0]
    grid_spec = pltpu.PrefetchScalarGridSpec(
        num_scalar_prefetch=5,
        grid=(n_steps,),
        in_specs=[pl.BlockSpec((R, D), lambda s, cb, ce, le, lc, fl: (cb[s], 0)),
                  pl.BlockSpec((None, None, C1, H2), lambda s, cb, ce, le, lc, fl: (layer, le[s], lc[s], 0)),
                  pl.BlockSpec((None, None, C2, D), lambda s, cb, ce, le, lc, fl: (layer, le[s], lc[s], 0)),
                  pl.BlockSpec((1, 1, H2), lambda s, cb, ce, le, lc, fl: (ce[s], 0, 0)),
                  pl.BlockSpec((1, 1, D), lambda s, cb, ce, le, lc, fl: (ce[s], 0, 0))],
        out_specs=pl.BlockSpec((R, D), lambda s, cb, ce, le, lc, fl: (cb[s], 0)),
        scratch_shapes=[pltpu.VMEM((D, H2), BF16), pltpu.VMEM((D_EXPERT, D), BF16),
                        pltpu.VMEM((D, H2), BF16), pltpu.VMEM((D_EXPERT, D), BF16)],
    )
    return pl.pallas_call(
        _expert_kernel,
        grid_spec=grid_spec,
        out_shape=jax.ShapeDtypeStruct((rows, D), BF16),
        compiler_params=_cparams("arbitrary"),
        name="expert_ffn",
    )(*sched, xbuf, w1, w2, b1.reshape(N_EXPERTS, 1, -1), b2.reshape(N_EXPERTS, 1, -1))


def moe_layer(h32, h16, w_r, b_r, w1, b1, w2, b2, ln_g, ln_b, layer):
    T, D = h32.shape
    R = MOE_ROWS
    M = T * TOP_K
    gates, top_idx, rank, counts = route_tokens(h32, w_r, b_r)
    padded = (counts + R - 1) // R * R
    pad_end = _prefix_sum(padded)
    pad_start = pad_end - padded
    pos = rank + pad_start[top_idx]
    n_blocks = (M + N_EXPERTS * (R - 1) + R - 1) // R
    rows = n_blocks * R
    _, tok_sorted = lax.sort_key_val(pos.reshape(M), jnp.arange(M, dtype=jnp.int32) % T)
    blk_row0 = jnp.arange(n_blocks, dtype=jnp.int32) * R
    blk_e = jnp.minimum(jnp.sum(pad_end[None, :] <= blk_row0[:, None], axis=1), N_EXPERTS - 1)
    first = _prefix_sum(counts) - counts
    row = blk_row0[:, None] + jnp.arange(R, dtype=jnp.int32)[None, :]
    in_e = row - pad_start[blk_e][:, None]
    src_tok = jnp.where(in_e < counts[blk_e][:, None],
                        tok_sorted[jnp.minimum(first[blk_e][:, None] + in_e, M - 1)],
                        row % T).reshape(rows)
    xbuf = h16[src_tok]
    ybuf = expert_ffn(xbuf, padded // R, w1, b1, w2, b2, layer)
    y4 = ybuf[pos]
    return combine_residual_ln(y4, gates, h32, ln_g, ln_b)


MOBA_GROUP = 4
MOBA_GATE_ROWS = 8


def _col_groups(x, op):
    rows, cols = x.shape
    return op(x.reshape(rows // 8, 8, cols), axis=0)


def _moba_kernel(q_ref, k_ref, v_ref, o_ref, kmean_ref, vt_ref, bias_ref):
    pg = pl.program_id(1)
    cur = pl.program_id(2)
    BLK = MOBA_BLOCK
    GP = MOBA_GROUP
    NR = MOBA_GATE_ROWS
    nb = k_ref.shape[0] // BLK
    assert nb <= NR
    lane = lax.broadcasted_iota(jnp.int32, (1, LANES), 1)
    LOG2E = math.log2(math.e)
    c2 = MOBA_HEAD_DIM ** -0.5 * LOG2E

    @pl.when(cur == 0)
    def _():
        kmean_ref[...] = jnp.zeros_like(kmean_ref)
        for g in range(GP):
            for n in range(nb):
                rows = slice(n * BLK, (n + 1) * BLK)
                kmean_ref[g, n:n + 1, :] = jnp.mean(k_ref[rows, g * LANES:(g + 1) * LANES], axis=0,
                                                    keepdims=True)
                vt_ref[g, :, rows] = v_ref[rows, g * LANES:(g + 1) * LANES].T.astype(BF16)

    Q2 = 2 * BLK
    key = lax.broadcasted_iota(jnp.int32, (BLK, Q2), 0)
    qry = lax.broadcasted_iota(jnp.int32, (BLK, Q2), 1) % BLK
    krow = lax.broadcasted_iota(jnp.int32, (BLK, LANES), 0)
    klane = lax.broadcasted_iota(jnp.int32, (BLK, LANES), 1)
    k_extra = jnp.where(klane < 2, krow, 0).astype(F32).astype(BF16)
    srow = lax.broadcasted_iota(jnp.int32, (LANES, BLK), 0)
    blk_row = lax.broadcasted_iota(jnp.int32, (NR, BLK), 0)

    qts = []
    for g in range(GP):
        q = q_ref[:, g * LANES:(g + 1) * LANES]
        halves, biases = [], []
        for hh in range(2):
            own = (lane // MOBA_HEAD_DIM) == hh
            qm_t = jnp.where(own, q, 0.0).T
            head_idx = 2 * (pg * GP + g) + hh + 1
            sl2 = jnp.exp(jnp.full((1, 1), head_idx, jnp.int32).astype(F32)
                          * (-8.0 * math.log(2.0) / MOBA_HEADS)) * LOG2E
            s_hi = sl2.astype(BF16).astype(F32)
            s_lo = (sl2 - s_hi).astype(BF16).astype(F32)
            q_extra = jnp.where(srow == 0, s_hi, jnp.where(srow == 1, s_lo, 0.0))
            halves.append(jnp.concatenate([qm_t * c2, q_extra], axis=0).astype(BF16))
            gate = _dot(kmean_ref[g], qm_t, precision=HIGHEST)
            rank = jnp.zeros((NR, BLK), jnp.int32)
            for m in range(nb):
                gm = gate[m:m + 1, :]
                ahead = (gm > gate) | ((gm == gate) & (m < blk_row))
                rank = rank + jnp.where(ahead, 1, 0) * (m < cur).astype(jnp.int32)
            sel = (rank < MOBA_TOPK) & (blk_row < cur)
            biases.append(jnp.where(sel, -sl2 * ((cur - blk_row) * BLK).astype(F32), NEG_INF))
        qts.append(jnp.concatenate(halves, axis=1))
        bias_ref[g] = jnp.concatenate(biases, axis=1)

    def scores(g, off):
        kc = jnp.concatenate([k_ref[pl.ds(off, BLK), g * LANES:(g + 1) * LANES].astype(BF16), k_extra],
                             axis=1)
        return _dot(kc, qts[g])

    def softmax_step(t, m_run, l_run, bias):
        if bias is None:
            t = jnp.where(key <= qry, t, NEG_INF)
            m_new = jnp.maximum(m_run, jnp.max(_col_groups(t, jnp.max), axis=0, keepdims=True))
            shift = m_new
        else:
            m_new = jnp.maximum(m_run, jnp.max(_col_groups(t, jnp.max), axis=0, keepdims=True) + bias)
            shift = m_new - bias
        alpha = jnp.exp2(m_run - m_new)
        pr = jnp.exp2(t - shift)
        l_new = alpha * l_run + _col_groups(pr, jnp.sum)
        return m_new, l_new, alpha, pr.astype(BF16)

    def block(g, n, carry, bias):
        m_run, l_run, acc = carry
        off = pl.multiple_of(n * BLK, BLK)
        m_new, l_new, alpha, pr = softmax_step(scores(g, off), m_run, l_run, bias)
        return m_new, l_new, alpha * acc + _dot(vt_ref[g, :, pl.ds(off, BLK)], pr)

    init = (jnp.full((1, Q2), NEG_INF, F32), jnp.zeros((8, Q2), F32), jnp.zeros((LANES, Q2), F32))
    carry = tuple(block(g, cur, init, None) for g in range(GP))

    def body(n, carry):
        return tuple(block(g, n, carry[g], bias_ref[g, pl.ds(n, 1), :]) for g in range(GP))

    carry = lax.fori_loop(0, cur, body, carry)
    row128 = lax.broadcasted_iota(jnp.int32, (LANES, 1), 0)
    for g in range(GP):
        m_run, l_run, acc = carry[g]
        out = acc / jnp.sum(l_run, axis=0, keepdims=True)
        out_t = jnp.where(row128 < MOBA_HEAD_DIM, out[:, :BLK], out[:, BLK:])
        o_ref[:, g * LANES:(g + 1) * LANES] = out_t.T.astype(o_ref.dtype)


def moba_attention(proj, B, S):
    T = B * S
    BLK = MOBA_BLOCK
    GP = MOBA_GROUP
    W = GP * LANES
    n_grp = MOBA_DIM // W
    nq = S // BLK
    return pl.pallas_call(
        _moba_kernel,
        grid=(B, n_grp, nq),
        in_specs=[pl.BlockSpec((BLK, W), lambda b, p, c: (b * nq + c, p)),
                  pl.BlockSpec((S, W), lambda b, p, c: (b, n_grp + p)),
                  pl.BlockSpec((S, W), lambda b, p, c: (b, 2 * n_grp + p))],
        out_specs=pl.BlockSpec((BLK, W), lambda b, p, c: (b * nq + c, p)),
        out_shape=jax.ShapeDtypeStruct((T, MOBA_DIM), BF16),
        scratch_shapes=[pltpu.VMEM((GP, MOBA_GATE_ROWS, LANES), F32),
                        pltpu.VMEM((GP, LANES, S), BF16),
                        pltpu.VMEM((GP, MOBA_GATE_ROWS, 2 * BLK), F32)],
        compiler_params=_cparams("parallel", "parallel", "arbitrary"),
        name="moba_attention",
    )(proj, proj, proj)


def _rwkv_prep_kernel(has_vres, S, *refs):
    if has_vres:
        (pm_ref, pl_ref, pm_prev_ref, pl_prev_ref, mu_m_ref, mu_l_ref, w0_ref, w2_ref, a0_ref, a2_ref,
         g2_ref, kk_ref, ka_ref, vfirst_ref, v0_ref, v1_ref, v2_ref,
         r_o, lw_o, k_o, v_o, kkn_o, a_o, g_o) = refs
    else:
        (pm_ref, pl_ref, pm_prev_ref, pl_prev_ref, mu_m_ref, mu_l_ref, w0_ref, w2_ref, a0_ref, a2_ref,
         g2_ref, kk_ref, ka_ref,
         r_o, lw_o, k_o, v_o, kkn_o, a_o, g_o) = refs
    i = pl.program_id(0)
    tm = pm_ref.shape[0]
    C = RWKV_DIM
    row = lax.broadcasted_iota(jnp.int32, (tm, 1), 0)
    seq_start = (i * tm) % S == 0

    def shifted(cur_ref, prev_ref, mu_ref):
        x = cur_ref[...]
        prev_row = jnp.where(seq_start, 0.0, prev_ref[7:8, :])
        xs = jnp.where(row == 0, prev_row, pltpu.roll(x, 1, 0))
        return x + (xs - x) * mu_ref[...]

    pm = shifted(pm_ref, pm_prev_ref, mu_m_ref)
    plo = shifted(pl_ref, pl_prev_ref, mu_l_ref)
    r = pm[:, :C]
    k = pm[:, C:2 * C]
    v = pm[:, 2 * C:]
    wd = plo[:, :RWKV_DECAY_LORA]
    ad = plo[:, RWKV_DECAY_LORA:RWKV_DECAY_LORA + RWKV_A_LORA]
    gd = plo[:, RWKV_DECAY_LORA + RWKV_A_LORA:RWKV_LORA]
    w = -jax.nn.softplus(-(w0_ref[...] + _dot(jnp.tanh(wd).astype(BF16), w2_ref[...]))) - 0.5
    a = jax.nn.sigmoid(a0_ref[...] + _dot(ad.astype(BF16), a2_ref[...]))
    g = _dot(jax.nn.sigmoid(gd).astype(BF16), g2_ref[...])
    if has_vres:
        lo = _dot(_dot(v.astype(BF16), v1_ref[...]).astype(BF16), v2_ref[...])
        v = v + (vfirst_ref[...] - v) * jax.nn.sigmoid(v0_ref[...] + lo)
    kk = k * kk_ref[...]
    hid_r = lax.broadcasted_iota(jnp.int32, (LANES, LANES), 0) // RWKV_HEAD_DIM
    hid_c = lax.broadcasted_iota(jnp.int32, (LANES, LANES), 1) // RWKV_HEAD_DIM
    ones_bd = (hid_r == hid_c).astype(F32)
    for pp in range(C // LANES):
        sl = slice(pp * LANES, (pp + 1) * LANES)
        kkp = kk[:, sl]
        ss = _dot(kkp * kkp, ones_bd, precision=HIGHEST)
        kkn_o[:, sl] = kkp * lax.rsqrt(jnp.maximum(ss, 1e-24))
    r_o[...] = r
    lw_o[...] = -jnp.exp(w)
    k_o[...] = k * (1.0 + (a - 1.0) * ka_ref[...])
    v_o[...] = v
    a_o[...] = a
    g_o[...] = g


def rwkv_prep(proj, plora, S, mu, w0, w2, a0, a2, g2, k_k, k_a, v_first, v_lora, tm=256):
    T = proj.shape[0]
    C = RWKV_DIM
    LP = plora.shape[1]
    has_vres = v_lora is not None
    mu_m = mu[:3 * C].reshape(1, 3 * C)
    mu_l = jnp.zeros((1, LP), F32).at[0, :RWKV_LORA].set(mu[3 * C:])
    row = lambda z: z.reshape(1, -1)
    full = lambda z: pl.BlockSpec(z.shape, lambda i: (0,) * z.ndim)
    args = [proj, plora, proj, plora, mu_m, mu_l, row(w0), w2.astype(BF16), row(a0), a2.astype(BF16),
            g2.astype(BF16), row(k_k), row(k_a)]
    in_specs = [pl.BlockSpec((tm, 3 * C), lambda i: (i, 1)),
                pl.BlockSpec((tm, LP), lambda i: (i, 0)),
                pl.BlockSpec((8, 3 * C), lambda i: (jnp.maximum(i * (tm // 8) - 1, 0), 1)),
                pl.BlockSpec((8, LP), lambda i: (jnp.maximum(i * (tm // 8) - 1, 0), 0))]
    in_specs += [full(z) for z in args[4:]]
    if has_vres:
        v0, v1, v2 = v_lora
        extra = [v_first, row(v0), v1.astype(BF16), v2.astype(BF16)]
        args += extra
        in_specs += [pl.BlockSpec((tm, C), lambda i: (i, 0))] + [full(z) for z in extra[1:]]
    out_spec = pl.BlockSpec((tm, C), lambda i: (i, 0))
    return pl.pallas_call(
        functools.partial(_rwkv_prep_kernel, has_vres, S),
        grid=(T // tm,),
        in_specs=in_specs,
        out_specs=[out_spec] * 7,
        out_shape=[jax.ShapeDtypeStruct((T, C), F32)] * 7,
        compiler_params=_cparams("parallel"),
        name="rwkv_prep",
    )(*args)


def _split3_bf16(x):
    x0 = x.astype(BF16)
    r1 = x - x0.astype(F32)
    x1 = r1.astype(BF16)
    x2 = (r1 - x1.astype(F32)).astype(BF16)
    return x0, x1, x2


def _sum01_left(m01, x):
    x0, x1, x2 = _split3_bf16(x)
    return _dot(m01, x0) + (_dot(m01, x1) + _dot(m01, x2))


def _sum01_right(x, m01):
    x0, x1, x2 = _split3_bf16(x)
    return _dot(x0, m01) + (_dot(x1, m01) + _dot(x2, m01))


_NN = (((1,), (0,)), ((), ()))
_NT = (((1,), (1,)), ((), ()))
_TN = (((0,), (0,)), ((), ()))


def _mm(a, b, passes, dims=_NN):
    dg = lambda x, y: lax.dot_general(x, y, dims, preferred_element_type=F32)
    if passes == 1:
        return dg(a.astype(BF16), b.astype(BF16))
    a_hi = a.astype(BF16)
    a_lo = (a - a_hi.astype(F32)).astype(BF16)
    b_hi = b.astype(BF16)
    b_lo = (b - b_hi.astype(F32)).astype(BF16)
    return dg(a_hi, b_hi) + (dg(a_hi, b_lo) + dg(a_lo, b_hi))


RWKV_PASSES = dict(gram=1, inv=1, apply=1, state=1)


def _rwkv_scan_kernel(r_ref, lw_ref, k_ref, v_ref, kk_ref, a_ref, g_ref, rk_ref, lnw_ref, lnb_ref,
                      o_ref, state_ref):
    c = pl.program_id(1)
    CH = RWKV_CHUNK
    N = RWKV_HEAD_DIM
    P2 = 2 * CH
    NP = RWKV_DIM // LANES
    pg, pi, pa, ps = (RWKV_PASSES[n] for n in ("gram", "inv", "apply", "state"))

    @pl.when(c == 0)
    def _():
        state_ref[...] = jnp.zeros_like(state_ref)

    lane = lax.broadcasted_iota(jnp.int32, (1, LANES), 1)
    head0 = lane < N
    ri = lax.broadcasted_iota(jnp.int32, (P2, P2), 0)
    ci = lax.broadcasted_iota(jnp.int32, (P2, P2), 1)
    same_head = (ri // CH) == (ci // CH)
    strict = (ri % CH) > (ci % CH)
    incl = (ri % CH) >= (ci % CH)
    eye = (ri == ci).astype(F32)
    ones_bd = same_head.astype(BF16)
    ti = lax.broadcasted_iota(jnp.int32, (CH, CH), 0)
    si = lax.broadcasted_iota(jnp.int32, (CH, CH), 1)
    tril_incl = (ti >= si).astype(BF16)

    def stack(x):
        return jnp.concatenate([jnp.where(head0, x, 0.0), jnp.where(head0, 0.0, x)], axis=0)

    pairs = range(NP)
    sls = [slice(p * LANES, (p + 1) * LANES) for p in pairs]
    r = [r_ref[:, sl] for sl in sls]
    k = [k_ref[:, sl] for sl in sls]
    v = [v_ref[:, sl] for sl in sls]
    kk = [kk_ref[:, sl] for sl in sls]
    lw = [lw_ref[:, sl] for sl in sls]
    cum = [_sum01_left(tril_incl, lw[p]) for p in pairs]
    cum_end = [cum[p][CH - 1:CH, :] for p in pairs]
    b = [kk[p] * a_ref[:, sls[p]] for p in pairs]
    e_neg = [jnp.exp(-cum[p]) for p in pairs]
    e_end = [jnp.exp(cum_end[p] - cum[p]) for p in pairs]
    A_st = [stack(-kk[p] * jnp.exp(cum[p] - lw[p])) for p in pairs]
    R_st = [stack(r[p] * jnp.exp(cum[p])) for p in pairs]
    BK = [jnp.concatenate([stack(b[p] * e_neg[p]), stack(k[p] * e_neg[p])], axis=0) for p in pairs]
    Bend_st = [stack(b[p] * e_end[p]) for p in pairs]
    Kend_st = [stack(k[p] * e_end[p]) for p in pairs]
    V_st = [stack(v[p]) for p in pairs]

    G = [_mm(jnp.concatenate([A_st[p], R_st[p]], axis=0), BK[p], pg, _NT) for p in pairs]
    Lab = [jnp.where(strict, G[p][:P2, :P2], 0.0) for p in pairs]
    Lak = [jnp.where(strict, G[p][:P2, P2:], 0.0) for p in pairs]
    Mrb = [jnp.where(incl, G[p][P2:, :P2], 0.0) for p in pairs]
    Mrk = [jnp.where(incl, G[p][P2:, P2:], 0.0) for p in pairs]

    Tinv = [eye + Lab[p] for p in pairs]
    Lp = Lab
    for _ in range(int(math.log2(CH)) - 1):
        Lp = [_mm(Lp[p], Lp[p], pi) for p in pairs]
        Tinv = [Tinv[p] + _mm(Tinv[p], Lp[p], pi) for p in pairs]

    LakV = [_mm(Lak[p], V_st[p], pa) for p in pairs]
    AU = [_mm(Tinv[p], jnp.concatenate([A_st[p], LakV[p]], axis=1), pa) for p in pairs]
    MM = [_mm(Mrb[p], AU[p], pa) for p in pairs]
    MV = [_mm(Mrk[p], V_st[p], pa) for p in pairs]
    BT = [_mm(Bend_st[p], AU[p], pa, _TN) for p in pairs]
    KV = [_mm(Kend_st[p], V_st[p], pa, _TN) for p in pairs]
    Rhat = [R_st[p] + MM[p][:, :LANES] for p in pairs]
    Mmat = [eye * jnp.exp(cum_end[p]) + BT[p][:, :LANES] for p in pairs]

    RS = [_mm(jnp.concatenate([Rhat[p], Mmat[p]], axis=0), state_ref[p], ps) for p in pairs]
    for p in pairs:
        state_ref[p] = jnp.where(same_head, RS[p][P2:] + BT[p][:, LANES:] + KV[p], 0.0)
    Y_st = [RS[p][:P2] + MM[p][:, LANES:] + MV[p] for p in pairs]
    y = [Y_st[p][:CH] + Y_st[p][CH:] for p in pairs]

    sums = [_sum01_right(jnp.concatenate([y[p], r[p] * k[p] * rk_ref[:, sls[p]]], axis=0), ones_bd)
            for p in pairs]
    yc = [y[p] - sums[p][:CH] * (1.0 / N) for p in pairs]
    var = [_sum01_right(yc[p] * yc[p], ones_bd) * (1.0 / N) for p in pairs]
    for p in pairs:
        yn = yc[p] * lax.rsqrt(var[p] + RWKV_GN_EPS) * lnw_ref[:, sls[p]] + lnb_ref[:, sls[p]]
        o_ref[:, sls[p]] = ((yn + sums[p][CH:] * v[p]) * g_ref[:, sls[p]]).astype(o_ref.dtype)


def rwkv_scan(r, lw, k, v, kk, a, g, r_k, ln_w, ln_b, B, S):
    T, C = r.shape
    CH = RWKV_CHUNK
    nc = S // CH
    blk = pl.BlockSpec((CH, C), lambda b, c: (b * nc + c, 0))
    par = pl.BlockSpec((1, C), lambda b, c: (0, 0))
    return pl.pallas_call(
        _rwkv_scan_kernel,
        grid=(B, nc),
        in_specs=[blk] * 7 + [par] * 3,
        out_specs=blk,
        out_shape=jax.ShapeDtypeStruct((T, C), BF16),
        scratch_shapes=[pltpu.VMEM((C // LANES, LANES, LANES), F32)],
        compiler_params=_cparams("parallel", "arbitrary"),
        name="rwkv_scan",
    )(r, lw, k, v, kk, a, g, r_k.reshape(1, C), ln_w.reshape(1, C), ln_b.reshape(1, C))


def _rope_pairs(x, cos2, sin2):
    lane = lax.broadcasted_iota(jnp.int32, (1, LANES), 1)
    first_half = (lane % MLA_ROPE_DIM) < (MLA_ROPE_DIM // 2)
    partner = jnp.where(first_half, pltpu.roll(x, LANES - MLA_ROPE_DIM // 2, 1),
                        pltpu.roll(x, MLA_ROPE_DIM // 2, 1))
    return x * cos2 + partner * sin2


def _mla_cproj_kernel(h_ref, w_ref, cos_ref, sin_ref, c_ref, kr_ref):
    acc = _dot(h_ref[...], w_ref[...])
    NC = MLA_Q_LORA + MLA_KV_LORA
    c_ref[...] = acc[:, :NC]
    kr_ref[...] = _rope_pairs(acc[:, NC:], cos_ref[...], sin_ref[...]).astype(kr_ref.dtype)


def mla_cproj(h16, w_in, cos2, sin2, S, tm=512):
    T, D = h16.shape
    NC = MLA_Q_LORA + MLA_KV_LORA
    w = jnp.concatenate([w_in, w_in[:, NC:]], axis=1).astype(BF16)
    ns = S // tm
    return pl.pallas_call(
        _mla_cproj_kernel,
        grid=(T // tm,),
        in_specs=[pl.BlockSpec((tm, D), lambda i: (i, 0)),
                  pl.BlockSpec((D, NC + LANES), lambda i: (0, 0)),
                  pl.BlockSpec((tm, LANES), lambda i: (i % ns, 0)),
                  pl.BlockSpec((tm, LANES), lambda i: (i % ns, 0))],
        out_specs=[pl.BlockSpec((tm, NC), lambda i: (i, 0)),
                   pl.BlockSpec((tm, LANES), lambda i: (i, 0))],
        out_shape=[jax.ShapeDtypeStruct((T, NC), F32), jax.ShapeDtypeStruct((T, LANES), BF16)],
        compiler_params=_cparams("parallel"),
        name="mla_cproj",
    )(h16, w, cos2, sin2)


def _rms_up_kernel(rope_from, out_scale, c_ref, g_ref, w_ref, cos_ref, sin_ref, o_ref):
    x = c_ref[...]
    ms = jnp.mean(x * x, axis=-1, keepdims=True)
    cn = (x * lax.rsqrt(ms + RMS_EPS) * g_ref[...]).astype(BF16)
    N = w_ref.shape[1]
    plain = N if rope_from is None else rope_from
    TN = 1024
    for n0 in range(0, plain, TN):
        acc = _dot(cn, w_ref[:, n0:n0 + TN])
        if out_scale != 1.0:
            acc = acc * out_scale
        o_ref[:, n0:n0 + TN] = acc.astype(o_ref.dtype)
    if rope_from is not None:
        cos2, sin2 = cos_ref[...], sin_ref[...]
        acc = _dot(cn, w_ref[:, rope_from:]) * out_scale
        for gidx in range(acc.shape[1] // LANES):
            sl = slice(gidx * LANES, (gidx + 1) * LANES)
            o_ref[:, rope_from + gidx * LANES:rope_from + (gidx + 1) * LANES] = (
                _rope_pairs(acc[:, sl], cos2, sin2).astype(o_ref.dtype))


def rms_up_proj(c, col_block, gain, w, cos2, sin2, S, rope_from, out_scale=1.0, tm=512):
    T = c.shape[0]
    K, N = w.shape
    ns = S // tm
    return pl.pallas_call(
        functools.partial(_rms_up_kernel, rope_from, out_scale),
        grid=(T // tm,),
        in_specs=[pl.BlockSpec((tm, K), lambda i: (i, col_block)),
                  pl.BlockSpec((1, K), lambda i: (0, 0)),
                  pl.BlockSpec((K, N), lambda i: (0, 0)),
                  pl.BlockSpec((tm, LANES), lambda i: (i % ns, 0)),
                  pl.BlockSpec((tm, LANES), lambda i: (i % ns, 0))],
        out_specs=pl.BlockSpec((tm, N), lambda i: (i, 0)),
        out_shape=jax.ShapeDtypeStruct((T, N), BF16),
        compiler_params=_cparams("parallel"),
        name="rms_up_proj",
    )(c, gain.reshape(1, K), w, cos2, sin2)


MLA_GROUP = 4
MLA_SCORE_SCALE = (MLA_NOPE_DIM + MLA_ROPE_DIM) ** -0.5 * math.log2(math.e)


def _mla_attn_kernel(qn_ref, qr_ref, kv_ref, kr_ref, o_ref, vt_ref):
    qi = pl.program_id(2)
    TQ = qn_ref.shape[0]
    S = kv_ref.shape[0]
    G = MLA_GROUP
    lane = lax.broadcasted_iota(jnp.int32, (1, LANES), 1)

    @pl.when(qi == 0)
    def _():
        for j in range(G):
            for n in range(S // TQ):
                rows = slice(n * TQ, (n + 1) * TQ)
                v_t = kv_ref[rows, (2 * j + 1) * LANES:(2 * j + 2) * LANES]
                vt_ref[j, :, rows] = v_t.astype(F32).T.astype(BF16)

    key = lax.broadcasted_iota(jnp.int32, (TQ, TQ), 0)
    qry = lax.broadcasted_iota(jnp.int32, (TQ, TQ), 1)
    qts = []
    for j in range(G):
        own = (lane // MLA_ROPE_DIM) == (j % 2)
        qr = qr_ref[:, (j // 2) * LANES:(j // 2 + 1) * LANES]
        qc = jnp.concatenate([qn_ref[:, j * LANES:(j + 1) * LANES],
                              jnp.where(own, qr, jnp.zeros_like(qr))], axis=1)
        qts.append(qc.astype(F32).T.astype(BF16))

    def scores(j, off):
        kc = jnp.concatenate([kv_ref[pl.ds(off, TQ), 2 * j * LANES:(2 * j + 1) * LANES],
                              kr_ref[pl.ds(off, TQ), :]], axis=1)
        return _dot(kc, qts[j])

    def softmax_step(t, m_run, l_run, diag):
        if diag:
            t = jnp.where(key <= qry, t, NEG_INF)
        m_new = jnp.maximum(m_run, jnp.max(_col_groups(t, jnp.max), axis=0, keepdims=True))
        alpha = jnp.exp2(m_run - m_new)
        pr = jnp.exp2(t - m_new)
        l_new = alpha * l_run + _col_groups(pr, jnp.sum)
        return m_new, l_new, alpha, pr.astype(BF16)

    def weighted_values(j, off, acc, alpha, pr):
        return alpha * acc + _dot(vt_ref[j, :, pl.ds(off, TQ)], pr)

    def body(n, carry):
        off = pl.multiple_of(n * TQ, TQ)
        out = []
        for j in range(G):
            m_run, l_run, acc = carry[j]
            m_new, l_new, alpha, pr = softmax_step(scores(j, off), m_run, l_run, False)
            out.append((m_new, l_new, weighted_values(j, off, acc, alpha, pr)))
        return tuple(out)

    carry = tuple((jnp.full((1, TQ), NEG_INF, F32), jnp.zeros((8, TQ), F32),
                   jnp.zeros((LANES, TQ), F32)) for j in range(G))
    carry = lax.fori_loop(0, qi, body, carry)
    off_d = pl.multiple_of(qi * TQ, TQ)
    for j in range(G):
        m_run, l_run, acc = carry[j]
        m_new, l_new, alpha, pr = softmax_step(scores(j, off_d), m_run, l_run, True)
        acc = weighted_values(j, off_d, acc, alpha, pr)
        out_t = acc / jnp.sum(l_new, axis=0, keepdims=True)
        o_ref[:, j * LANES:(j + 1) * LANES] = out_t.T.astype(o_ref.dtype)


def mla_attention(q, kv, kr2, B, S, tq=512):
    T = B * S
    H, G = MLA_HEADS, MLA_GROUP
    nq = S // tq
    n_nope = H // G
    return pl.pallas_call(
        _mla_attn_kernel,
        grid=(B, H // G, nq),
        in_specs=[pl.BlockSpec((tq, G * LANES), lambda b, g, i: (b * nq + i, g)),
                  pl.BlockSpec((tq, G // 2 * LANES), lambda b, g, i: (b * nq + i, 2 * n_nope + g)),
                  pl.BlockSpec((S, 2 * G * LANES), lambda b, g, i: (b, g)),
                  pl.BlockSpec((S, LANES), lambda b, g, i: (b, 0))],
        out_specs=pl.BlockSpec((tq, G * LANES), lambda b, g, i: (b * nq + i, g)),
        out_shape=jax.ShapeDtypeStruct((T, H * MLA_V_DIM), BF16),
        scratch_shapes=[pltpu.VMEM((G, LANES, S), BF16)],
        compiler_params=_cparams("parallel", "parallel", "arbitrary"),
        name="mla_attention",
    )(q, q, kv, kr2)


def _rope_tables(S):
    half = MLA_ROPE_DIM // 2
    inv = ROPE_THETA ** (-jnp.arange(0, MLA_ROPE_DIM, 2, dtype=F32) / MLA_ROPE_DIM)
    ang = jnp.arange(S, dtype=F32)[:, None] * inv[None, :]
    cos, sin = jnp.cos(ang), jnp.sin(ang)
    reps = LANES // MLA_ROPE_DIM
    cos2 = jnp.tile(jnp.concatenate([cos, cos], axis=1), (1, reps))
    sin2 = jnp.tile(jnp.concatenate([-sin, sin], axis=1), (1, reps))
    assert cos2.shape == (S, LANES) and half * 2 * reps == LANES
    return cos2, sin2


def kernel(x, ev_w_in, ev_w_out, rw_mu, rw_w0, rw_w2, rw_a0, rw_a2, rw_g2, rw_k_k, rw_k_a, rw_r_k,
           rw_ln_w, rw_ln_b, rw_v0, rw_v1, rw_v2, od_w_in, od_q_norm, od_kv_norm, od_w_uq, od_w_ukv,
           od_w_out, ln_mix_g, ln_mix_b, ln_ffn_g, ln_ffn_b, moe_w_r, moe_b_r, moe_w1, moe_b1,
           moe_w2, moe_b2):
    B, S, D = x.shape
    T = B * S
    h32 = x.reshape(T, D)
    h16 = h32.astype(BF16)
    cos2, sin2 = _rope_tables(S)
    n_main = 3 * MOBA_DIM + 3 * RWKV_DIM
    lora_pad = -(-RWKV_LORA // LANES) * LANES
    qd = MLA_NOPE_DIM + MLA_ROPE_DIM
    perm = np.concatenate([
        (np.arange(MLA_HEADS)[:, None] * qd + np.arange(MLA_NOPE_DIM)[None, :]).reshape(-1),
        (np.arange(MLA_HEADS)[:, None] * qd + MLA_NOPE_DIM + np.arange(MLA_ROPE_DIM)[None, :]).reshape(-1)])
    v_first = None
    for layer in range(DEPTH):
        j = layer // 2
        if layer % 2 == 0:
            w_in = ev_w_in[j]
            w_main = w_in[:, :n_main].astype(BF16)
            w_lora = jnp.zeros((D, lora_pad), BF16).at[:, :RWKV_LORA].set(w_in[:, n_main:].astype(BF16))
            proj = matmul(h16, w_main, F32)
            plora = matmul(h16, w_lora, F32)
            a_out = moba_attention(proj, B, S)
            v_lora = None if j == 0 else (rw_v0[j - 1], rw_v1[j - 1], rw_v2[j - 1])
            r, lw, k, v, kk, a, g = rwkv_prep(proj, plora, S, rw_mu[j], rw_w0[j], rw_w2[j], rw_a0[j],
                                             rw_a2[j], rw_g2[j], rw_k_k[j], rw_k_a[j], v_first, v_lora)
            if j == 0:
                v_first = v
            b_out = rwkv_scan(r, lw, k, v, kk, a, g, rw_r_k[j], rw_ln_w[j], rw_ln_b[j], B, S)
            w_out = ev_w_out[j].astype(BF16)
            h32, h16 = proj_residual_ln([a_out, b_out], [w_out[:MOBA_DIM], w_out[MOBA_DIM:]], h32,
                                        ln_mix_g[layer], ln_mix_b[layer])
        else:
            c, kr2 = mla_cproj(h16, od_w_in[j], cos2, sin2, S)
            q = rms_up_proj(c, 0, od_q_norm[j], od_w_uq[j][:, perm].astype(BF16), cos2, sin2, S,
                            rope_from=MLA_HEADS * MLA_NOPE_DIM, out_scale=MLA_SCORE_SCALE)
            kv = rms_up_proj(c, 1, od_kv_norm[j], od_w_ukv[j].astype(BF16), cos2, sin2, S,
                             rope_from=None)
            o = mla_attention(q, kv, kr2, B, S)
            h32, h16 = proj_residual_ln([o], [od_w_out[j].astype(BF16)], h32,
                                        ln_mix_g[layer], ln_mix_b[layer])
        h32, h16 = moe_layer(h32, h16, moe_w_r[layer], moe_b_r[layer], moe_w1, moe_b1[layer], moe_w2,
                             moe_b2[layer], ln_ffn_g[layer], ln_ffn_b[layer], layer)
    return h32.reshape(B, S, D)
```
